```python
import math
import jax
import jax.numpy as jnp
from jax import lax
import numpy as np

D_MODEL = 1024
BATCH = 8
SEQ = 2048
DEPTH = 4
DEC_BATCH = 128
DEC_SEQ = 4
PAST_LEN = 2048
PAGE_SIZE = 128

N_MIXERS = 3
N_A = (DEPTH + 2) // 3
N_B = (DEPTH + 1) // 3
N_C = DEPTH // 3
D_FF = 2816
NORM_EPS = 1e-6

RWKV_HEAD_DIM = 64
RWKV_HEADS = D_MODEL // RWKV_HEAD_DIM
RWKV_DECAY_LORA = 64
RWKV_AAA_LORA = 64
RWKV_MV_LORA = 32
RWKV_GATE_LORA = 160
RWKV_LNX_EPS = 64e-5

DIFF_HEADS = 8
DIFF_HEAD_DIM = 64
DIFF_E = 2 * DIFF_HEAD_DIM
DIFF_WIDTH = DIFF_HEADS * DIFF_E
SUBLN_EPS = 1e-5
Q_BLOCK = 128
REL_BUCKETS = 32
REL_MAX_EXACT = 16
REL_MAX_DIST = 128

GDN_HEADS = 8
GDN_DK = 128
GDN_DV = 128
GDN_CONV = 4
GDN_CHUNK = 64
GDN_QKV = GDN_HEADS * (2 * GDN_DK + GDN_DV)
GDN_IN = GDN_QKV + GDN_HEADS * GDN_DV + 2 * GDN_HEADS

kernel_name = 'hybrid_rwkv7_diffattn_gdn_macaron_step'


def rmsnorm(x, w, eps=NORM_EPS):
    xf = x.astype(jnp.float32)
    y = xf * lax.rsqrt(jnp.mean(xf * xf, axis=-1, keepdims=True) + eps)
    return (y * w.astype(jnp.float32)).astype(x.dtype)


def l2norm(t, eps):
    return t * lax.rsqrt(jnp.sum(t * t, axis=-1, keepdims=True) + eps)


def macaron_half(x, g, w_gate, w_up, w_down):
    h = rmsnorm(x, g)
    return x + 0.5 * ((jax.nn.silu(h @ w_gate) * (h @ w_up)) @ w_down)


def rwkv7_mix(u, shift_prev, S0, p, v_first, vres):
    B, T, D = u.shape
    H, N = RWKV_HEADS, RWKV_HEAD_DIM
    f32 = jnp.float32
    prev = jnp.concatenate([shift_prev[:, None, :].astype(u.dtype), u[:, :-1]], axis=1)
    xx = prev - u
    mu = p['mu']
    xr, xw, xk = u + xx * mu[0], u + xx * mu[1], u + xx * mu[2]
    xv, xa, xg = u + xx * mu[3], u + xx * mu[4], u + xx * mu[5]
    r = xr @ p['w_rkv'][0]
    k = xk @ p['w_rkv'][1]
    v = xv @ p['w_rkv'][2]
    wlog = -jax.nn.softplus(-(p['w0'] + jnp.tanh(xw @ p['w1']) @ p['w2']).astype(f32)) - 0.5
    decay = jnp.exp(-jnp.exp(wlog))
    if vres is None:
        v_first = v
    else:
        v0, v1, v2 = vres
        v = v + (v_first - v) * jax.nn.sigmoid(v0 + (xv @ v1) @ v2)
    a = jax.nn.sigmoid((p['a0'] + (xa @ p['a1']) @ p['a2']).astype(f32))
    gate = jax.nn.sigmoid(xg @ p['g1']) @ p['g2']

    def heads(t):
        return t.astype(f32).reshape(B, T, H, N)

    kk = l2norm(heads(k * p['k_k']), 1e-24)
    k_mod = k.astype(f32) * (1.0 + (a - 1.0) * p['k_a'].astype(f32))
    r_h, w_h, k_h, v_h, a_h = heads(r), heads(decay), heads(k_mod), heads(v), heads(a)
    seq = tuple(jnp.moveaxis(t, 1, 0) for t in (r_h, w_h, k_h, v_h, -kk, kk * a_h))

    def step(S, inp):
        r_t, w_t, k_t, v_t, a_t, b_t = inp
        sa = jnp.einsum('bhvk,bhk->bhv', S, a_t)
        S = S * w_t[:, :, None, :] + sa[..., None] * b_t[:, :, None, :] + v_t[..., None] * k_t[:, :, None, :]
        return S, jnp.einsum('bhvk,bhk->bhv', S, r_t)

    S, o = lax.scan(step, S0.astype(f32), seq)
    o = jnp.moveaxis(o, 0, 1)
    mean = jnp.mean(o, axis=-1, keepdims=True)
    var = jnp.mean(jnp.square(o - mean), axis=-1, keepdims=True)
    o = ((o - mean) * lax.rsqrt(var + RWKV_LNX_EPS)).reshape(B, T, D)
    o = o * p['lnx_w'].astype(f32) + p['lnx_b'].astype(f32)
    bonus = jnp.sum(r_h * k_h * p['r_k'].astype(f32), axis=-1, keepdims=True) * v_h
    o = (o + bonus.reshape(B, T, D)).astype(u.dtype) * gate
    return o @ p['w_o'], v_first, S, u[:, -1]


def t5_bucket(q_pos, k_pos):
    n = jnp.maximum(q_pos[:, None] - k_pos[None, :], 0)
    nf = jnp.maximum(n, 1).astype(jnp.float32)
    large = REL_MAX_EXACT + (jnp.log(nf / REL_MAX_EXACT) / math.log(REL_MAX_DIST / REL_MAX_EXACT)
                             * (REL_BUCKETS - REL_MAX_EXACT)).astype(jnp.int32)
    large = jnp.minimum(large, REL_BUCKETS - 1)
    return jnp.where(n < REL_MAX_EXACT, n, large)


def diff_sweep(q, k, v, q_pos, k_pos, lam, rel_bias):
    B, T, H, E = q.shape
    dh = E // 2
    qb = math.gcd(T, Q_BLOCK)
    nb = T // qb
    scale = dh ** -0.5
    kf = k.astype(jnp.float32)
    k1, k2 = kf[..., :dh], kf[..., dh:]
    vf = v.astype(jnp.float32)
    table = rel_bias.astype(jnp.float32)
    q_blocks = jnp.moveaxis(q.astype(jnp.float32).reshape(B, nb, qb, H, E), 1, 0)
    pos_blocks = q_pos.reshape(nb, qb)

    def one_block(args):
        qblk, qp = args
        bias = jnp.moveaxis(table[t5_bucket(qp, k_pos)], -1, 0)
        visible = k_pos[None, :] <= qp[:, None]
        s1 = jnp.einsum('bqhd,bshd->bhqs', qblk[..., :dh], k1) * scale + bias
        s2 = jnp.einsum('bqhd,bshd->bhqs', qblk[..., dh:], k2) * scale + bias
        p1 = jax.nn.softmax(jnp.where(visible, s1, -jnp.inf), axis=-1)
        p2 = jax.nn.softmax(jnp.where(visible, s2, -jnp.inf), axis=-1)
        return jnp.einsum('bhqs,bshe->bqhe', p1 - lam * p2, vf)

    o = lax.map(one_block, (q_blocks, pos_blocks))
    return jnp.moveaxis(o, 0, 1).reshape(B, T, H, E)


def diff_attention_mix(u, past_k, past_v, p, lam_init, rel_bias):
    B, T, _ = u.shape
    H, E = DIFF_HEADS, DIFF_E
    q, k, v = jnp.split(u @ p['w_qkv'], 3, axis=-1)
    q, k, v = q.reshape(B, T, H, E), k.reshape(B, T, H, E), v.reshape(B, T, H, E)
    if past_k is None:
        k_all, v_all = k, v
    else:
        k_all = jnp.concatenate([past_k.astype(k.dtype), k], axis=1)
        v_all = jnp.concatenate([past_v.astype(v.dtype), v], axis=1)
    past = k_all.shape[1] - T
    q_pos = past + jnp.arange(T, dtype=jnp.int32)
    k_pos = jnp.arange(past + T, dtype=jnp.int32)
    lv = p['lam'].astype(jnp.float32)
    lam = jnp.exp(jnp.sum(lv[0] * lv[1])) - jnp.exp(jnp.sum(lv[2] * lv[3])) + lam_init
    o = diff_sweep(q, k_all, v_all, q_pos, k_pos, lam, rel_bias)
    o = rmsnorm(o, p['subln_w'], SUBLN_EPS) * (1.0 - lam_init)
    return o.astype(u.dtype).reshape(B, T, H * E) @ p['w_o'], k, v


def gated_delta_chunked(q, k, v, beta, g, S0):
    B, T, H, DK = q.shape
    DV = v.shape[-1]
    L = math.gcd(T, GDN_CHUNK)
    n = T // L

    def to_chunks(t):
        t = t.reshape((B, n, L) + t.shape[2:])
        return jnp.moveaxis(jnp.moveaxis(t, 1, 0), 2, 3)

    qc, kc, vc, bc, gc = (to_chunks(t) for t in (q, k, v, beta, g))
    gc = jnp.cumsum(gc, axis=-1)
    diff = gc[..., :, None] - gc[..., None, :]
    idx = jnp.arange(L)
    strict = idx[:, None] > idx[None, :]
    incl = idx[:, None] >= idx[None, :]
    decay_strict = jnp.exp(jnp.where(strict, diff, -jnp.inf))
    decay_incl = jnp.exp(jnp.where(incl, diff, -jnp.inf))
    a_mat = bc[..., :, None] * jnp.einsum('nbhid,nbhjd->nbhij', kc, kc) * decay_strict + jnp.eye(L, dtype=q.dtype)
    rhs = jnp.concatenate([bc[..., None] * vc, (bc * jnp.exp(gc))[..., None] * kc], axis=-1)
    sol = lax.linalg.triangular_solve(a_mat, rhs, left_side=True, lower=True, unit_diagonal=True)
    wv, wk = sol[..., :DV], sol[..., DV:]
    aqk = jnp.einsum('nbhid,nbhjd->nbhij', qc, kc) * decay_incl
    qg = qc * jnp.exp(gc)[..., None]
    kd = kc * jnp.exp(gc[..., -1:] - gc)[..., None]
    glast = jnp.exp(gc[..., -1])

    def step(S, c):
        wv_c, wk_c, aqk_c, qg_c, kd_c, gl_c = c
        u = wv_c - jnp.einsum('bhlk,bhkv->bhlv', wk_c, S)
        o = jnp.einsum('bhlk,bhkv->bhlv', qg_c, S) + jnp.einsum('bhij,bhjv->bhiv', aqk_c, u)
        S = gl_c[..., None, None] * S + jnp.einsum('bhlk,bhlv->bhkv', kd_c, u)
        return S, o

    S, o = lax.scan(step, S0, (wv, wk, aqk, qg, kd, glast))
    o = jnp.swapaxes(jnp.moveaxis(o, 0, 1), 2, 3).reshape(B, T, H, DV)
    return o, S


def gdn_mix(u, conv_prev, S0, p):
    B, T, _ = u.shape
    H, DK, DV = GDN_HEADS, GDN_DK, GDN_DV
    f32 = jnp.float32
    proj = u @ p['w_in']
    qkv = proj[..., :GDN_QKV]
    z = proj[..., GDN_QKV:GDN_QKV + H * DV]
    b_raw = proj[..., GDN_QKV + H * DV:GDN_QKV + H * DV + H]
    a_raw = proj[..., GDN_QKV + H * DV + H:]
    xp = jnp.concatenate([conv_prev.astype(u.dtype), qkv], axis=1)
    conv = sum(p['conv_w'][j] * xp[:, j:j + T] for j in range(GDN_CONV))
    qkv_c = jax.nn.silu(conv).astype(f32)
    q = l2norm(qkv_c[..., :H * DK].reshape(B, T, H, DK), 1e-6) * (DK ** -0.5)
    k = l2norm(qkv_c[..., H * DK:2 * H * DK].reshape(B, T, H, DK), 1e-6)
    v = qkv_c[..., 2 * H * DK:].reshape(B, T, H, DV)
    beta = jax.nn.sigmoid(b_raw.astype(f32))
    g = -jnp.exp(p['a_log'].astype(f32)) * jax.nn.softplus(a_raw.astype(f32) + p['dt_bias'].astype(f32))
    o, S = gated_delta_chunked(q, k, v, beta, g, S0.astype(f32))
    o = rmsnorm(o, p['norm_w']) * jax.nn.silu(z.astype(f32).reshape(B, T, H, DV))
    out = o.astype(u.dtype).reshape(B, T, H * DV) @ p['w_o']
    return out, S, xp[:, -(GDN_CONV - 1):]


def setup_inputs(seed: int = 0) -> dict:
    key = jax.random.key(seed)
    keys = list(jax.random.split(key, 64))
    f32 = jnp.float32

    def nk():
        return keys.pop()

    def dense(shape, fan_in, scale=1.0):
        return jax.random.normal(nk(), shape, f32) * (scale / math.sqrt(fan_in))

    def small(shape, scale):
        return scale * jax.random.normal(nk(), shape, f32)

    def gain(shape, center=1.0):
        return center + 0.05 * jax.random.normal(nk(), shape, f32)

    n_pages = PAST_LEN // PAGE_SIZE
    n_used = DEC_BATCH * n_pages
    n_pool = n_used + max(1, n_used // 4)
    page_table = jax.random.permutation(nk(), n_pool)[:n_used].reshape(DEC_BATCH, n_pages).astype(jnp.int32)
    dt = jnp.exp(jax.random.uniform(nk(), (N_C, GDN_HEADS), f32, math.log(1e-3), math.log(1e-1)))
    nv = max(N_A - 1, 0)
    return {
        'x_prompt': small((BATCH, SEQ, D_MODEL), 1.0),
        'x_sample': small((DEC_BATCH, DEC_SEQ, D_MODEL), 1.0),
        'state_rwkv_wkv': small((N_A, DEC_BATCH, RWKV_HEADS, RWKV_HEAD_DIM, RWKV_HEAD_DIM), 0.5),
        'state_rwkv_shift': small((N_A, DEC_BATCH, D_MODEL), 1.0),
        'cache_attn_k': small((N_B, n_pool, PAGE_SIZE, DIFF_HEADS, DIFF_E), 1.0),
        'cache_attn_v': small((N_B, n_pool, PAGE_SIZE, DIFF_HEADS, DIFF_E), 1.0),
        'state_gdn': small((N_C, DEC_BATCH, GDN_HEADS, GDN_DK, GDN_DV), 0.1),
        'state_gdn_conv': small((N_C, DEC_BATCH, GDN_CONV - 1, GDN_QKV), 1.0),
        'page_table': page_table,
        'norm_w': gain((DEPTH, 3, D_MODEL)),
        'final_norm_w': gain((D_MODEL,)),
        'ffn_w_gate': dense((DEPTH, 2, D_MODEL, D_FF), D_MODEL),
        'ffn_w_up': dense((DEPTH, 2, D_MODEL, D_FF), D_MODEL),
        'ffn_w_down': dense((DEPTH, 2, D_FF, D_MODEL), D_FF),
        'rwkv_mu': jax.random.uniform(nk(), (N_A, 6, D_MODEL), f32),
        'rwkv_w_rkv': dense((N_A, 3, D_MODEL, D_MODEL), D_MODEL),
        'rwkv_w_o': dense((N_A, D_MODEL, D_MODEL), D_MODEL),
        'rwkv_w0': jax.random.uniform(nk(), (N_A, D_MODEL), f32, -6.0, -0.5),
        'rwkv_w1': dense((N_A, D_MODEL, RWKV_DECAY_LORA), D_MODEL),
        'rwkv_w2': dense((N_A, RWKV_DECAY_LORA, D_MODEL), RWKV_DECAY_LORA, 0.5),
        'rwkv_a0': small((N_A, D_MODEL), 0.1),
        'rwkv_a1': dense((N_A, D_MODEL, RWKV_AAA_LORA), D_MODEL),
        'rwkv_a2': dense((N_A, RWKV_AAA_LORA, D_MODEL), RWKV_AAA_LORA, 0.5),
        'rwkv_g1': dense((N_A, D_MODEL, RWKV_GATE_LORA), D_MODEL),
        'rwkv_g2': dense((N_A, RWKV_GATE_LORA, D_MODEL), RWKV_GATE_LORA),
        'rwkv_k_k': gain((N_A, D_MODEL), 0.85),
        'rwkv_k_a': gain((N_A, D_MODEL), 1.0),
        'rwkv_r_k': gain((N_A, RWKV_HEADS, RWKV_HEAD_DIM), -0.04),
        'rwkv_lnx_w': gain((N_A, D_MODEL)),
        'rwkv_lnx_b': small((N_A, D_MODEL), 0.02),
        'rwkv_v0': gain((nv, D_MODEL), 1.0),
        'rwkv_v1': dense((nv, D_MODEL, RWKV_MV_LORA), D_MODEL),
        'rwkv_v2': dense((nv, RWKV_MV_LORA, D_MODEL), RWKV_MV_LORA, 0.5),
        'attn_w_qkv': dense((N_B, D_MODEL, 3 * DIFF_WIDTH), D_MODEL),
        'attn_w_o': dense((N_B, DIFF_WIDTH, D_MODEL), DIFF_WIDTH),
        'attn_lambda': small((N_B, 4, DIFF_HEAD_DIM), 0.1),
        'attn_subln_w': gain((N_B, DIFF_E)),
        'rel_bias': small((REL_BUCKETS, DIFF_HEADS), 0.5),
        'gdn_w_in': dense((N_C, D_MODEL, GDN_IN), D_MODEL),
        'gdn_conv_w': small((N_C, GDN_CONV, GDN_QKV), 0.5),
        'gdn_a_log': jnp.log(jax.random.uniform(nk(), (N_C, GDN_HEADS), f32, 1.0, 16.0)),
        'gdn_dt_bias': dt + jnp.log(-jnp.expm1(-dt)),
        'gdn_norm_w': gain((N_C, GDN_DV)),
        'gdn_w_o': dense((N_C, GDN_HEADS * GDN_DV, D_MODEL), GDN_HEADS * GDN_DV),
    }


def reference(x_prompt, x_sample, state_rwkv_wkv, state_rwkv_shift, cache_attn_k, cache_attn_v,
              state_gdn, state_gdn_conv, page_table, norm_w, final_norm_w, ffn_w_gate, ffn_w_up,
              ffn_w_down, rwkv_mu, rwkv_w_rkv, rwkv_w_o, rwkv_w0, rwkv_w1, rwkv_w2, rwkv_a0, rwkv_a1,
              rwkv_a2, rwkv_g1, rwkv_g2, rwkv_k_k, rwkv_k_a, rwkv_r_k, rwkv_lnx_w, rwkv_lnx_b, rwkv_v0,
              rwkv_v1, rwkv_v2, attn_w_qkv, attn_w_o, attn_lambda, attn_subln_w, rel_bias, gdn_w_in,
              gdn_conv_w, gdn_a_log, gdn_dt_bias, gdn_norm_w, gdn_w_o):
    B = x_prompt.shape[0]
    DB = x_sample.shape[0]
    dt = x_prompt.dtype
    xp, xs = x_prompt, x_sample
    vf_p = vf_s = None
    wkv_p, shift_p, ak_p, av_p, gs_p, gc_p = [], [], [], [], [], []
    wkv_s, shift_s, ak_s, av_s, gs_s, gc_s = [], [], [], [], [], []
    for i in range(DEPTH):
        kind, j = i % N_MIXERS, i // N_MIXERS
        fa = (norm_w[i, 0], ffn_w_gate[i, 0], ffn_w_up[i, 0], ffn_w_down[i, 0])
        fb = (norm_w[i, 2], ffn_w_gate[i, 1], ffn_w_up[i, 1], ffn_w_down[i, 1])
        xp = macaron_half(xp, *fa)
        xs = macaron_half(xs, *fa)
        up = rmsnorm(xp, norm_w[i, 1])
        us = rmsnorm(xs, norm_w[i, 1])
        if kind == 0:
            p = dict(mu=rwkv_mu[j], w_rkv=rwkv_w_rkv[j], w_o=rwkv_w_o[j], w0=rwkv_w0[j], w1=rwkv_w1[j],
                     w2=rwkv_w2[j], a0=rwkv_a0[j], a1=rwkv_a1[j], a2=rwkv_a2[j], g1=rwkv_g1[j], g2=rwkv_g2[j],
                     k_k=rwkv_k_k[j], k_a=rwkv_k_a[j], r_k=rwkv_r_k[j], lnx_w=rwkv_lnx_w[j], lnx_b=rwkv_lnx_b[j])
            vres = None if j == 0 else (rwkv_v0[j - 1], rwkv_v1[j - 1], rwkv_v2[j - 1])
            zero_shift = jnp.zeros((B, D_MODEL), up.dtype)
            zero_state = jnp.zeros((B, RWKV_HEADS, RWKV_HEAD_DIM, RWKV_HEAD_DIM), jnp.float32)
            mp, vf_p, s_p, sh_p = rwkv7_mix(up, zero_shift, zero_state, p, vf_p, vres)
            ms, vf_s, s_s, sh_s = rwkv7_mix(us, state_rwkv_shift[j], state_rwkv_wkv[j], p, vf_s, vres)
            wkv_p.append(s_p.astype(dt)); shift_p.append(sh_p.astype(dt))
            wkv_s.append(s_s.astype(dt)); shift_s.append(sh_s.astype(dt))
        elif kind == 1:
            p = dict(w_qkv=attn_w_qkv[j], w_o=attn_w_o[j], lam=attn_lambda[j], subln_w=attn_subln_w[j])
            lam_init = 0.8 - 0.6 * math.exp(-0.3 * i)
            mp, k_p, v_p = diff_attention_mix(up, None, None, p, lam_init, rel_bias)
            past_k = cache_attn_k[j][page_table].reshape(DB, -1, DIFF_HEADS, DIFF_E)
            past_v = cache_attn_v[j][page_table].reshape(DB, -1, DIFF_HEADS, DIFF_E)
            ms, k_s, v_s = diff_attention_mix(us, past_k, past_v, p, lam_init, rel_bias)
            ak_p.append(k_p); av_p.append(v_p)
            ak_s.append(k_s); av_s.append(v_s)
        else:
            p = dict(w_in=gdn_w_in[j], conv_w=gdn_conv_w[j], a_log=gdn_a_log[j], dt_bias=gdn_dt_bias[j],
                     norm_w=gdn_norm_w[j], w_o=gdn_w_o[j])
            zero_conv = jnp.zeros((B, GDN_CONV - 1, GDN_QKV), up.dtype)
            zero_state = jnp.zeros((B, GDN_HEADS, GDN_DK, GDN_DV), jnp.float32)
            mp, s_p, c_p = gdn_mix(up, zero_conv, zero_state, p)
            ms, s_s, c_s = gdn_mix(us, state_gdn_conv[j], state_gdn[j], p)
            gs_p.append(s_p.astype(dt)); gc_p.append(c_p.astype(dt))
            gs_s.append(s_s.astype(dt)); gc_s.append(c_s.astype(dt))
        xp = xp + mp
        xs = xs + ms
        xp = macaron_half(xp, *fb)
        xs = macaron_half(xs, *fb)
    y_prompt = rmsnorm(xp, final_norm_w)
    y_sample = rmsnorm(xs, final_norm_w)
    return (y_prompt, y_sample,
            jnp.stack(wkv_p), jnp.stack(shift_p), jnp.stack(ak_p), jnp.stack(av_p), jnp.stack(gs_p), jnp.stack(gc_p),
            jnp.stack(wkv_s), jnp.stack(shift_s), jnp.stack(ak_s), jnp.stack(av_s), jnp.stack(gs_s), jnp.stack(gc_s))
```

```python
import functools
import math

import jax
import jax.numpy as jnp
from jax import lax
from jax.experimental import pallas as pl
from jax.experimental.pallas import tpu as pltpu

F32 = jnp.float32
BF = jnp.bfloat16

D_MODEL = 1024
NORM_EPS = 1e-6
RWKV_N = 64
RWKV_H = D_MODEL // RWKV_N
RWKV_LNX_EPS = 64e-5
DIFF_H = 8
DIFF_DH = 64
DIFF_E = 128
SUBLN_EPS = 1e-5
REL_BUCKETS = 32
REL_MAX_EXACT = 16
REL_MAX_DIST = 128
GDN_H = 8
GDN_DK = 128
GDN_DV = 128
GDN_CONV = 4
GDN_QKV = GDN_H * (2 * GDN_DK + GDN_DV)
CHUNK = 64
LANES = 128
NEG = -1e30
VMEM_LIMIT = 56 * 1024 * 1024


def _dot(a, b):
    return jnp.dot(a.astype(BF), b.astype(BF), preferred_element_type=F32)


def _dot_nt(a, b):
    return lax.dot_general(a.astype(BF), b.astype(BF), (((1,), (1,)), ((), ())), preferred_element_type=F32)


def _dot_tn(a, b):
    return lax.dot_general(a.astype(BF), b.astype(BF), (((0,), (0,)), ((), ())), preferred_element_type=F32)


def _split(x, n):
    out = []
    for _ in range(n):
        h = x.astype(BF)
        out.append(h)
        x = x - h.astype(F32)
    return out


def _sel_dot(sel, x, n=3):
    return sum(jnp.dot(sel, t, preferred_element_type=F32) for t in _split(x, n))


def _dot_sel(x, sel, n=2):
    return sum(jnp.dot(t, sel, preferred_element_type=F32) for t in _split(x, n))


def _rms(x, g, eps):
    return x * lax.rsqrt(jnp.mean(x * x, axis=-1, keepdims=True) + eps) * g


def _sigmoid(x):
    return 1.0 / (1.0 + jnp.exp(-x))


def _softplus(x):
    return jnp.maximum(x, 0.0) + jnp.log(1.0 + jnp.exp(-jnp.abs(x)))


def _iota2(shape):
    return lax.broadcasted_iota(jnp.int32, shape, 0), lax.broadcasted_iota(jnp.int32, shape, 1)


def _tri_inv(n_mat, size):
    ii, jj = _iota2((size, size))
    eye = (ii == jj).astype(F32)
    x = eye + jnp.where((ii >> 1) == (jj >> 1), n_mat, 0.0)
    lvl = 1
    while (2 << lvl) <= size:
        m = 1 << lvl
        sel = ((ii >> (lvl + 1)) == (jj >> (lvl + 1))) & (((ii >> lvl) & 1) == 1) & (((jj >> lvl) & 1) == 0)
        c = jnp.where(sel, n_mat, 0.0)
        x = x + _dot(_dot(x, c), x)
        lvl += 1
        del m
    return x


def _rows_call(body, rows, consts, outs, tm, name):
    n = rows[0].shape[0]
    assert n % tm == 0, (n, tm)
    in_specs = [pl.BlockSpec((tm, a.shape[1]), lambda i: (i, 0)) for a in rows]
    for a in consts:
        in_specs.append(pl.BlockSpec(a.shape, lambda i, nd=a.ndim: (0,) * nd, pipeline_mode=pl.Buffered(1)))
    out_specs = [pl.BlockSpec((tm, c), lambda i: (i, 0)) for c, _ in outs]
    out_shape = [jax.ShapeDtypeStruct((n, c), dt) for c, dt in outs]
    return pl.pallas_call(
        body, grid=(n // tm,), in_specs=in_specs, out_specs=out_specs, out_shape=out_shape, name=name,
        compiler_params=pltpu.CompilerParams(dimension_semantics=("parallel",), vmem_limit_bytes=VMEM_LIMIT),
    )(*rows, *consts)


def _row_tile(n, want):
    tm = min(want, n)
    while n % tm:
        tm //= 2
    return tm


def _ffn_body(x_ref, g_ref, wg_ref, wu_ref, wd_ref, g2_ref, o_ref, u_ref, *, nchunk):
    x = x_ref[...]
    h = _rms(x, g_ref[...], NORM_EPS).astype(BF)
    fc = wg_ref.shape[1] // nchunk
    acc = jnp.zeros_like(x)
    for c in range(nchunk):
        sl = slice(c * fc, (c + 1) * fc)
        gate = jnp.dot(h, wg_ref[:, sl], preferred_element_type=F32)
        up = jnp.dot(h, wu_ref[:, sl], preferred_element_type=F32)
        act = (gate * _sigmoid(gate) * up).astype(BF)
        acc = acc + jnp.dot(act, wd_ref[sl, :], preferred_element_type=F32)
    y = x + 0.5 * acc
    o_ref[...] = y
    u_ref[...] = _rms(y, g2_ref[...], NORM_EPS)


def _ffn(x, g, wg, wu, wd, g_next):
    nchunk = 2 if wg.shape[1] % 256 == 0 else 1
    body = functools.partial(_ffn_body, nchunk=nchunk)
    tm = _row_tile(x.shape[0], 512)
    return _rows_call(body, [x], [g[None], wg.astype(BF), wu.astype(BF), wd.astype(BF), g_next[None]],
                      [(D_MODEL, F32), (D_MODEL, F32)], tm, "ffn_half")


def _linear_body(*refs, n_out, residual):
    if residual:
        x_ref, u_ref, w_ref = refs[:3]
        outs = refs[3:]
    else:
        u_ref, w_ref = refs[:2]
        outs = refs[2:]
    y = jnp.dot(u_ref[...].astype(BF), w_ref[...], preferred_element_type=F32)
    if residual:
        outs[0][...] = x_ref[...] + y
    else:
        c = y.shape[1] // n_out
        for k, o in enumerate(outs):
            o[...] = y[:, k * c:(k + 1) * c]


def _linear(u, w, n_out, name):
    body = functools.partial(_linear_body, n_out=n_out, residual=False)
    tm = _row_tile(u.shape[0], 512)
    c = w.shape[1] // n_out
    return _rows_call(body, [u], [w.astype(BF)], [(c, F32)] * n_out, tm, name)


def _linear_res(x, u, w, name):
    body = functools.partial(_linear_body, n_out=1, residual=True)
    tm = _row_tile(u.shape[0], 512)
    return _rows_call(body, [x, u], [w.astype(BF)], [(w.shape[1], F32)], tm, name)[0]


def _head_sel(width, heads):
    e = (jnp.arange(width)[:, None] // (width // heads) == jnp.arange(heads)[None, :]).astype(BF)
    return e, e.T


def _rwkv_proj_body(*refs, has_vres):
    (u_ref, prev_ref) = refs[:2]
    k0 = 2
    if has_vres:
        vf_ref = refs[2]
        k0 = 3
    (mu_ref, wr_ref, wk_ref, wv_ref, w0_ref, w1_ref, w2_ref, a0_ref, a1_ref, a2_ref, g1_ref, g2_ref,
     kk_ref, ka_ref, rk_ref, e_ref, et_ref) = refs[k0:k0 + 17]
    k1 = k0 + 17
    if has_vres:
        v0_ref, v1_ref, v2_ref = refs[k1:k1 + 3]
        k1 += 3
    r_o, lw_o, k_o, v_o, a_o, b_o, g_o, bon_o = refs[k1:]
    u = u_ref[...]
    xx = prev_ref[...] - u
    mu = mu_ref[...]
    xr, xw, xk = u + xx * mu[0:1], u + xx * mu[1:2], u + xx * mu[2:3]
    xv, xa, xg = u + xx * mu[3:4], u + xx * mu[4:5], u + xx * mu[5:6]
    r = _dot(xr, wr_ref[...])
    k = _dot(xk, wk_ref[...])
    v = _dot(xv, wv_ref[...])
    wlog = -_softplus(-(w0_ref[...] + _dot(jnp.tanh(_dot(xw, w1_ref[...])), w2_ref[...]))) - 0.5
    lw_o[...] = -jnp.exp(wlog)
    if has_vres:
        v = v + (vf_ref[...] - v) * _sigmoid(v0_ref[...] + _dot(_dot(xv, v1_ref[...]), v2_ref[...]))
    a = _sigmoid(a0_ref[...] + _dot(_dot(xa, a1_ref[...]), a2_ref[...]))
    g_o[...] = _dot(_sigmoid(_dot(xg, g1_ref[...])), g2_ref[...])
    e, et = e_ref[...], et_ref[...]
    kk = k * kk_ref[...]
    kk = kk * lax.rsqrt(_dot_sel(_dot_sel(kk * kk, e), et) + 1e-24)
    k_mod = k * (1.0 + (a - 1.0) * ka_ref[...])
    r_o[...] = r
    k_o[...] = k_mod
    v_o[...] = v
    a_o[...] = -kk
    b_o[...] = kk * a
    bon_o[...] = _dot_sel(_dot_sel(r * k_mod * rk_ref[...], e), et) * v


def _rwkv_chunk_body(r_ref, lw_ref, k_ref, v_ref, a_ref, b_ref, o_ref, s_ref, *, seq):
    L, P, N = CHUNK, LANES, RWKV_N
    ii, jj = _iota2((L, L))
    strict = (ii > jj).astype(F32)
    incl = (ii >= jj).astype(F32)
    ci, cj = _iota2((2 * L, L))
    cum_sel = ((cj <= ci) | (ci >= L)).astype(F32).astype(BF)
    lane = lax.broadcasted_iota(jnp.int32, (L, P), 1)
    lane2 = lax.broadcasted_iota(jnp.int32, (2 * L, P), 1)
    bi, bj = _iota2((P, P))
    bd = ((bi >= N) == (bj >= N)).astype(F32)

    def chunk(c, s):
        rows = pl.ds(pl.multiple_of(c * L, L), L)
        lw = lw_ref[rows, :]
        cs = _sel_dot(cum_sel, lw)
        gc, gl = cs[:L], cs[L:]
        r, k, v, a, b = r_ref[rows, :], k_ref[rows, :], v_ref[rows, :], a_ref[rows, :], b_ref[rows, :]
        e_neg = jnp.exp(-gc)
        e_end = jnp.exp(gl - gc)
        x = jnp.concatenate([a * jnp.exp(gc - lw), r * jnp.exp(gc)], axis=0).astype(BF)
        bt = (b * e_neg).astype(BF)
        kt = (k * e_neg).astype(BF)
        xs = _dot_nt(x, s)
        rhs = xs[:L]
        out = xs[L:]
        parts = []
        for h in range(2):
            mh = (lane >= N) == bool(h)
            xh = jnp.where((lane2 >= N) == bool(h), x, jnp.zeros_like(x))
            ab = _dot_nt(xh, bt)
            ak = _dot_nt(xh, kt)
            t_inv = _tri_inv(ab[:L] * strict, L)
            vh = jnp.where(mh, v, 0.0)
            rhs = rhs + _dot(ak[:L] * strict, vh)
            out = out + _dot(ak[L:] * incl, vh)
            parts.append((mh, t_inv, ab[L:] * incl))
        uu = jnp.zeros_like(rhs)
        for mh, t_inv, _ in parts:
            uu = uu + _dot(t_inv, jnp.where(mh, rhs, 0.0))
        for mh, _, arb in parts:
            out = out + _dot(arb, jnp.where(mh, uu, 0.0))
        o_ref[rows, :] = out
        upd = _dot_tn(uu, b * e_end) + _dot_tn(v, k * e_end)
        return s * jnp.exp(gl[0:1]) + bd * upd

    s = lax.fori_loop(0, seq // L, chunk, jnp.zeros((P, P), F32))
    s_ref[0, 0] = s[:N, :N]
    s_ref[0, 1] = s[N:, N:]


def _rwkv_chunk(seqs, nb, seq):
    hp = D_MODEL // LANES
    spec = pl.BlockSpec((seq, LANES), lambda b, h: (b, h))
    return pl.pallas_call(
        functools.partial(_rwkv_chunk_body, seq=seq), grid=(nb, hp),
        in_specs=[spec] * 6,
        out_specs=[spec, pl.BlockSpec((1, 2, RWKV_N, RWKV_N), lambda b, h: (b, h, 0, 0))],
        out_shape=[jax.ShapeDtypeStruct((nb * seq, D_MODEL), F32),
                   jax.ShapeDtypeStruct((nb, RWKV_H, RWKV_N, RWKV_N), F32)],
        name="rwkv_chunk",
        compiler_params=pltpu.CompilerParams(dimension_semantics=("parallel", "parallel"),
                                             vmem_limit_bytes=VMEM_LIMIT),
    )(*seqs)


def _rwkv_step_body(r_ref, lw_ref, k_ref, v_ref, a_ref, b_ref, s0_ref, o_ref, s_ref, *, bb, steps):
    N = RWKV_N
    ii, jj = _iota2((N, N))
    eye = (ii == jj).astype(F32)

    def one(i, carry):
        b_i = i // RWKV_H
        h = i % RWKV_H
        s = s0_ref[b_i, h]
        for t in range(steps):
            row = lambda ref: ref[b_i, t, pl.ds(h, 1), :]
            sa = jnp.sum(s * row(a_ref), axis=1, keepdims=True)
            v_col = jnp.sum(eye * row(v_ref), axis=1, keepdims=True)
            s = s * jnp.exp(row(lw_ref)) + sa * row(b_ref) + v_col * row(k_ref)
            o_col = jnp.sum(s * row(r_ref), axis=1, keepdims=True)
            o_ref[b_i, t, pl.ds(h, 1), :] = jnp.sum(eye * o_col, axis=0, keepdims=True)
        s_ref[b_i, h] = s
        return carry

    lax.fori_loop(0, bb * RWKV_H, one, 0)


def _rwkv_step(seqs, s0, nb, steps):
    bb = _row_tile(nb, 8)
    spec = pl.BlockSpec((bb, steps, RWKV_H, RWKV_N), lambda b: (b, 0, 0, 0))
    sspec = pl.BlockSpec((bb, RWKV_H, RWKV_N, RWKV_N), lambda b: (b, 0, 0, 0))
    seqs = [t.reshape(nb, steps, RWKV_H, RWKV_N) for t in seqs]
    o, s = pl.pallas_call(
        functools.partial(_rwkv_step_body, bb=bb, steps=steps), grid=(nb // bb,),
        in_specs=[spec] * 6 + [sspec], out_specs=[spec, sspec],
        out_shape=[jax.ShapeDtypeStruct((nb, steps, RWKV_H, RWKV_N), F32),
                   jax.ShapeDtypeStruct((nb, RWKV_H, RWKV_N, RWKV_N), F32)],
        name="rwkv_step",
        compiler_params=pltpu.CompilerParams(dimension_semantics=("parallel",), vmem_limit_bytes=VMEM_LIMIT),
    )(*seqs, s0)
    return o.reshape(nb * steps, D_MODEL), s


def _rwkv_out_body(x_ref, o_ref, g_ref, bon_ref, lw_ref, lb_ref, e_ref, et_ref, wo_ref, y_ref):
    e, et = e_ref[...], et_ref[...]
    o = o_ref[...]
    inv_n = 1.0 / RWKV_N
    d = o - _dot_sel(_dot_sel(o, e), et) * inv_n
    var = _dot_sel(_dot_sel(d * d, e), et) * inv_n
    on = d * lax.rsqrt(var + RWKV_LNX_EPS) * lw_ref[...] + lb_ref[...]
    y_ref[...] = x_ref[...] + _dot((on + bon_ref[...]) * g_ref[...], wo_ref[...])


def _shifted(u, first, nb, seq):
    u3 = u.reshape(nb, seq, D_MODEL)
    return jnp.concatenate([first[:, None, :], u3[:, :-1]], axis=1).reshape(nb * seq, D_MODEL)


def _rwkv_layer(x, u, dims, shift_s, wkv_s, p, v_first, vres):
    nb_p, seq_p, nb_s, seq_s = dims
    n_p = nb_p * seq_p
    prev = jnp.concatenate([_shifted(u[:n_p], jnp.zeros((nb_p, D_MODEL), F32), nb_p, seq_p),
                            _shifted(u[n_p:], shift_s, nb_s, seq_s)], axis=0)
    e, et = _head_sel(D_MODEL, RWKV_H)
    has_vres = vres is not None
    rows = [u, prev] + ([v_first] if has_vres else [])
    consts = [p['mu'], p['w_rkv'][0].astype(BF), p['w_rkv'][1].astype(BF), p['w_rkv'][2].astype(BF),
              p['w0'][None], p['w1'].astype(BF), p['w2'].astype(BF), p['a0'][None], p['a1'].astype(BF),
              p['a2'].astype(BF), p['g1'].astype(BF), p['g2'].astype(BF), p['k_k'][None], p['k_a'][None],
              p['r_k'].reshape(1, D_MODEL), e, et]
    if has_vres:
        consts += [vres[0][None], vres[1].astype(BF), vres[2].astype(BF)]
    tm = _row_tile(u.shape[0], 256)
    r, lw, k, v, a, b, gate, bonus = _rows_call(
        functools.partial(_rwkv_proj_body, has_vres=has_vres), rows, consts, [(D_MODEL, F32)] * 8, tm, "rwkv_proj")
    seqs = (r, lw, k, v, a, b)
    o_p, s_p = _rwkv_chunk(seqs, nb_p, seq_p)
    o_s, s_s = _rwkv_step([t[n_p:] for t in seqs], wkv_s, nb_s, seq_s)
    o = jnp.concatenate([o_p, o_s], axis=0)
    tm = _row_tile(u.shape[0], 512)
    x = _rows_call(_rwkv_out_body, [x, o, gate, bonus],
                   [p['lnx_w'][None], p['lnx_b'][None], e, et, p['w_o'].astype(BF)], [(D_MODEL, F32)], tm,
                   "rwkv_out")[0]
    u3p = u[:n_p].reshape(nb_p, seq_p, D_MODEL)
    u3s = u[n_p:].reshape(nb_s, seq_s, D_MODEL)
    return x, (v if not has_vres else v_first), (s_p, u3p[:, -1], s_s, u3s[:, -1])


def _t5_bias(dist, rel_bias):
    n = jnp.maximum(dist, 0)
    nf = jnp.maximum(n, 1).astype(F32)
    large = REL_MAX_EXACT + (jnp.log(nf / REL_MAX_EXACT) / math.log(REL_MAX_DIST / REL_MAX_EXACT)
                             * (REL_BUCKETS - REL_MAX_EXACT)).astype(jnp.int32)
    large = jnp.minimum(large, REL_BUCKETS - 1)
    return rel_bias[jnp.where(n < REL_MAX_EXACT, n, large)]


def _lam_of(lam_ref, lam_init):
    lv = lam_ref[...]
    return (jnp.exp(jnp.sum(lv[0:1] * lv[1:2], axis=1, keepdims=True))
            - jnp.exp(jnp.sum(lv[2:3] * lv[3:4], axis=1, keepdims=True)) + lam_init)


def _attn_prompt_body(q_ref, k_ref, v_ref, bias_ref, far_ref, lam_ref, sw_ref, o_ref, *, tq, lam_init):
    qi = pl.program_id(2)
    lane = lax.broadcasted_iota(jnp.int32, (tq, DIFF_E), 1)
    q = q_ref[...] * (DIFF_DH ** -0.5)
    q1 = jnp.where(lane < DIFF_DH, q, 0.0).astype(BF)
    q2 = jnp.where(lane >= DIFF_DH, q, 0.0).astype(BF)
    ii, jj = _iota2((tq, tq))

    def step(carry, kj, bias):
        kb = k_ref[pl.ds(pl.multiple_of(kj * tq, tq), tq), :].astype(BF)
        vb = v_ref[pl.ds(pl.multiple_of(kj * tq, tq), tq), :].astype(BF)
        new = []
        for qm, (m, l, acc) in zip((q1, q2), carry):
            s = _dot_nt(qm, kb) + bias
            m_new = jnp.maximum(m, jnp.max(s, axis=1, keepdims=True))
            alpha = jnp.exp(m - m_new)
            pr = jnp.exp(s - m_new)
            new.append((m_new, l * alpha + jnp.sum(pr, axis=1, keepdims=True),
                        acc * alpha + jnp.dot(pr.astype(BF), vb, preferred_element_type=F32)))
        return tuple(new)

    init = tuple((jnp.full((tq, 1), NEG, F32), jnp.zeros((tq, 1), F32), jnp.zeros((tq, DIFF_E), F32))
                 for _ in range(2))
    carry = step(init, qi, jnp.where(ii >= jj, bias_ref[0, 0], NEG))
    prev_ok = qi >= 1
    carry = step(carry, jnp.maximum(qi - 1, 0), jnp.where(prev_ok, bias_ref[0, 1], NEG))
    far = far_ref[0, 0:1, :]
    carry = lax.fori_loop(0, jnp.maximum(qi - 1, 0), lambda kj, c: step(c, kj, far), carry)
    (_, l1, acc1), (_, l2, acc2) = carry
    o = acc1 / l1 - _lam_of(lam_ref, lam_init) * (acc2 / l2)
    o_ref[...] = _rms(o, sw_ref[...], SUBLN_EPS) * (1.0 - lam_init)


def _attn_prompt(q, k, v, nb, seq, rel_bias, lam, subln_w, lam_init):
    tq = _row_tile(seq, 256)
    assert tq >= REL_MAX_DIST or tq == seq
    nq = seq // tq
    d0 = jnp.arange(tq)[:, None] - jnp.arange(tq)[None, :]
    bias = jnp.stack([_t5_bias(d0, rel_bias), _t5_bias(d0 + tq, rel_bias)])
    bias = jnp.transpose(bias, (3, 0, 1, 2))
    far = jnp.broadcast_to(rel_bias[REL_BUCKETS - 1][:, None, None], (DIFF_H, 8, tq)).astype(F32)
    qspec = pl.BlockSpec((tq, DIFF_E), lambda h, b, i: (b * nq + i, h))
    kspec = pl.BlockSpec((seq, DIFF_E), lambda h, b, i: (b, h))
    return pl.pallas_call(
        functools.partial(_attn_prompt_body, tq=tq, lam_init=lam_init), grid=(DIFF_H, nb, nq),
        in_specs=[qspec, kspec, kspec,
                  pl.BlockSpec((1, 2, tq, tq), lambda h, b, i: (h, 0, 0, 0)),
                  pl.BlockSpec((1, 8, tq), lambda h, b, i: (h, 0, 0)),
                  pl.BlockSpec((4, DIFF_DH), lambda h, b, i: (0, 0)),
                  pl.BlockSpec((1, DIFF_E), lambda h, b, i: (0, 0))],
        out_specs=qspec, out_shape=jax.ShapeDtypeStruct((nb * seq, D_MODEL), F32), name="attn_prompt",
        compiler_params=pltpu.CompilerParams(dimension_semantics=("parallel", "parallel", "parallel"),
                                             vmem_limit_bytes=VMEM_LIMIT),
    )(q, k, v, bias, far, lam, subln_w[None])


def _attn_sample_body(pt_ref, qt_ref, kc_ref, vc_ref, kn_ref, vn_ref, bp_ref, bn_ref, lam_ref, sw_ref, g_ref,
                      o_ref, m_sc, l_sc, acc_sc, *, n_pages, lam_init):
    del pt_ref
    p = pl.program_id(1)

    @pl.when(p == 0)
    def _():
        m_sc[...] = jnp.full(m_sc.shape, NEG, F32)
        l_sc[...] = jnp.zeros(l_sc.shape, F32)
        acc_sc[...] = jnp.zeros(acc_sc.shape, F32)

    def accumulate(kb, vb, bias):
        s = jnp.dot(kb.astype(BF), qt_ref[0], preferred_element_type=F32) + bias
        m = m_sc[...]
        m_new = jnp.maximum(m, jnp.max(s, axis=0, keepdims=True))
        alpha = jnp.exp(m - m_new)
        pr = jnp.exp(s - m_new)
        l_sc[...] = l_sc[...] * alpha + jnp.sum(pr, axis=0, keepdims=True)
        acc_sc[...] = acc_sc[...] * alpha + _dot_tn(vb, pr)
        m_sc[...] = m_new

    @pl.when(p < n_pages)
    def _():
        accumulate(kc_ref[0], vc_ref[0], bp_ref[...])

    @pl.when(p == n_pages)
    def _():
        accumulate(kn_ref[0], vn_ref[0], bn_ref[...])
        z = acc_sc[...] / l_sc[...]
        ri, ci = _iota2(z.shape)
        z = jnp.where((ri // DIFF_E) == (ci // (z.shape[1] // DIFF_H)), z, 0.0)
        o = _dot_sel(z, g_ref[0], 3) - _lam_of(lam_ref, lam_init) * _dot_sel(z, g_ref[1], 3)
        nq = o.shape[1]
        o3 = o.reshape(DIFF_H, DIFF_E, nq)
        ms = jnp.mean(o3 * o3, axis=1, keepdims=True)
        o3 = o3 * lax.rsqrt(ms + SUBLN_EPS)
        o_ref[0] = o3.reshape(DIFF_H * DIFF_E, nq) * sw_ref[...] * (1.0 - lam_init)


def _attn_sample(q, k_new, v_new, cache_k, cache_v, page_table, rel_bias, lam, subln_w, lam_init):
    nb, n_pages = page_table.shape
    page = cache_k.shape[1]
    steps = q.shape[0] // nb
    past = n_pages * page
    width = DIFF_H * DIFF_E
    ncol = DIFF_H * steps * 2
    q5 = (q * (DIFF_DH ** -0.5)).reshape(nb, steps, DIFF_H, 2, DIFF_DH)
    hsel = jnp.eye(DIFF_H, dtype=F32)
    csel = jnp.eye(2, dtype=F32)
    qt = jnp.einsum('bihcd,hg,ce->bhcdgie', q5, hsel, csel).reshape(nb, width, ncol).astype(BF)
    q_pos = past + jnp.arange(steps)
    bias_all = _t5_bias(q_pos[None, :] - jnp.arange(past + steps)[:, None], rel_bias)
    vis = (jnp.arange(past + steps)[:, None] <= q_pos[None, :])[..., None]
    bias_all = jnp.where(vis, bias_all, NEG)
    bias_all = jnp.broadcast_to(jnp.transpose(bias_all, (0, 2, 1))[..., None], (past + steps, DIFF_H, steps, 2))
    bias_all = bias_all.reshape(past + steps, ncol)
    pad = 8 - steps
    bias_new = jnp.concatenate([bias_all[past:], jnp.full((pad, ncol), NEG, F32)], axis=0)
    kn = jnp.pad(k_new.reshape(nb, steps, width), ((0, 0), (0, pad), (0, 0)))
    vn = jnp.pad(v_new.reshape(nb, steps, width), ((0, 0), (0, pad), (0, 0)))
    col = jnp.arange(ncol)
    gsel = jnp.stack([((col[:, None] // 2) % steps == jnp.arange(steps)[None, :]) & (col[:, None] % 2 == c)
                      for c in range(2)]).astype(BF)
    sw_col = jnp.tile(subln_w, DIFF_H)[:, None]
    last = n_pages - 1
    grid_spec = pltpu.PrefetchScalarGridSpec(
        num_scalar_prefetch=1, grid=(nb, n_pages + 1),
        in_specs=[pl.BlockSpec((1, width, ncol), lambda b, p, pt: (b, 0, 0)),
                  pl.BlockSpec((1, page, width), lambda b, p, pt: (pt[b, jnp.minimum(p, last)], 0, 0)),
                  pl.BlockSpec((1, page, width), lambda b, p, pt: (pt[b, jnp.minimum(p, last)], 0, 0)),
                  pl.BlockSpec((1, 8, width), lambda b, p, pt: (b, 0, 0)),
                  pl.BlockSpec((1, 8, width), lambda b, p, pt: (b, 0, 0)),
                  pl.BlockSpec((page, ncol), lambda b, p, pt: (jnp.minimum(p, last), 0)),
                  pl.BlockSpec((8, ncol), lambda b, p, pt: (0, 0)),
                  pl.BlockSpec((4, DIFF_DH), lambda b, p, pt: (0, 0)),
                  pl.BlockSpec((width, 1), lambda b, p, pt: (0, 0)),
                  pl.BlockSpec((2, ncol, steps), lambda b, p, pt: (0, 0, 0))],
        out_specs=pl.BlockSpec((1, width, steps), lambda b, p, pt: (b, 0, 0)),
        scratch_shapes=[pltpu.VMEM((1, ncol), F32), pltpu.VMEM((1, ncol), F32), pltpu.VMEM((width, ncol), F32)])
    o_t = pl.pallas_call(
        functools.partial(_attn_sample_body, n_pages=n_pages, lam_init=lam_init), grid_spec=grid_spec,
        out_shape=jax.ShapeDtypeStruct((nb, width, steps), F32), name="attn_sample",
        compiler_params=pltpu.CompilerParams(dimension_semantics=("parallel", "arbitrary"),
                                             vmem_limit_bytes=VMEM_LIMIT),
    )(page_table, qt, cache_k.reshape(-1, page, width), cache_v.reshape(-1, page, width), kn, vn,
      bias_all[:past], bias_new, lam, sw_col, gsel)
    return jnp.transpose(o_t, (0, 2, 1)).reshape(nb * steps, width)


def _attn_layer(x, u, dims, cache_k, cache_v, page_table, p, lam_init, rel_bias):
    nb_p, seq_p, nb_s, seq_s = dims
    n_p = nb_p * seq_p
    q, k, v = _linear(u, p['w_qkv'], 3, "attn_qkv")
    o_p = _attn_prompt(q, k, v, nb_p, seq_p, rel_bias, p['lam'], p['subln_w'], lam_init)
    o_s = _attn_sample(q[n_p:], k[n_p:], v[n_p:], cache_k, cache_v, page_table, rel_bias, p['lam'],
                       p['subln_w'], lam_init)
    x = _linear_res(x, jnp.concatenate([o_p, o_s], axis=0), p['w_o'], "attn_out")
    shp_p = (nb_p, seq_p, DIFF_H, DIFF_E)
    shp_s = (nb_s, seq_s, DIFF_H, DIFF_E)
    return x, (k[:n_p].reshape(shp_p), v[:n_p].reshape(shp_p), k[n_p:].reshape(shp_s), v[n_p:].reshape(shp_s))


def _gdn_proj_body(u_ref, wqkv_ref, wz_ref, wb_ref, wa_ref, al_ref, dt_ref, eb_ref, qkv_o, z_o, beta_o, g_o):
    u = u_ref[...].astype(BF)
    qkv_o[...] = jnp.dot(u, wqkv_ref[...], preferred_element_type=F32)
    z_o[...] = jnp.dot(u, wz_ref[...], preferred_element_type=F32)
    beta = _sigmoid(jnp.dot(u, wb_ref[...], preferred_element_type=F32))
    g = -jnp.exp(al_ref[...]) * _softplus(jnp.dot(u, wa_ref[...], preferred_element_type=F32) + dt_ref[...])
    eb = eb_ref[...]
    beta_o[...] = _dot_sel(beta, eb, 3)
    g_o[...] = _dot_sel(g, eb, 3)


def _gdn_post(conv, q_o, k_o, v_o):
    c = conv * _sigmoid(conv)
    hk = GDN_H * GDN_DK
    for h in range(GDN_H):
        qs = c[:, h * GDN_DK:(h + 1) * GDN_DK]
        ks = c[:, hk + h * GDN_DK:hk + (h + 1) * GDN_DK]
        q_o[:, h * GDN_DK:(h + 1) * GDN_DK] = qs * lax.rsqrt(jnp.sum(qs * qs, axis=1, keepdims=True) + 1e-6) * (GDN_DK ** -0.5)
        k_o[:, h * GDN_DK:(h + 1) * GDN_DK] = ks * lax.rsqrt(jnp.sum(ks * ks, axis=1, keepdims=True) + 1e-6)
    v_o[...] = c[:, 2 * hk:]


def _gdn_conv_prompt_body(x_ref, halo_ref, w_ref, q_o, k_o, v_o, *, tiles_per_seq):
    i = pl.program_id(0)
    x = x_ref[...]
    tm = x.shape[0]
    halo = jnp.where(i % tiles_per_seq == 0, 0.0, halo_ref[...])
    row8 = lax.broadcasted_iota(jnp.int32, (8, x.shape[1]), 0)
    w = w_ref[...]
    conv = w[GDN_CONV - 1:GDN_CONV] * x
    for j in range(GDN_CONV - 1):
        sh = GDN_CONV - 1 - j
        rolled = pltpu.roll(x, sh, 0)
        head = jnp.where(row8 < sh, pltpu.roll(halo, sh, 0), rolled[:8])
        tap = jnp.concatenate([head, rolled[8:]], axis=0) if tm > 8 else head
        conv = conv + w[j:j + 1] * tap
    _gdn_post(conv, q_o, k_o, v_o)


def _gdn_conv_taps_body(t0_ref, t1_ref, t2_ref, t3_ref, w_ref, q_o, k_o, v_o):
    w = w_ref[...]
    conv = w[0:1] * t0_ref[...] + w[1:2] * t1_ref[...] + w[2:3] * t2_ref[...] + w[3:4] * t3_ref[...]
    _gdn_post(conv, q_o, k_o, v_o)


def _gdn_chunk_body(q_ref, k_ref, v_ref, beta_ref, g_ref, o_ref, s_ref, *, seq):
    L = CHUNK
    ii, jj = _iota2((L, L))
    strict = ii > jj
    incl = ii >= jj
    ci, cj = _iota2((2 * L, L))
    cum_sel = ((cj <= ci) | (ci >= L)).astype(F32).astype(BF)

    def chunk(c, s):
        rows = pl.ds(pl.multiple_of(c * L, L), L)
        q, k, v, beta = q_ref[rows, :], k_ref[rows, :], v_ref[rows, :], beta_ref[rows, :]
        cs = _sel_dot(cum_sel, g_ref[rows, :])
        gc, gl = cs[:L], cs[L:]
        diff = gc[:, :L] - gc.T[:L, :]
        dec_s = jnp.exp(jnp.where(strict, diff, NEG))
        dec_i = jnp.exp(jnp.where(incl, diff, NEG))
        t_inv = _tri_inv(-(beta[:, :L] * _dot_nt(k, k) * dec_s), L)
        e_gc = jnp.exp(gc)
        wv = _dot(t_inv, beta * v)
        wk = _dot(t_inv, beta * e_gc * k)
        uu = wv - _dot(wk, s)
        o_ref[rows, :] = _dot(q * e_gc, s) + _dot(_dot_nt(q, k) * dec_i, uu)
        return jnp.exp(gl[0:1]) * s + _dot_tn(k * jnp.exp(gl - gc), uu)

    s_ref[0, 0] = lax.fori_loop(0, seq // L, chunk, jnp.zeros((GDN_DK, GDN_DV), F32))


def _gdn_step_body(q_ref, k_ref, v_ref, beta_ref, g_ref, s0_ref, o_ref, s_ref, *, bb, steps):
    ii, jj = _iota2((GDN_DK, GDN_DK))
    eye = (ii == jj).astype(F32)

    def one(b_i, carry):
        s = s0_ref[b_i, 0]
        for t in range(steps):
            row = lambda ref: ref[b_i, pl.ds(t, 1), :]
            k_col = jnp.sum(eye * row(k_ref), axis=1, keepdims=True)
            q_col = jnp.sum(eye * row(q_ref), axis=1, keepdims=True)
            beta = row(beta_ref)
            ks = jnp.sum(k_col * s, axis=0, keepdims=True)
            s = jnp.exp(row(g_ref)) * (s - (beta * k_col) * ks) + (beta * k_col) * row(v_ref)
            o_ref[b_i, pl.ds(t, 1), :] = jnp.sum(q_col * s, axis=0, keepdims=True)
        s_ref[b_i, 0] = s
        return carry

    lax.fori_loop(0, bb, one, 0)


def _gdn_out_body(x_ref, o_ref, z_ref, nw_ref, wo_ref, y_ref):
    z = z_ref[...]
    nw = nw_ref[...]
    o = o_ref[...]
    parts = []
    for h in range(GDN_H):
        sl = slice(h * GDN_DV, (h + 1) * GDN_DV)
        parts.append(_rms(o[:, sl], nw, NORM_EPS))
    on = jnp.concatenate(parts, axis=1) * (z * _sigmoid(z))
    y_ref[...] = x_ref[...] + _dot(on, wo_ref[...])


def _gdn_layer(x, u, dims, conv_s, state_s, p):
    nb_p, seq_p, nb_s, seq_s = dims
    n_p = nb_p * seq_p
    n = u.shape[0]
    hv = GDN_H * GDN_DV
    w_in = p['w_in']
    pad = lambda w: jnp.pad(w, ((0, 0), (0, LANES - w.shape[-1])))
    eb = (jnp.arange(LANES)[:, None] == jnp.arange(hv)[None, :] // GDN_DV).astype(BF)
    qkv, z, beta, g = _rows_call(
        _gdn_proj_body, [u],
        [w_in[:, :GDN_QKV].astype(BF), w_in[:, GDN_QKV:GDN_QKV + hv].astype(BF),
         pad(w_in[:, GDN_QKV + hv:GDN_QKV + hv + GDN_H]).astype(BF), pad(w_in[:, GDN_QKV + hv + GDN_H:]).astype(BF),
         pad(p['a_log'][None]), pad(p['dt_bias'][None]), eb],
        [(GDN_QKV, F32), (hv, F32), (hv, F32), (hv, F32)], _row_tile(n, 256), "gdn_proj")
    tm = _row_tile(seq_p, 256)
    qkv_p = qkv[:n_p]
    cw = p['conv_w']
    outs3 = [jax.ShapeDtypeStruct((n_p, hv), F32)] * 3
    q_p, k_p, v_p = pl.pallas_call(
        functools.partial(_gdn_conv_prompt_body, tiles_per_seq=seq_p // tm), grid=(n_p // tm,),
        in_specs=[pl.BlockSpec((tm, GDN_QKV), lambda i: (i, 0)),
                  pl.BlockSpec((8, GDN_QKV), lambda i: (jnp.maximum(i * (tm // 8) - 1, 0), 0)),
                  pl.BlockSpec((GDN_CONV, GDN_QKV), lambda i: (0, 0))],
        out_specs=[pl.BlockSpec((tm, hv), lambda i: (i, 0))] * 3, out_shape=outs3, name="gdn_conv_prompt",
        compiler_params=pltpu.CompilerParams(dimension_semantics=("parallel",), vmem_limit_bytes=VMEM_LIMIT),
    )(qkv_p, qkv_p, cw)
    qkv_s = qkv[n_p:].reshape(nb_s, seq_s, GDN_QKV)
    xp_s = jnp.concatenate([conv_s, qkv_s], axis=1)
    taps = [xp_s[:, j:j + seq_s].reshape(nb_s * seq_s, GDN_QKV) for j in range(GDN_CONV)]
    q_s, k_s, v_s = _rows_call(_gdn_conv_taps_body, taps, [cw], [(hv, F32)] * 3, _row_tile(nb_s * seq_s, 256),
                               "gdn_conv_sample")
    spec = pl.BlockSpec((seq_p, LANES), lambda b, h: (b, h))
    o_p, s_p = pl.pallas_call(
        functools.partial(_gdn_chunk_body, seq=seq_p), grid=(nb_p, GDN_H), in_specs=[spec] * 5,
        out_specs=[spec, pl.BlockSpec((1, 1, GDN_DK, GDN_DV), lambda b, h: (b, h, 0, 0))],
        out_shape=[jax.ShapeDtypeStruct((n_p, hv), F32), jax.ShapeDtypeStruct((nb_p, GDN_H, GDN_DK, GDN_DV), F32)],
        name="gdn_chunk",
        compiler_params=pltpu.CompilerParams(dimension_semantics=("parallel", "parallel"),
                                             vmem_limit_bytes=VMEM_LIMIT),
    )(q_p, k_p, v_p, beta, g)
    bb = _row_tile(nb_s, 8)
    sspec = pl.BlockSpec((bb, seq_s, LANES), lambda b, h: (b, 0, h))
    stspec = pl.BlockSpec((bb, 1, GDN_DK, GDN_DV), lambda b, h: (b, h, 0, 0))
    r3 = lambda t: t.reshape(nb_s, seq_s, hv)
    o_s, s_s = pl.pallas_call(
        functools.partial(_gdn_step_body, bb=bb, steps=seq_s), grid=(nb_s // bb, GDN_H),
        in_specs=[sspec] * 5 + [stspec], out_specs=[sspec, stspec],
        out_shape=[jax.ShapeDtypeStruct((nb_s, seq_s, hv), F32),
                   jax.ShapeDtypeStruct((nb_s, GDN_H, GDN_DK, GDN_DV), F32)],
        name="gdn_step",
        compiler_params=pltpu.CompilerParams(dimension_semantics=("parallel", "parallel"),
                                             vmem_limit_bytes=VMEM_LIMIT),
    )(r3(q_s), r3(k_s), r3(v_s), r3(beta[n_p:]), r3(g[n_p:]), state_s)
    o = jnp.concatenate([o_p, o_s.reshape(nb_s * seq_s, hv)], axis=0)
    x = _rows_call(_gdn_out_body, [x, o, z], [p['norm_w'][None], p['w_o'].astype(BF)], [(D_MODEL, F32)],
                   _row_tile(n, 512), "gdn_out")[0]
    keep = GDN_CONV - 1
    conv_p = jnp.concatenate([jnp.zeros((nb_p, keep, GDN_QKV), F32), qkv_p.reshape(nb_p, seq_p, GDN_QKV)],
                             axis=1)[:, -keep:]
    return x, (s_p, conv_p, s_s, xp_s[:, -keep:])


def kernel(x_prompt, x_sample, state_rwkv_wkv, state_rwkv_shift, cache_attn_k, cache_attn_v, state_gdn, state_gdn_conv, page_table, norm_w, final_norm_w, ffn_w_gate, ffn_w_up, ffn_w_down, rwkv_mu, rwkv_w_rkv, rwkv_w_o, rwkv_w0, rwkv_w1, rwkv_w2, rwkv_a0, rwkv_a1, rwkv_a2, rwkv_g1, rwkv_g2, rwkv_k_k, rwkv_k_a, rwkv_r_k, rwkv_lnx_w, rwkv_lnx_b, rwkv_v0, rwkv_v1, rwkv_v2, attn_w_qkv, attn_w_o, attn_lambda, attn_subln_w, rel_bias, gdn_w_in, gdn_conv_w, gdn_a_log, gdn_dt_bias, gdn_norm_w, gdn_w_o):
    nb_p, seq_p, _ = x_prompt.shape
    nb_s, seq_s, _ = x_sample.shape
    dims = (nb_p, seq_p, nb_s, seq_s)
    n_p = nb_p * seq_p
    depth = norm_w.shape[0]
    x = jnp.concatenate([x_prompt.reshape(n_p, D_MODEL), x_sample.reshape(nb_s * seq_s, D_MODEL)], axis=0)
    v_first = None
    rw, at, gd = [], [], []
    for i in range(depth):
        kind, j = i % 3, i // 3
        x, u = _ffn(x, norm_w[i, 0], ffn_w_gate[i, 0], ffn_w_up[i, 0], ffn_w_down[i, 0], norm_w[i, 1])
        if kind == 0:
            p = dict(mu=rwkv_mu[j], w_rkv=rwkv_w_rkv[j], w_o=rwkv_w_o[j], w0=rwkv_w0[j], w1=rwkv_w1[j],
                     w2=rwkv_w2[j], a0=rwkv_a0[j], a1=rwkv_a1[j], a2=rwkv_a2[j], g1=rwkv_g1[j], g2=rwkv_g2[j],
                     k_k=rwkv_k_k[j], k_a=rwkv_k_a[j], r_k=rwkv_r_k[j], lnx_w=rwkv_lnx_w[j], lnx_b=rwkv_lnx_b[j])
            vres = None if j == 0 else (rwkv_v0[j - 1], rwkv_v1[j - 1], rwkv_v2[j - 1])
            x, v_first, st = _rwkv_layer(x, u, dims, state_rwkv_shift[j], state_rwkv_wkv[j], p, v_first, vres)
            rw.append(st)
        elif kind == 1:
            p = dict(w_qkv=attn_w_qkv[j], w_o=attn_w_o[j], lam=attn_lambda[j], subln_w=attn_subln_w[j])
            lam_init = 0.8 - 0.6 * math.exp(-0.3 * i)
            x, st = _attn_layer(x, u, dims, cache_attn_k[j], cache_attn_v[j], page_table, p, lam_init, rel_bias)
            at.append(st)
        else:
            p = dict(w_in=gdn_w_in[j], conv_w=gdn_conv_w[j], a_log=gdn_a_log[j], dt_bias=gdn_dt_bias[j],
                     norm_w=gdn_norm_w[j], w_o=gdn_w_o[j])
            x, st = _gdn_layer(x, u, dims, state_gdn_conv[j], state_gdn[j], p)
            gd.append(st)
        g_next = final_norm_w if i == depth - 1 else jnp.ones((D_MODEL,), F32)
        x, y = _ffn(x, norm_w[i, 2], ffn_w_gate[i, 1], ffn_w_up[i, 1], ffn_w_down[i, 1], g_next)
    stack = lambda lst, k: jnp.stack([t[k] for t in lst])
    return (y[:n_p].reshape(x_prompt.shape), y[n_p:].reshape(x_sample.shape),
            stack(rw, 0), stack(rw, 1), stack(at, 0), stack(at, 1), stack(gd, 0), stack(gd, 1),
            stack(rw, 2), stack(rw, 3), stack(at, 2), stack(at, 3), stack(gd, 2), stack(gd, 3))
```

```python
import functools
import math

import jax
import jax.numpy as jnp
from jax import lax
from jax.experimental import pallas as pl
from jax.experimental.pallas import tpu as pltpu

F32 = jnp.float32
BF = jnp.bfloat16

D_MODEL = 1024
NORM_EPS = 1e-6
RWKV_N = 64
RWKV_H = D_MODEL // RWKV_N
RWKV_LNX_EPS = 64e-5
DIFF_H = 8
DIFF_DH = 64
DIFF_E = 128
SUBLN_EPS = 1e-5
REL_BUCKETS = 32
REL_MAX_EXACT = 16
REL_MAX_DIST = 128
GDN_H = 8
GDN_DK = 128
GDN_DV = 128
GDN_CONV = 4
GDN_QKV = GDN_H * (2 * GDN_DK + GDN_DV)
CHUNK = 64
SEQ_BLOCK = 512
RWKV_GROUPS = 8
GDN_GROUPS = 8
STEP_HEADS = 4
LANES = 128
NEG = -1e30
VMEM_LIMIT = 56 * 1024 * 1024


def _dot(a, b):
    return jnp.dot(a.astype(BF), b.astype(BF), preferred_element_type=F32)


def _dot_nt(a, b):
    return lax.dot_general(a.astype(BF), b.astype(BF), (((1,), (1,)), ((), ())), preferred_element_type=F32)


def _dot_tn(a, b):
    return lax.dot_general(a.astype(BF), b.astype(BF), (((0,), (0,)), ((), ())), preferred_element_type=F32)


def _split(x, n):
    out = []
    for _ in range(n):
        h = x.astype(BF)
        out.append(h)
        x = x - h.astype(F32)
    return out


def _sel_dot(sel, x, n=3):
    return sum(jnp.dot(sel, t, preferred_element_type=F32) for t in _split(x, n))


def _dot_sel(x, sel, n=2):
    return sum(jnp.dot(t, sel, preferred_element_type=F32) for t in _split(x, n))


def _rms(x, g, eps):
    return x * lax.rsqrt(jnp.mean(x * x, axis=-1, keepdims=True) + eps) * g


def _sigmoid(x):
    return 1.0 / (1.0 + jnp.exp(-x))


def _softplus(x):
    return jnp.maximum(x, 0.0) + jnp.log(1.0 + jnp.exp(-jnp.abs(x)))


def _iota2(shape):
    return lax.broadcasted_iota(jnp.int32, shape, 0), lax.broadcasted_iota(jnp.int32, shape, 1)


def _tri_inv(n_mats, size):
    ii, jj = _iota2((size, size))
    eye = (ii == jj).astype(F32)
    xs = [eye + jnp.where((ii >> 1) == (jj >> 1), n, 0.0) for n in n_mats]
    lvl = 1
    while (2 << lvl) <= size:
        sel = ((ii >> (lvl + 1)) == (jj >> (lvl + 1))) & (((ii >> lvl) & 1) == 1) & (((jj >> lvl) & 1) == 0)
        ys = [_dot(x, jnp.where(sel, n, 0.0)) for x, n in zip(xs, n_mats)]
        xs = [x + _dot(y, x) for x, y in zip(xs, ys)]
        lvl += 1
    return xs


def _rows_call(body, rows, consts, outs, tm, name):
    n = rows[0].shape[0]
    assert n % tm == 0, (n, tm)
    in_specs = [pl.BlockSpec((tm, a.shape[1]), lambda i: (i, 0)) for a in rows]
    for a in consts:
        in_specs.append(pl.BlockSpec(a.shape, lambda i, nd=a.ndim: (0,) * nd, pipeline_mode=pl.Buffered(1)))
    out_specs = [pl.BlockSpec((tm, c), lambda i: (i, 0)) for c, _ in outs]
    out_shape = [jax.ShapeDtypeStruct((n, c), dt) for c, dt in outs]
    return pl.pallas_call(
        body, grid=(n // tm,), in_specs=in_specs, out_specs=out_specs, out_shape=out_shape, name=name,
        compiler_params=pltpu.CompilerParams(dimension_semantics=("parallel",), vmem_limit_bytes=VMEM_LIMIT),
    )(*rows, *consts)


def _row_tile(n, want):
    tm = min(want, n)
    while n % tm:
        tm //= 2
    return tm


def _ffn_body(x_ref, g_ref, wg_ref, wu_ref, wd_ref, g2_ref, o_ref, u_ref, *, nchunk):
    x = x_ref[...]
    h = _rms(x, g_ref[...], NORM_EPS).astype(BF)
    fc = wg_ref.shape[1] // nchunk
    acc = jnp.zeros_like(x)
    for c in range(nchunk):
        sl = slice(c * fc, (c + 1) * fc)
        gate = jnp.dot(h, wg_ref[:, sl], preferred_element_type=F32)
        up = jnp.dot(h, wu_ref[:, sl], preferred_element_type=F32)
        act = (gate * _sigmoid(gate) * up).astype(BF)
        acc = acc + jnp.dot(act, wd_ref[sl, :], preferred_element_type=F32)
    y = x + 0.5 * acc
    o_ref[...] = y
    u_ref[...] = _rms(y, g2_ref[...], NORM_EPS)


def _ffn(x, g, wg, wu, wd, g_next):
    nchunk = 2 if wg.shape[1] % 256 == 0 else 1
    body = functools.partial(_ffn_body, nchunk=nchunk)
    tm = _row_tile(x.shape[0], 512)
    return _rows_call(body, [x], [g[None], wg.astype(BF), wu.astype(BF), wd.astype(BF), g_next[None]],
                      [(D_MODEL, F32), (D_MODEL, F32)], tm, "ffn_half")


def _linear_body(*refs, n_out, residual):
    if residual:
        x_ref, u_ref, w_ref = refs[:3]
        outs = refs[3:]
    else:
        u_ref, w_ref = refs[:2]
        outs = refs[2:]
    y = jnp.dot(u_ref[...].astype(BF), w_ref[...], preferred_element_type=F32)
    if residual:
        outs[0][...] = x_ref[...] + y
    else:
        c = y.shape[1] // n_out
        for k, o in enumerate(outs):
            o[...] = y[:, k * c:(k + 1) * c]


def _linear(u, w, n_out, name):
    body = functools.partial(_linear_body, n_out=n_out, residual=False)
    tm = _row_tile(u.shape[0], 512)
    c = w.shape[1] // n_out
    return _rows_call(body, [u], [w.astype(BF)], [(c, F32)] * n_out, tm, name)


def _linear_res(x, u, w, name):
    body = functools.partial(_linear_body, n_out=1, residual=True)
    tm = _row_tile(u.shape[0], 512)
    return _rows_call(body, [x, u], [w.astype(BF)], [(w.shape[1], F32)], tm, name)[0]


def _head_sel(width, heads):
    e = (jnp.arange(width)[:, None] // (width // heads) == jnp.arange(heads)[None, :]).astype(BF)
    return e, e.T


def _rwkv_proj_body(*refs, has_vres):
    (u_ref, prev_ref) = refs[:2]
    k0 = 2
    if has_vres:
        vf_ref = refs[2]
        k0 = 3
    (mu_ref, wr_ref, wk_ref, wv_ref, w0_ref, w1_ref, w2_ref, a0_ref, a1_ref, a2_ref, g1_ref, g2_ref,
     kk_ref, ka_ref, rk_ref, e_ref, et_ref) = refs[k0:k0 + 17]
    k1 = k0 + 17
    if has_vres:
        v0_ref, v1_ref, v2_ref = refs[k1:k1 + 3]
        k1 += 3
    r_o, lw_o, k_o, v_o, a_o, b_o, g_o, bon_o = refs[k1:]
    u = u_ref[...]
    xx = prev_ref[...] - u
    mu = mu_ref[...]
    xr, xw, xk = u + xx * mu[0:1], u + xx * mu[1:2], u + xx * mu[2:3]
    xv, xa, xg = u + xx * mu[3:4], u + xx * mu[4:5], u + xx * mu[5:6]
    r = _dot(xr, wr_ref[...])
    k = _dot(xk, wk_ref[...])
    v = _dot(xv, wv_ref[...])
    wlog = -_softplus(-(w0_ref[...] + _dot(jnp.tanh(_dot(xw, w1_ref[...])), w2_ref[...]))) - 0.5
    lw_o[...] = -jnp.exp(wlog)
    if has_vres:
        v = v + (vf_ref[...] - v) * _sigmoid(v0_ref[...] + _dot(_dot(xv, v1_ref[...]), v2_ref[...]))
    a = _sigmoid(a0_ref[...] + _dot(_dot(xa, a1_ref[...]), a2_ref[...]))
    g_o[...] = _dot(_sigmoid(_dot(xg, g1_ref[...])), g2_ref[...])
    e, et = e_ref[...], et_ref[...]
    kk = k * kk_ref[...]
    kk = kk * lax.rsqrt(_dot_sel(_dot_sel(kk * kk, e), et) + 1e-24)
    k_mod = k * (1.0 + (a - 1.0) * ka_ref[...])
    r_o[...] = r
    k_o[...] = k_mod
    v_o[...] = v
    a_o[...] = -kk
    b_o[...] = kk * a
    bon_o[...] = _dot_sel(_dot_sel(r * k_mod * rk_ref[...], e), et) * v


def _rwkv_chunk_body(r_ref, lw_ref, k_ref, v_ref, a_ref, b_ref, o_ref, s_ref, st_sc, *, tb, groups):
    L, P, N = CHUNK, LANES, RWKV_N
    t = pl.program_id(2)

    @pl.when(t == 0)
    def _():
        st_sc[...] = jnp.zeros(st_sc.shape, F32)

    ii, jj = _iota2((L, L))
    strict = (ii > jj).astype(F32)
    incl = (ii >= jj).astype(F32)
    ci, cj = _iota2((2 * L, L))
    cum_sel = ((cj <= ci) | (ci >= L)).astype(F32).astype(BF)
    lane = lax.broadcasted_iota(jnp.int32, (L, P), 1)
    lane2 = lax.broadcasted_iota(jnp.int32, (2 * L, P), 1)
    bi, bj = _iota2((P, P))
    bd = ((bi >= N) == (bj >= N)).astype(F32)

    gs = range(groups)
    chains = [(g, h) for g in gs for h in range(2)]
    head_lanes = [(lane >= N) == bool(h) for h in range(2)]
    head_lanes2 = [(lane2 >= N) == bool(h) for h in range(2)]

    def chunk(c, carry):
        rows = pl.ds(pl.multiple_of(c * L, L), L)
        cols = [slice(g * P, (g + 1) * P) for g in gs]
        st = [st_sc[g] for g in gs]
        lw = [lw_ref[rows, cols[g]] for g in gs]
        cs = [_sel_dot(cum_sel, lw[g]) for g in gs]
        x, bt, kt, v, b_end, k_end, w_end = [], [], [], [], [], [], []
        for g in gs:
            gc, gl = cs[g][:L], cs[g][L:]
            r, k, a, b = r_ref[rows, cols[g]], k_ref[rows, cols[g]], a_ref[rows, cols[g]], b_ref[rows, cols[g]]
            e_neg = jnp.exp(-gc)
            e_end = jnp.exp(gl - gc)
            x.append(jnp.concatenate([a * jnp.exp(gc - lw[g]), r * jnp.exp(gc)], axis=0).astype(BF))
            bt.append((b * e_neg).astype(BF))
            kt.append((k * e_neg).astype(BF))
            v.append(v_ref[rows, cols[g]])
            b_end.append(b * e_end)
            k_end.append(k * e_end)
            w_end.append(jnp.exp(gl[0:1]))
        xs = [_dot_nt(x[g], st[g]) for g in gs]
        xh = {(g, h): jnp.where(head_lanes2[h], x[g], jnp.zeros_like(x[g])) for g, h in chains}
        ab = {ch: _dot_nt(xh[ch], bt[ch[0]]) for ch in chains}
        ak = {ch: _dot_nt(xh[ch], kt[ch[0]]) for ch in chains}
        t_inv = dict(zip(chains, _tri_inv([ab[ch][:L] * strict for ch in chains], L)))
        vh = {(g, h): jnp.where(head_lanes[h], v[g], 0.0) for g, h in chains}
        rhs_h = {ch: _dot(ak[ch][:L] * strict, vh[ch]) for ch in chains}
        out_h = {ch: _dot(ak[ch][L:] * incl, vh[ch]) for ch in chains}
        rhs = [xs[g][:L] + rhs_h[(g, 0)] + rhs_h[(g, 1)] for g in gs]
        uu_h = {(g, h): _dot(t_inv[(g, h)], jnp.where(head_lanes[h], rhs[g], 0.0)) for g, h in chains}
        uu = [uu_h[(g, 0)] + uu_h[(g, 1)] for g in gs]
        o_h = {(g, h): _dot(ab[(g, h)][L:] * incl, jnp.where(head_lanes[h], uu[g], 0.0)) for g, h in chains}
        upd = [_dot_tn(uu[g], b_end[g]) + _dot_tn(v[g], k_end[g]) for g in gs]
        for g in gs:
            o_ref[rows, cols[g]] = xs[g][L:] + out_h[(g, 0)] + out_h[(g, 1)] + o_h[(g, 0)] + o_h[(g, 1)]
            st_sc[g] = st[g] * w_end[g] + bd * upd[g]
        return carry

    lax.fori_loop(0, tb // L, chunk, 0)

    @pl.when(t == pl.num_programs(2) - 1)
    def _():
        for g in range(groups):
            s = st_sc[g]
            s_ref[0, 2 * g] = s[:N, :N]
            s_ref[0, 2 * g + 1] = s[N:, N:]


def _rwkv_chunk(seqs, nb, seq):
    groups = RWKV_GROUPS
    width = groups * LANES
    tb = _row_tile(seq, SEQ_BLOCK)
    nt = seq // tb
    spec = pl.BlockSpec((tb, width), lambda b, h, t: (b * nt + t, h))
    return pl.pallas_call(
        functools.partial(_rwkv_chunk_body, tb=tb, groups=groups), grid=(nb, D_MODEL // width, nt),
        in_specs=[spec] * 6,
        out_specs=[spec, pl.BlockSpec((1, 2 * groups, RWKV_N, RWKV_N), lambda b, h, t: (b, h, 0, 0))],
        out_shape=[jax.ShapeDtypeStruct((nb * seq, D_MODEL), F32),
                   jax.ShapeDtypeStruct((nb, RWKV_H, RWKV_N, RWKV_N), F32)],
        scratch_shapes=[pltpu.VMEM((groups, LANES, LANES), F32)],
        name="rwkv_chunk",
        compiler_params=pltpu.CompilerParams(dimension_semantics=("parallel", "parallel", "arbitrary"),
                                             vmem_limit_bytes=VMEM_LIMIT),
    )(*seqs)


def _rwkv_step_body(r_ref, lw_ref, k_ref, v_ref, a_ref, b_ref, s0_ref, o_ref, s_ref, *, bb, steps):
    N = RWKV_N
    ii, jj = _iota2((N, N))
    eye = (ii == jj).astype(F32)

    def head(b_i, h):
        s = s0_ref[b_i, h]
        for t in range(steps):
            row = lambda ref: ref[b_i, t, pl.ds(h, 1), :]
            sa = jnp.sum(s * row(a_ref), axis=1, keepdims=True)
            v_col = jnp.sum(eye * row(v_ref), axis=1, keepdims=True)
            s = s * jnp.exp(row(lw_ref)) + sa * row(b_ref) + v_col * row(k_ref)
            o_col = jnp.sum(s * row(r_ref), axis=1, keepdims=True)
            o_ref[b_i, t, pl.ds(h, 1), :] = jnp.sum(eye * o_col, axis=0, keepdims=True)
        s_ref[b_i, h] = s

    per_row = RWKV_H // STEP_HEADS

    def one(i, carry):
        b_i = i // per_row
        h0 = (i % per_row) * STEP_HEADS
        for hh in range(STEP_HEADS):
            head(b_i, h0 + hh)
        return carry

    lax.fori_loop(0, bb * per_row, one, 0)


def _rwkv_step(seqs, s0, nb, steps):
    bb = _row_tile(nb, 8)
    spec = pl.BlockSpec((bb, steps, RWKV_H, RWKV_N), lambda b: (b, 0, 0, 0))
    sspec = pl.BlockSpec((bb, RWKV_H, RWKV_N, RWKV_N), lambda b: (b, 0, 0, 0))
    seqs = [t.reshape(nb, steps, RWKV_H, RWKV_N) for t in seqs]
    o, s = pl.pallas_call(
        functools.partial(_rwkv_step_body, bb=bb, steps=steps), grid=(nb // bb,),
        in_specs=[spec] * 6 + [sspec], out_specs=[spec, sspec],
        out_shape=[jax.ShapeDtypeStruct((nb, steps, RWKV_H, RWKV_N), F32),
                   jax.ShapeDtypeStruct((nb, RWKV_H, RWKV_N, RWKV_N), F32)],
        name="rwkv_step",
        compiler_params=pltpu.CompilerParams(dimension_semantics=("parallel",), vmem_limit_bytes=VMEM_LIMIT),
    )(*seqs, s0)
    return o.reshape(nb * steps, D_MODEL), s


def _rwkv_out_body(x_ref, o_ref, g_ref, bon_ref, lw_ref, lb_ref, e_ref, et_ref, wo_ref, y_ref):
    e, et = e_ref[...], et_ref[...]
    o = o_ref[...]
    inv_n = 1.0 / RWKV_N
    d = o - _dot_sel(_dot_sel(o, e), et) * inv_n
    var = _dot_sel(_dot_sel(d * d, e), et) * inv_n
    on = d * lax.rsqrt(var + RWKV_LNX_EPS) * lw_ref[...] + lb_ref[...]
    y_ref[...] = x_ref[...] + _dot((on + bon_ref[...]) * g_ref[...], wo_ref[...])


def _shifted(u, first, nb, seq):
    u3 = u.reshape(nb, seq, D_MODEL)
    return jnp.concatenate([first[:, None, :], u3[:, :-1]], axis=1).reshape(nb * seq, D_MODEL)


def _rwkv_layer(x, u, dims, shift_s, wkv_s, p, v_first, vres):
    nb_p, seq_p, nb_s, seq_s = dims
    n_p = nb_p * seq_p
    prev = jnp.concatenate([_shifted(u[:n_p], jnp.zeros((nb_p, D_MODEL), F32), nb_p, seq_p),
                            _shifted(u[n_p:], shift_s, nb_s, seq_s)], axis=0)
    e, et = _head_sel(D_MODEL, RWKV_H)
    has_vres = vres is not None
    rows = [u, prev] + ([v_first] if has_vres else [])
    consts = [p['mu'], p['w_rkv'][0].astype(BF), p['w_rkv'][1].astype(BF), p['w_rkv'][2].astype(BF),
              p['w0'][None], p['w1'].astype(BF), p['w2'].astype(BF), p['a0'][None], p['a1'].astype(BF),
              p['a2'].astype(BF), p['g1'].astype(BF), p['g2'].astype(BF), p['k_k'][None], p['k_a'][None],
              p['r_k'].reshape(1, D_MODEL), e, et]
    if has_vres:
        consts += [vres[0][None], vres[1].astype(BF), vres[2].astype(BF)]
    tm = _row_tile(u.shape[0], 256)
    r, lw, k, v, a, b, gate, bonus = _rows_call(
        functools.partial(_rwkv_proj_body, has_vres=has_vres), rows, consts, [(D_MODEL, F32)] * 8, tm, "rwkv_proj")
    seqs = (r, lw, k, v, a, b)
    o_p, s_p = _rwkv_chunk(seqs, nb_p, seq_p)
    o_s, s_s = _rwkv_step([t[n_p:] for t in seqs], wkv_s, nb_s, seq_s)
    o = jnp.concatenate([o_p, o_s], axis=0)
    tm = _row_tile(u.shape[0], 512)
    x = _rows_call(_rwkv_out_body, [x, o, gate, bonus],
                   [p['lnx_w'][None], p['lnx_b'][None], e, et, p['w_o'].astype(BF)], [(D_MODEL, F32)], tm,
                   "rwkv_out")[0]
    u3p = u[:n_p].reshape(nb_p, seq_p, D_MODEL)
    u3s = u[n_p:].reshape(nb_s, seq_s, D_MODEL)
    return x, (v if not has_vres else v_first), (s_p, u3p[:, -1], s_s, u3s[:, -1])


def _t5_bias(dist, rel_bias):
    n = jnp.maximum(dist, 0)
    nf = jnp.maximum(n, 1).astype(F32)
    large = REL_MAX_EXACT + (jnp.log(nf / REL_MAX_EXACT) / math.log(REL_MAX_DIST / REL_MAX_EXACT)
                             * (REL_BUCKETS - REL_MAX_EXACT)).astype(jnp.int32)
    large = jnp.minimum(large, REL_BUCKETS - 1)
    bucket = jnp.where(n < REL_MAX_EXACT, n, large)
    onehot = (bucket[..., None] == jnp.arange(REL_BUCKETS)).astype(F32)
    return jnp.einsum('...b,bh->...h', onehot, rel_bias.astype(F32), precision=lax.Precision.HIGHEST)


def _lam_of(lam_ref, lam_init):
    lv = lam_ref[...]
    return (jnp.exp(jnp.sum(lv[0:1] * lv[1:2], axis=1, keepdims=True))
            - jnp.exp(jnp.sum(lv[2:3] * lv[3:4], axis=1, keepdims=True)) + lam_init)


def _attn_prompt_body(q_ref, k_ref, v_ref, bias_ref, far_ref, lam_ref, sw_ref, o_ref, *, tq, lam_init):
    qi = pl.program_id(2)
    lane = lax.broadcasted_iota(jnp.int32, (tq, DIFF_E), 1)
    q = q_ref[...] * (DIFF_DH ** -0.5)
    q1 = jnp.where(lane < DIFF_DH, q, 0.0).astype(BF)
    q2 = jnp.where(lane >= DIFF_DH, q, 0.0).astype(BF)
    ii, jj = _iota2((tq, tq))

    def step(carry, kj, bias):
        kb = k_ref[pl.ds(pl.multiple_of(kj * tq, tq), tq), :].astype(BF)
        vb = v_ref[pl.ds(pl.multiple_of(kj * tq, tq), tq), :].astype(BF)
        new = []
        for qm, (m, l, acc) in zip((q1, q2), carry):
            s = _dot_nt(qm, kb) + bias
            m_new = jnp.maximum(m, jnp.max(s, axis=1, keepdims=True))
            alpha = jnp.exp(m - m_new)
            pr = jnp.exp(s - m_new)
            new.append((m_new, l * alpha + jnp.sum(pr, axis=1, keepdims=True),
                        acc * alpha + jnp.dot(pr.astype(BF), vb, preferred_element_type=F32)))
        return tuple(new)

    init = tuple((jnp.full((tq, 1), NEG, F32), jnp.zeros((tq, 1), F32), jnp.zeros((tq, DIFF_E), F32))
                 for _ in range(2))
    carry = step(init, qi, jnp.where(ii >= jj, bias_ref[0, 0], NEG))
    prev_ok = qi >= 1
    carry = step(carry, jnp.maximum(qi - 1, 0), jnp.where(prev_ok, bias_ref[0, 1], NEG))
    far = far_ref[0, 0:1, :]
    carry = lax.fori_loop(0, jnp.maximum(qi - 1, 0), lambda kj, c: step(c, kj, far), carry)
    (_, l1, acc1), (_, l2, acc2) = carry
    o = acc1 / l1 - _lam_of(lam_ref, lam_init) * (acc2 / l2)
    o_ref[...] = _rms(o, sw_ref[...], SUBLN_EPS) * (1.0 - lam_init)


def _attn_prompt(q, k, v, nb, seq, rel_bias, lam, subln_w, lam_init):
    tq = _row_tile(seq, 256)
    assert tq >= REL_MAX_DIST or tq == seq
    nq = seq // tq
    d0 = jnp.arange(tq)[:, None] - jnp.arange(tq)[None, :]
    bias = jnp.stack([_t5_bias(d0, rel_bias), _t5_bias(d0 + tq, rel_bias)])
    bias = jnp.transpose(bias, (3, 0, 1, 2))
    far = jnp.broadcast_to(rel_bias[REL_BUCKETS - 1][:, None, None], (DIFF_H, 8, tq)).astype(F32)
    qspec = pl.BlockSpec((tq, DIFF_E), lambda h, b, i: (b * nq + i, h))
    kspec = pl.BlockSpec((seq, DIFF_E), lambda h, b, i: (b, h))
    return pl.pallas_call(
        functools.partial(_attn_prompt_body, tq=tq, lam_init=lam_init), grid=(DIFF_H, nb, nq),
        in_specs=[qspec, kspec, kspec,
                  pl.BlockSpec((1, 2, tq, tq), lambda h, b, i: (h, 0, 0, 0)),
                  pl.BlockSpec((1, 8, tq), lambda h, b, i: (h, 0, 0)),
                  pl.BlockSpec((4, DIFF_DH), lambda h, b, i: (0, 0)),
                  pl.BlockSpec((1, DIFF_E), lambda h, b, i: (0, 0))],
        out_specs=qspec, out_shape=jax.ShapeDtypeStruct((nb * seq, D_MODEL), F32), name="attn_prompt",
        compiler_params=pltpu.CompilerParams(dimension_semantics=("parallel", "parallel", "parallel"),
                                             vmem_limit_bytes=VMEM_LIMIT),
    )(q, k, v, bias, far, lam, subln_w[None])


def _attn_sample_body(pt_ref, qt_ref, *refs, n_pages, lam_init):
    del pt_ref
    k_refs, v_refs = refs[:n_pages], refs[n_pages:2 * n_pages]
    kn_ref, vn_ref, bp_ref, bn_ref, lam_ref, sw_ref, g_ref, o_ref = refs[2 * n_pages:]
    qt = qt_ref[0]
    page = k_refs[0].shape[1]
    scores = [jnp.dot(k_refs[i][0].astype(BF), qt, preferred_element_type=F32) + bp_ref[i * page:(i + 1) * page, :]
              for i in range(n_pages)]
    scores.append(jnp.dot(kn_ref[0].astype(BF), qt, preferred_element_type=F32) + bn_ref[...])
    m = functools.reduce(jnp.maximum, [jnp.max(s, axis=0, keepdims=True) for s in scores])
    probs = [jnp.exp(s - m) for s in scores]
    l = functools.reduce(jnp.add, [jnp.sum(pr, axis=0, keepdims=True) for pr in probs])
    acc = _dot_tn(vn_ref[0], probs[-1])
    for i in range(n_pages):
        acc = acc + _dot_tn(v_refs[i][0], probs[i])
    z = acc / l
    ri, ci = _iota2(z.shape)
    z = jnp.where((ri // DIFF_E) == (ci // (z.shape[1] // DIFF_H)), z, 0.0)
    o = _dot_sel(z, g_ref[0], 3) - _lam_of(lam_ref, lam_init) * _dot_sel(z, g_ref[1], 3)
    nq = o.shape[1]
    o3 = o.reshape(DIFF_H, DIFF_E, nq)
    ms = jnp.mean(o3 * o3, axis=1, keepdims=True)
    o3 = o3 * lax.rsqrt(ms + SUBLN_EPS)
    o_ref[0] = o3.reshape(DIFF_H * DIFF_E, nq) * sw_ref[...] * (1.0 - lam_init)


def _attn_sample(q, k_new, v_new, cache_k, cache_v, layer, page_table, rel_bias, lam, subln_w, lam_init):
    nb, n_pages = page_table.shape
    page = cache_k.shape[2]
    page_ids = page_table + layer * cache_k.shape[1]
    steps = q.shape[0] // nb
    past = n_pages * page
    width = DIFF_H * DIFF_E
    ncol = DIFF_H * steps * 2
    q5 = (q * (DIFF_DH ** -0.5)).reshape(nb, steps, DIFF_H, 2, DIFF_DH)
    hsel = jnp.eye(DIFF_H, dtype=F32)
    csel = jnp.eye(2, dtype=F32)
    qt = jnp.einsum('bihcd,hg,ce->bhcdgie', q5, hsel, csel).reshape(nb, width, ncol).astype(BF)
    q_pos = past + jnp.arange(steps)
    bias_all = _t5_bias(q_pos[None, :] - jnp.arange(past + steps)[:, None], rel_bias)
    vis = (jnp.arange(past + steps)[:, None] <= q_pos[None, :])[..., None]
    bias_all = jnp.where(vis, bias_all, NEG)
    bias_all = jnp.broadcast_to(jnp.transpose(bias_all, (0, 2, 1))[..., None], (past + steps, DIFF_H, steps, 2))
    bias_all = bias_all.reshape(past + steps, ncol)
    pad = 8 - steps
    bias_new = jnp.concatenate([bias_all[past:], jnp.full((pad, ncol), NEG, F32)], axis=0)
    kn = jnp.pad(k_new.reshape(nb, steps, width), ((0, 0), (0, pad), (0, 0)))
    vn = jnp.pad(v_new.reshape(nb, steps, width), ((0, 0), (0, pad), (0, 0)))
    col = jnp.arange(ncol)
    gsel = jnp.stack([((col[:, None] // 2) % steps == jnp.arange(steps)[None, :]) & (col[:, None] % 2 == c)
                      for c in range(2)]).astype(BF)
    sw_col = jnp.tile(subln_w, DIFF_H)[:, None]
    page_specs = [pl.BlockSpec((1, page, width), lambda b, pt, i=i: (pt[b, i], 0, 0)) for i in range(n_pages)]
    grid_spec = pltpu.PrefetchScalarGridSpec(
        num_scalar_prefetch=1, grid=(nb,),
        in_specs=[pl.BlockSpec((1, width, ncol), lambda b, pt: (b, 0, 0))] + page_specs + page_specs + [
            pl.BlockSpec((1, 8, width), lambda b, pt: (b, 0, 0)),
            pl.BlockSpec((1, 8, width), lambda b, pt: (b, 0, 0)),
            pl.BlockSpec((past, ncol), lambda b, pt: (0, 0)),
            pl.BlockSpec((8, ncol), lambda b, pt: (0, 0)),
            pl.BlockSpec((4, DIFF_DH), lambda b, pt: (0, 0)),
            pl.BlockSpec((width, 1), lambda b, pt: (0, 0)),
            pl.BlockSpec((2, ncol, steps), lambda b, pt: (0, 0, 0))],
        out_specs=pl.BlockSpec((1, width, steps), lambda b, pt: (b, 0, 0)))
    ck = cache_k.reshape(-1, page, width)
    cv = cache_v.reshape(-1, page, width)
    o_t = pl.pallas_call(
        functools.partial(_attn_sample_body, n_pages=n_pages, lam_init=lam_init), grid_spec=grid_spec,
        out_shape=jax.ShapeDtypeStruct((nb, width, steps), F32), name="attn_sample",
        compiler_params=pltpu.CompilerParams(dimension_semantics=("parallel",), vmem_limit_bytes=VMEM_LIMIT),
    )(page_ids, qt, *([ck] * n_pages), *([cv] * n_pages), kn, vn, bias_all[:past], bias_new, lam, sw_col, gsel)
    return jnp.transpose(o_t, (0, 2, 1)).reshape(nb * steps, width)


def _attn_layer(x, u, dims, cache_k, cache_v, layer, page_table, p, lam_init, rel_bias):
    nb_p, seq_p, nb_s, seq_s = dims
    n_p = nb_p * seq_p
    q, k, v = _linear(u, p['w_qkv'], 3, "attn_qkv")
    o_p = _attn_prompt(q, k, v, nb_p, seq_p, rel_bias, p['lam'], p['subln_w'], lam_init)
    o_s = _attn_sample(q[n_p:], k[n_p:], v[n_p:], cache_k, cache_v, layer, page_table, rel_bias, p['lam'],
                       p['subln_w'], lam_init)
    x = _linear_res(x, jnp.concatenate([o_p, o_s], axis=0), p['w_o'], "attn_out")
    shp_p = (nb_p, seq_p, DIFF_H, DIFF_E)
    shp_s = (nb_s, seq_s, DIFF_H, DIFF_E)
    return x, (k[:n_p].reshape(shp_p), v[:n_p].reshape(shp_p), k[n_p:].reshape(shp_s), v[n_p:].reshape(shp_s))


def _gdn_proj_body(u_ref, wqkv_ref, wz_ref, wb_ref, wa_ref, al_ref, dt_ref, eb_ref, qkv_o, z_o, beta_o, g_o):
    u = u_ref[...].astype(BF)
    qkv_o[...] = jnp.dot(u, wqkv_ref[...], preferred_element_type=F32)
    z_o[...] = jnp.dot(u, wz_ref[...], preferred_element_type=F32)
    beta = _sigmoid(jnp.dot(u, wb_ref[...], preferred_element_type=F32))
    g = -jnp.exp(al_ref[...]) * _softplus(jnp.dot(u, wa_ref[...], preferred_element_type=F32) + dt_ref[...])
    eb = eb_ref[...]
    beta_o[...] = _dot_sel(beta, eb, 3)
    g_o[...] = _dot_sel(g, eb, 3)


def _gdn_post(conv, q_o, k_o, v_o):
    c = conv * _sigmoid(conv)
    hk = GDN_H * GDN_DK
    for h in range(GDN_H):
        qs = c[:, h * GDN_DK:(h + 1) * GDN_DK]
        ks = c[:, hk + h * GDN_DK:hk + (h + 1) * GDN_DK]
        q_o[:, h * GDN_DK:(h + 1) * GDN_DK] = qs * lax.rsqrt(jnp.sum(qs * qs, axis=1, keepdims=True) + 1e-6) * (GDN_DK ** -0.5)
        k_o[:, h * GDN_DK:(h + 1) * GDN_DK] = ks * lax.rsqrt(jnp.sum(ks * ks, axis=1, keepdims=True) + 1e-6)
    v_o[...] = c[:, 2 * hk:]


def _gdn_conv_prompt_body(x_ref, halo_ref, w_ref, q_o, k_o, v_o, *, tiles_per_seq):
    i = pl.program_id(0)
    x = x_ref[...]
    tm = x.shape[0]
    halo = jnp.where(i % tiles_per_seq == 0, 0.0, halo_ref[...])
    row8 = lax.broadcasted_iota(jnp.int32, (8, x.shape[1]), 0)
    w = w_ref[...]
    conv = w[GDN_CONV - 1:GDN_CONV] * x
    for j in range(GDN_CONV - 1):
        sh = GDN_CONV - 1 - j
        rolled = pltpu.roll(x, sh, 0)
        head = jnp.where(row8 < sh, pltpu.roll(halo, sh, 0), rolled[:8])
        tap = jnp.concatenate([head, rolled[8:]], axis=0) if tm > 8 else head
        conv = conv + w[j:j + 1] * tap
    _gdn_post(conv, q_o, k_o, v_o)


def _gdn_conv_taps_body(t0_ref, t1_ref, t2_ref, t3_ref, w_ref, q_o, k_o, v_o):
    w = w_ref[...]
    conv = w[0:1] * t0_ref[...] + w[1:2] * t1_ref[...] + w[2:3] * t2_ref[...] + w[3:4] * t3_ref[...]
    _gdn_post(conv, q_o, k_o, v_o)


def _gdn_chunk_body(q_ref, k_ref, v_ref, beta_ref, g_ref, o_ref, s_ref, st_sc, *, tb, groups):
    L, P = CHUNK, LANES
    t = pl.program_id(2)

    @pl.when(t == 0)
    def _():
        st_sc[...] = jnp.zeros(st_sc.shape, F32)

    ii, jj = _iota2((L, L))
    strict = ii > jj
    incl = ii >= jj
    ci, cj = _iota2((2 * L, L))
    cum_sel = ((cj <= ci) | (ci >= L)).astype(F32).astype(BF)

    gs = range(groups)

    def chunk(c, carry):
        rows = pl.ds(pl.multiple_of(c * L, L), L)
        cols = [slice(g * P, (g + 1) * P) for g in gs]
        st = [st_sc[g] for g in gs]
        cs = [_sel_dot(cum_sel, g_ref[rows, cols[g]]) for g in gs]
        q = [q_ref[rows, cols[g]] for g in gs]
        k = [k_ref[rows, cols[g]] for g in gs]
        kk = [_dot_nt(k[g], k[g]) for g in gs]
        qk = [_dot_nt(q[g], k[g]) for g in gs]
        n_mats, dec_i, e_gc, beta = [], [], [], []
        for g in gs:
            gc = cs[g][:L]
            diff = gc[:, :L] - gc.T[:L, :]
            beta.append(beta_ref[rows, cols[g]])
            n_mats.append(-(beta[g][:, :L] * kk[g] * jnp.exp(jnp.where(strict, diff, NEG))))
            dec_i.append(jnp.exp(jnp.where(incl, diff, NEG)))
            e_gc.append(jnp.exp(gc))
        t_inv = _tri_inv(n_mats, L)
        wv = [_dot(t_inv[g], beta[g] * v_ref[rows, cols[g]]) for g in gs]
        wk = [_dot(t_inv[g], beta[g] * e_gc[g] * k[g]) for g in gs]
        uu = [wv[g] - _dot(wk[g], st[g]) for g in gs]
        qs = [_dot(q[g] * e_gc[g], st[g]) for g in gs]
        intra = [_dot(qk[g] * dec_i[g], uu[g]) for g in gs]
        upd = [_dot_tn(k[g] * jnp.exp(cs[g][L:] - cs[g][:L]), uu[g]) for g in gs]
        for g in gs:
            o_ref[rows, cols[g]] = qs[g] + intra[g]
            st_sc[g] = jnp.exp(cs[g][L:L + 1]) * st[g] + upd[g]
        return carry

    lax.fori_loop(0, tb // L, chunk, 0)

    @pl.when(t == pl.num_programs(2) - 1)
    def _():
        for g in range(groups):
            s_ref[0, g] = st_sc[g]


def _gdn_step_body(q_ref, k_ref, v_ref, beta_ref, g_ref, s0_ref, o_ref, s_ref, *, bb, steps):
    ii, jj = _iota2((GDN_DK, GDN_DK))
    eye = (ii == jj).astype(F32)

    def one_row(b_i):
        s = s0_ref[b_i, 0]
        for t in range(steps):
            row = lambda ref: ref[b_i, pl.ds(t, 1), :]
            k_col = jnp.sum(eye * row(k_ref), axis=1, keepdims=True)
            q_col = jnp.sum(eye * row(q_ref), axis=1, keepdims=True)
            beta = row(beta_ref)
            ks = jnp.sum(k_col * s, axis=0, keepdims=True)
            s = jnp.exp(row(g_ref)) * (s - (beta * k_col) * ks) + (beta * k_col) * row(v_ref)
            o_ref[b_i, pl.ds(t, 1), :] = jnp.sum(q_col * s, axis=0, keepdims=True)
        s_ref[b_i, 0] = s

    together = math.gcd(bb, STEP_HEADS)

    def one(i, carry):
        for r in range(together):
            one_row(i * together + r)
        return carry

    lax.fori_loop(0, bb // together, one, 0)


def _gdn_out_body(x_ref, o_ref, z_ref, nw_ref, wo_ref, y_ref):
    z = z_ref[...]
    nw = nw_ref[...]
    o = o_ref[...]
    parts = []
    for h in range(GDN_H):
        sl = slice(h * GDN_DV, (h + 1) * GDN_DV)
        parts.append(_rms(o[:, sl], nw, NORM_EPS))
    on = jnp.concatenate(parts, axis=1) * (z * _sigmoid(z))
    y_ref[...] = x_ref[...] + _dot(on, wo_ref[...])


def _gdn_layer(x, u, dims, conv_s, state_s, p):
    nb_p, seq_p, nb_s, seq_s = dims
    n_p = nb_p * seq_p
    n = u.shape[0]
    hv = GDN_H * GDN_DV
    w_in = p['w_in']
    pad = lambda w: jnp.pad(w, ((0, 0), (0, LANES - w.shape[-1])))
    eb = (jnp.arange(LANES)[:, None] == jnp.arange(hv)[None, :] // GDN_DV).astype(BF)
    qkv, z, beta, g = _rows_call(
        _gdn_proj_body, [u],
        [w_in[:, :GDN_QKV].astype(BF), w_in[:, GDN_QKV:GDN_QKV + hv].astype(BF),
         pad(w_in[:, GDN_QKV + hv:GDN_QKV + hv + GDN_H]).astype(BF), pad(w_in[:, GDN_QKV + hv + GDN_H:]).astype(BF),
         pad(p['a_log'][None]), pad(p['dt_bias'][None]), eb],
        [(GDN_QKV, F32), (hv, F32), (hv, F32), (hv, F32)], _row_tile(n, 256), "gdn_proj")
    tm = _row_tile(seq_p, 256)
    qkv_p = qkv[:n_p]
    cw = p['conv_w']
    outs3 = [jax.ShapeDtypeStruct((n_p, hv), F32)] * 3
    q_p, k_p, v_p = pl.pallas_call(
        functools.partial(_gdn_conv_prompt_body, tiles_per_seq=seq_p // tm), grid=(n_p // tm,),
        in_specs=[pl.BlockSpec((tm, GDN_QKV), lambda i: (i, 0)),
                  pl.BlockSpec((8, GDN_QKV), lambda i: (jnp.maximum(i * (tm // 8) - 1, 0), 0)),
                  pl.BlockSpec((GDN_CONV, GDN_QKV), lambda i: (0, 0))],
        out_specs=[pl.BlockSpec((tm, hv), lambda i: (i, 0))] * 3, out_shape=outs3, name="gdn_conv_prompt",
        compiler_params=pltpu.CompilerParams(dimension_semantics=("parallel",), vmem_limit_bytes=VMEM_LIMIT),
    )(qkv_p, qkv_p, cw)
    qkv_s = qkv[n_p:].reshape(nb_s, seq_s, GDN_QKV)
    xp_s = jnp.concatenate([conv_s, qkv_s], axis=1)
    taps = [xp_s[:, j:j + seq_s].reshape(nb_s * seq_s, GDN_QKV) for j in range(GDN_CONV)]
    q_s, k_s, v_s = _rows_call(_gdn_conv_taps_body, taps, [cw], [(hv, F32)] * 3, _row_tile(nb_s * seq_s, 256),
                               "gdn_conv_sample")
    groups = GDN_GROUPS
    tb = _row_tile(seq_p, SEQ_BLOCK)
    nt = seq_p // tb
    spec = pl.BlockSpec((tb, groups * LANES), lambda b, h, t: (b * nt + t, h))
    o_p, s_p = pl.pallas_call(
        functools.partial(_gdn_chunk_body, tb=tb, groups=groups), grid=(nb_p, GDN_H // groups, nt),
        in_specs=[spec] * 5,
        out_specs=[spec, pl.BlockSpec((1, groups, GDN_DK, GDN_DV), lambda b, h, t: (b, h, 0, 0))],
        out_shape=[jax.ShapeDtypeStruct((n_p, hv), F32), jax.ShapeDtypeStruct((nb_p, GDN_H, GDN_DK, GDN_DV), F32)],
        scratch_shapes=[pltpu.VMEM((groups, GDN_DK, GDN_DV), F32)],
        name="gdn_chunk",
        compiler_params=pltpu.CompilerParams(dimension_semantics=("parallel", "parallel", "arbitrary"),
                                             vmem_limit_bytes=VMEM_LIMIT),
    )(q_p, k_p, v_p, beta, g)
    bb = _row_tile(nb_s, 8)
    sspec = pl.BlockSpec((bb, seq_s, LANES), lambda b, h: (b, 0, h))
    stspec = pl.BlockSpec((bb, 1, GDN_DK, GDN_DV), lambda b, h: (b, h, 0, 0))
    r3 = lambda t: t.reshape(nb_s, seq_s, hv)
    o_s, s_s = pl.pallas_call(
        functools.partial(_gdn_step_body, bb=bb, steps=seq_s), grid=(nb_s // bb, GDN_H),
        in_specs=[sspec] * 5 + [stspec], out_specs=[sspec, stspec],
        out_shape=[jax.ShapeDtypeStruct((nb_s, seq_s, hv), F32),
                   jax.ShapeDtypeStruct((nb_s, GDN_H, GDN_DK, GDN_DV), F32)],
        name="gdn_step",
        compiler_params=pltpu.CompilerParams(dimension_semantics=("parallel", "parallel"),
                                             vmem_limit_bytes=VMEM_LIMIT),
    )(r3(q_s), r3(k_s), r3(v_s), r3(beta[n_p:]), r3(g[n_p:]), state_s)
    o = jnp.concatenate([o_p, o_s.reshape(nb_s * seq_s, hv)], axis=0)
    x = _rows_call(_gdn_out_body, [x, o, z], [p['norm_w'][None], p['w_o'].astype(BF)], [(D_MODEL, F32)],
                   _row_tile(n, 512), "gdn_out")[0]
    keep = GDN_CONV - 1
    conv_p = jnp.concatenate([jnp.zeros((nb_p, keep, GDN_QKV), F32), qkv_p.reshape(nb_p, seq_p, GDN_QKV)],
                             axis=1)[:, -keep:]
    return x, (s_p, conv_p, s_s, xp_s[:, -keep:])


def kernel(x_prompt, x_sample, state_rwkv_wkv, state_rwkv_shift, cache_attn_k, cache_attn_v, state_gdn, state_gdn_conv, page_table, norm_w, final_norm_w, ffn_w_gate, ffn_w_up, ffn_w_down, rwkv_mu, rwkv_w_rkv, rwkv_w_o, rwkv_w0, rwkv_w1, rwkv_w2, rwkv_a0, rwkv_a1, rwkv_a2, rwkv_g1, rwkv_g2, rwkv_k_k, rwkv_k_a, rwkv_r_k, rwkv_lnx_w, rwkv_lnx_b, rwkv_v0, rwkv_v1, rwkv_v2, attn_w_qkv, attn_w_o, attn_lambda, attn_subln_w, rel_bias, gdn_w_in, gdn_conv_w, gdn_a_log, gdn_dt_bias, gdn_norm_w, gdn_w_o):
    nb_p, seq_p, _ = x_prompt.shape
    nb_s, seq_s, _ = x_sample.shape
    dims = (nb_p, seq_p, nb_s, seq_s)
    n_p = nb_p * seq_p
    depth = norm_w.shape[0]
    x = jnp.concatenate([x_prompt.reshape(n_p, D_MODEL), x_sample.reshape(nb_s * seq_s, D_MODEL)], axis=0)
    v_first = None
    rw, at, gd = [], [], []
    for i in range(depth):
        kind, j = i % 3, i // 3
        x, u = _ffn(x, norm_w[i, 0], ffn_w_gate[i, 0], ffn_w_up[i, 0], ffn_w_down[i, 0], norm_w[i, 1])
        if kind == 0:
            p = dict(mu=rwkv_mu[j], w_rkv=rwkv_w_rkv[j], w_o=rwkv_w_o[j], w0=rwkv_w0[j], w1=rwkv_w1[j],
                     w2=rwkv_w2[j], a0=rwkv_a0[j], a1=rwkv_a1[j], a2=rwkv_a2[j], g1=rwkv_g1[j], g2=rwkv_g2[j],
                     k_k=rwkv_k_k[j], k_a=rwkv_k_a[j], r_k=rwkv_r_k[j], lnx_w=rwkv_lnx_w[j], lnx_b=rwkv_lnx_b[j])
            vres = None if j == 0 else (rwkv_v0[j - 1], rwkv_v1[j - 1], rwkv_v2[j - 1])
            x, v_first, st = _rwkv_layer(x, u, dims, state_rwkv_shift[j], state_rwkv_wkv[j], p, v_first, vres)
            rw.append(st)
        elif kind == 1:
            p = dict(w_qkv=attn_w_qkv[j], w_o=attn_w_o[j], lam=attn_lambda[j], subln_w=attn_subln_w[j])
            lam_init = 0.8 - 0.6 * math.exp(-0.3 * i)
            x, st = _attn_layer(x, u, dims, cache_attn_k, cache_attn_v, j, page_table, p, lam_init, rel_bias)
            at.append(st)
        else:
            p = dict(w_in=gdn_w_in[j], conv_w=gdn_conv_w[j], a_log=gdn_a_log[j], dt_bias=gdn_dt_bias[j],
                     norm_w=gdn_norm_w[j], w_o=gdn_w_o[j])
            x, st = _gdn_layer(x, u, dims, state_gdn_conv[j], state_gdn[j], p)
            gd.append(st)
        g_next = final_norm_w if i == depth - 1 else jnp.ones((D_MODEL,), F32)
        x, y = _ffn(x, norm_w[i, 2], ffn_w_gate[i, 1], ffn_w_up[i, 1], ffn_w_down[i, 1], g_next)
    stack = lambda lst, k: jnp.stack([t[k] for t in lst])
    return (y[:n_p].reshape(x_prompt.shape), y[n_p:].reshape(x_sample.shape),
            stack(rw, 0), stack(rw, 1), stack(at, 0), stack(at, 1), stack(gd, 0), stack(gd, 1),
            stack(rw, 2), stack(rw, 3), stack(at, 2), stack(at, 3), stack(gd, 2), stack(gd, 3))
```

```python
import functools
import math

import jax
import jax.numpy as jnp
from jax import lax
from jax.experimental import pallas as pl
from jax.experimental.pallas import tpu as pltpu

F32 = jnp.float32
BF = jnp.bfloat16

D_MODEL = 1024
NORM_EPS = 1e-6
RWKV_N = 64
RWKV_H = D_MODEL // RWKV_N
RWKV_LNX_EPS = 64e-5
DIFF_H = 8
DIFF_DH = 64
DIFF_E = 128
SUBLN_EPS = 1e-5
REL_BUCKETS = 32
REL_MAX_EXACT = 16
REL_MAX_DIST = 128
GDN_H = 8
GDN_DK = 128
GDN_DV = 128
GDN_CONV = 4
GDN_QKV = GDN_H * (2 * GDN_DK + GDN_DV)
CHUNK = 64
SEQ_BLOCK = 512
RWKV_GROUPS = 8
GDN_GROUPS = 8
STEP_HEADS = 4
LANES = 128
NEG = -1e30
VMEM_LIMIT = 56 * 1024 * 1024


def _dot(a, b):
    return jnp.dot(a.astype(BF), b.astype(BF), preferred_element_type=F32)


def _dot_nt(a, b):
    return lax.dot_general(a.astype(BF), b.astype(BF), (((1,), (1,)), ((), ())), preferred_element_type=F32)


def _dot_tn(a, b):
    return lax.dot_general(a.astype(BF), b.astype(BF), (((0,), (0,)), ((), ())), preferred_element_type=F32)


def _split(x, n):
    out = []
    for _ in range(n):
        h = x.astype(BF)
        out.append(h)
        x = x - h.astype(F32)
    return out


def _sel_dot(sel, x, n=3):
    return sum(jnp.dot(sel, t, preferred_element_type=F32) for t in _split(x, n))


def _dot_sel(x, sel, n=2):
    return sum(jnp.dot(t, sel, preferred_element_type=F32) for t in _split(x, n))


def _rms(x, g, eps):
    return x * lax.rsqrt(jnp.mean(x * x, axis=-1, keepdims=True) + eps) * g


def _sigmoid(x):
    return 1.0 / (1.0 + jnp.exp(-x))


def _softplus(x):
    return jnp.maximum(x, 0.0) + jnp.log(1.0 + jnp.exp(-jnp.abs(x)))


def _iota2(shape):
    return lax.broadcasted_iota(jnp.int32, shape, 0), lax.broadcasted_iota(jnp.int32, shape, 1)


def _tri_inv(n_mats, size):
    ii, jj = _iota2((size, size))
    eye = (ii == jj).astype(F32)
    xs = [eye + jnp.where((ii >> 1) == (jj >> 1), n, 0.0) for n in n_mats]
    lvl = 1
    while (2 << lvl) <= size:
        sel = ((ii >> (lvl + 1)) == (jj >> (lvl + 1))) & (((ii >> lvl) & 1) == 1) & (((jj >> lvl) & 1) == 0)
        ys = [_dot(x, jnp.where(sel, n, 0.0)) for x, n in zip(xs, n_mats)]
        xs = [x + _dot(y, x) for x, y in zip(xs, ys)]
        lvl += 1
    return xs


def _rows_call(body, rows, consts, outs, tm, name):
    n = rows[0].shape[0]
    assert n % tm == 0, (n, tm)
    in_specs = [pl.BlockSpec((tm, a.shape[1]), lambda i: (i, 0)) for a in rows]
    for a in consts:
        in_specs.append(pl.BlockSpec(a.shape, lambda i, nd=a.ndim: (0,) * nd, pipeline_mode=pl.Buffered(1)))
    out_specs = [pl.BlockSpec((tm, c), lambda i: (i, 0)) for c, _ in outs]
    out_shape = [jax.ShapeDtypeStruct((n, c), dt) for c, dt in outs]
    return pl.pallas_call(
        body, grid=(n // tm,), in_specs=in_specs, out_specs=out_specs, out_shape=out_shape, name=name,
        compiler_params=pltpu.CompilerParams(dimension_semantics=("parallel",), vmem_limit_bytes=VMEM_LIMIT),
    )(*rows, *consts)


def _row_tile(n, want):
    tm = min(want, n)
    while n % tm:
        tm //= 2
    return tm


def _ffn_body(x_ref, g_ref, wg_ref, wu_ref, wd_ref, g2_ref, o_ref, u_ref, *, nchunk):
    x = x_ref[...]
    h = _rms(x, g_ref[...], NORM_EPS).astype(BF)
    fc = wg_ref.shape[1] // nchunk
    acc = jnp.zeros_like(x)
    for c in range(nchunk):
        sl = slice(c * fc, (c + 1) * fc)
        gate = jnp.dot(h, wg_ref[:, sl], preferred_element_type=F32)
        up = jnp.dot(h, wu_ref[:, sl], preferred_element_type=F32)
        act = (gate * _sigmoid(gate) * up).astype(BF)
        acc = acc + jnp.dot(act, wd_ref[sl, :], preferred_element_type=F32)
    y = x + 0.5 * acc
    o_ref[...] = y
    u_ref[...] = _rms(y, g2_ref[...], NORM_EPS)


def _ffn(x, g, wg, wu, wd, g_next):
    nchunk = 2 if wg.shape[1] % 256 == 0 else 1
    body = functools.partial(_ffn_body, nchunk=nchunk)
    tm = _row_tile(x.shape[0], 512)
    return _rows_call(body, [x], [g[None], wg.astype(BF), wu.astype(BF), wd.astype(BF), g_next[None]],
                      [(D_MODEL, F32), (D_MODEL, F32)], tm, "ffn_half")


def _linear_body(*refs, n_out, residual):
    if residual:
        x_ref, u_ref, w_ref = refs[:3]
        outs = refs[3:]
    else:
        u_ref, w_ref = refs[:2]
        outs = refs[2:]
    y = jnp.dot(u_ref[...].astype(BF), w_ref[...], preferred_element_type=F32)
    if residual:
        outs[0][...] = x_ref[...] + y
    else:
        c = y.shape[1] // n_out
        for k, o in enumerate(outs):
            o[...] = y[:, k * c:(k + 1) * c]


def _linear(u, w, n_out, name):
    body = functools.partial(_linear_body, n_out=n_out, residual=False)
    tm = _row_tile(u.shape[0], 512)
    c = w.shape[1] // n_out
    return _rows_call(body, [u], [w.astype(BF)], [(c, F32)] * n_out, tm, name)


def _linear_res(x, u, w, name):
    body = functools.partial(_linear_body, n_out=1, residual=True)
    tm = _row_tile(u.shape[0], 512)
    return _rows_call(body, [x, u], [w.astype(BF)], [(w.shape[1], F32)], tm, name)[0]


def _head_sel(width, heads):
    e = (jnp.arange(width)[:, None] // (width // heads) == jnp.arange(heads)[None, :]).astype(BF)
    return e, e.T


def _rwkv_proj_body(*refs, has_vres):
    (u_ref, prev_ref) = refs[:2]
    k0 = 2
    if has_vres:
        vf_ref = refs[2]
        k0 = 3
    (mu_ref, wr_ref, wk_ref, wv_ref, w0_ref, w1_ref, w2_ref, a0_ref, a1_ref, a2_ref, g1_ref, g2_ref,
     kk_ref, ka_ref, rk_ref, e_ref, et_ref) = refs[k0:k0 + 17]
    k1 = k0 + 17
    if has_vres:
        v0_ref, v1_ref, v2_ref = refs[k1:k1 + 3]
        k1 += 3
    r_o, lw_o, k_o, v_o, a_o, b_o, g_o, bon_o = refs[k1:]
    u = u_ref[...]
    xx = prev_ref[...] - u
    mu = mu_ref[...]
    xr, xw, xk = u + xx * mu[0:1], u + xx * mu[1:2], u + xx * mu[2:3]
    xv, xa, xg = u + xx * mu[3:4], u + xx * mu[4:5], u + xx * mu[5:6]
    r = _dot(xr, wr_ref[...])
    k = _dot(xk, wk_ref[...])
    v = _dot(xv, wv_ref[...])
    wlog = -_softplus(-(w0_ref[...] + _dot(jnp.tanh(_dot(xw, w1_ref[...])), w2_ref[...]))) - 0.5
    lw_o[...] = -jnp.exp(wlog)
    if has_vres:
        v = v + (vf_ref[...] - v) * _sigmoid(v0_ref[...] + _dot(_dot(xv, v1_ref[...]), v2_ref[...]))
    a = _sigmoid(a0_ref[...] + _dot(_dot(xa, a1_ref[...]), a2_ref[...]))
    g_o[...] = _dot(_sigmoid(_dot(xg, g1_ref[...])), g2_ref[...])
    e, et = e_ref[...], et_ref[...]
    kk = k * kk_ref[...]
    kk = kk * lax.rsqrt(_dot_sel(_dot_sel(kk * kk, e), et) + 1e-24)
    k_mod = k * (1.0 + (a - 1.0) * ka_ref[...])
    r_o[...] = r
    k_o[...] = k_mod
    v_o[...] = v
    a_o[...] = -kk
    b_o[...] = kk * a
    bon_o[...] = _dot_sel(_dot_sel(r * k_mod * rk_ref[...], e), et) * v


def _rwkv_chunk_body(r_ref, lw_ref, k_ref, v_ref, a_ref, b_ref, o_ref, s_ref, st_sc, *, tb, groups):
    L, P, N = CHUNK, LANES, RWKV_N
    t = pl.program_id(2)

    @pl.when(t == 0)
    def _():
        st_sc[...] = jnp.zeros(st_sc.shape, F32)

    ii, jj = _iota2((L, L))
    strict = (ii > jj).astype(F32)
    incl = (ii >= jj).astype(F32)
    ci, cj = _iota2((2 * L, L))
    cum_sel = ((cj <= ci) | (ci >= L)).astype(F32).astype(BF)
    lane = lax.broadcasted_iota(jnp.int32, (L, P), 1)
    lane2 = lax.broadcasted_iota(jnp.int32, (2 * L, P), 1)
    bi, bj = _iota2((P, P))
    bd = ((bi >= N) == (bj >= N)).astype(F32)

    gs = range(groups)
    chains = [(g, h) for g in gs for h in range(2)]
    head_lanes = [(lane >= N) == bool(h) for h in range(2)]
    head_lanes2 = [(lane2 >= N) == bool(h) for h in range(2)]

    def chunk(c, carry):
        rows = pl.ds(pl.multiple_of(c * L, L), L)
        cols = [slice(g * P, (g + 1) * P) for g in gs]
        st = [st_sc[g] for g in gs]
        lw = [lw_ref[rows, cols[g]] for g in gs]
        cs = [_sel_dot(cum_sel, lw[g]) for g in gs]
        x, bt, kt, v, b_end, k_end, w_end = [], [], [], [], [], [], []
        for g in gs:
            gc, gl = cs[g][:L], cs[g][L:]
            r, k, a, b = r_ref[rows, cols[g]], k_ref[rows, cols[g]], a_ref[rows, cols[g]], b_ref[rows, cols[g]]
            e_neg = jnp.exp(-gc)
            e_end = jnp.exp(gl - gc)
            x.append(jnp.concatenate([a * jnp.exp(gc - lw[g]), r * jnp.exp(gc)], axis=0).astype(BF))
            bt.append((b * e_neg).astype(BF))
            kt.append((k * e_neg).astype(BF))
            v.append(v_ref[rows, cols[g]])
            b_end.append(b * e_end)
            k_end.append(k * e_end)
            w_end.append(jnp.exp(gl[0:1]))
        xs = [_dot_nt(x[g], st[g]) for g in gs]
        xh = {(g, h): jnp.where(head_lanes2[h], x[g], jnp.zeros_like(x[g])) for g, h in chains}
        ab = {ch: _dot_nt(xh[ch], bt[ch[0]]) for ch in chains}
        ak = {ch: _dot_nt(xh[ch], kt[ch[0]]) for ch in chains}
        t_inv = dict(zip(chains, _tri_inv([ab[ch][:L] * strict for ch in chains], L)))
        vh = {(g, h): jnp.where(head_lanes[h], v[g], 0.0) for g, h in chains}
        rhs_h = {ch: _dot(ak[ch][:L] * strict, vh[ch]) for ch in chains}
        out_h = {ch: _dot(ak[ch][L:] * incl, vh[ch]) for ch in chains}
        rhs = [xs[g][:L] + rhs_h[(g, 0)] + rhs_h[(g, 1)] for g in gs]
        uu_h = {(g, h): _dot(t_inv[(g, h)], jnp.where(head_lanes[h], rhs[g], 0.0)) for g, h in chains}
        uu = [uu_h[(g, 0)] + uu_h[(g, 1)] for g in gs]
        o_h = {(g, h): _dot(ab[(g, h)][L:] * incl, jnp.where(head_lanes[h], uu[g], 0.0)) for g, h in chains}
        upd = [_dot_tn(uu[g], b_end[g]) + _dot_tn(v[g], k_end[g]) for g in gs]
        for g in gs:
            o_ref[rows, cols[g]] = xs[g][L:] + out_h[(g, 0)] + out_h[(g, 1)] + o_h[(g, 0)] + o_h[(g, 1)]
            st_sc[g] = st[g] * w_end[g] + bd * upd[g]
        return carry

    lax.fori_loop(0, tb // L, chunk, 0)

    @pl.when(t == pl.num_programs(2) - 1)
    def _():
        for g in range(groups):
            s = st_sc[g]
            s_ref[0, 2 * g] = s[:N, :N]
            s_ref[0, 2 * g + 1] = s[N:, N:]


def _rwkv_chunk(seqs, nb, seq):
    groups = RWKV_GROUPS
    width = groups * LANES
    tb = _row_tile(seq, SEQ_BLOCK)
    nt = seq // tb
    spec = pl.BlockSpec((tb, width), lambda b, h, t: (b * nt + t, h))
    return pl.pallas_call(
        functools.partial(_rwkv_chunk_body, tb=tb, groups=groups), grid=(nb, D_MODEL // width, nt),
        in_specs=[spec] * 6,
        out_specs=[spec, pl.BlockSpec((1, 2 * groups, RWKV_N, RWKV_N), lambda b, h, t: (b, h, 0, 0))],
        out_shape=[jax.ShapeDtypeStruct((nb * seq, D_MODEL), F32),
                   jax.ShapeDtypeStruct((nb, RWKV_H, RWKV_N, RWKV_N), F32)],
        scratch_shapes=[pltpu.VMEM((groups, LANES, LANES), F32)],
        name="rwkv_chunk",
        compiler_params=pltpu.CompilerParams(dimension_semantics=("parallel", "parallel", "arbitrary"),
                                             vmem_limit_bytes=VMEM_LIMIT),
    )(*seqs)


def _rwkv_step_body(r_ref, lw_ref, k_ref, v_ref, a_ref, b_ref, s0_ref, o_ref, s_ref, *, bb, steps):
    N = RWKV_N
    ii, jj = _iota2((N, N))
    eye = (ii == jj).astype(F32)

    def head(b_i, h):
        s = s0_ref[b_i, h]
        for t in range(steps):
            row = lambda ref: ref[b_i, t, pl.ds(h, 1), :]
            sa = jnp.sum(s * row(a_ref), axis=1, keepdims=True)
            v_col = jnp.sum(eye * row(v_ref), axis=1, keepdims=True)
            s = s * jnp.exp(row(lw_ref)) + sa * row(b_ref) + v_col * row(k_ref)
            o_col = jnp.sum(s * row(r_ref), axis=1, keepdims=True)
            o_ref[b_i, t, pl.ds(h, 1), :] = jnp.sum(eye * o_col, axis=0, keepdims=True)
        s_ref[b_i, h] = s

    per_row = RWKV_H // STEP_HEADS

    def one(i, carry):
        b_i = i // per_row
        h0 = (i % per_row) * STEP_HEADS
        for hh in range(STEP_HEADS):
            head(b_i, h0 + hh)
        return carry

    lax.fori_loop(0, bb * per_row, one, 0)


def _rwkv_step(seqs, s0, nb, steps):
    bb = _row_tile(nb, 8)
    spec = pl.BlockSpec((bb, steps, RWKV_H, RWKV_N), lambda b: (b, 0, 0, 0))
    sspec = pl.BlockSpec((bb, RWKV_H, RWKV_N, RWKV_N), lambda b: (b, 0, 0, 0))
    seqs = [t.reshape(nb, steps, RWKV_H, RWKV_N) for t in seqs]
    o, s = pl.pallas_call(
        functools.partial(_rwkv_step_body, bb=bb, steps=steps), grid=(nb // bb,),
        in_specs=[spec] * 6 + [sspec], out_specs=[spec, sspec],
        out_shape=[jax.ShapeDtypeStruct((nb, steps, RWKV_H, RWKV_N), F32),
                   jax.ShapeDtypeStruct((nb, RWKV_H, RWKV_N, RWKV_N), F32)],
        name="rwkv_step",
        compiler_params=pltpu.CompilerParams(dimension_semantics=("parallel",), vmem_limit_bytes=VMEM_LIMIT),
    )(*seqs, s0)
    return o.reshape(nb * steps, D_MODEL), s


def _rwkv_out_body(x_ref, o_ref, g_ref, bon_ref, lw_ref, lb_ref, e_ref, et_ref, wo_ref, y_ref):
    e, et = e_ref[...], et_ref[...]
    o = o_ref[...]
    inv_n = 1.0 / RWKV_N
    d = o - _dot_sel(_dot_sel(o, e), et) * inv_n
    var = _dot_sel(_dot_sel(d * d, e), et) * inv_n
    on = d * lax.rsqrt(var + RWKV_LNX_EPS) * lw_ref[...] + lb_ref[...]
    y_ref[...] = x_ref[...] + _dot((on + bon_ref[...]) * g_ref[...], wo_ref[...])


def _shifted(u, first, nb, seq):
    u3 = u.reshape(nb, seq, D_MODEL)
    return jnp.concatenate([first[:, None, :], u3[:, :-1]], axis=1).reshape(nb * seq, D_MODEL)


def _rwkv_layer(x, u, dims, shift_s, wkv_s, p, v_first, vres):
    nb_p, seq_p, nb_s, seq_s = dims
    n_p = nb_p * seq_p
    prev = jnp.concatenate([_shifted(u[:n_p], jnp.zeros((nb_p, D_MODEL), F32), nb_p, seq_p),
                            _shifted(u[n_p:], shift_s, nb_s, seq_s)], axis=0)
    e, et = _head_sel(D_MODEL, RWKV_H)
    has_vres = vres is not None
    rows = [u, prev] + ([v_first] if has_vres else [])
    consts = [p['mu'], p['w_rkv'][0].astype(BF), p['w_rkv'][1].astype(BF), p['w_rkv'][2].astype(BF),
              p['w0'][None], p['w1'].astype(BF), p['w2'].astype(BF), p['a0'][None], p['a1'].astype(BF),
              p['a2'].astype(BF), p['g1'].astype(BF), p['g2'].astype(BF), p['k_k'][None], p['k_a'][None],
              p['r_k'].reshape(1, D_MODEL), e, et]
    if has_vres:
        consts += [vres[0][None], vres[1].astype(BF), vres[2].astype(BF)]
    tm = _row_tile(u.shape[0], 256)
    r, lw, k, v, a, b, gate, bonus = _rows_call(
        functools.partial(_rwkv_proj_body, has_vres=has_vres), rows, consts, [(D_MODEL, F32)] * 8, tm, "rwkv_proj")
    seqs = (r, lw, k, v, a, b)
    o_p, s_p = _rwkv_chunk(seqs, nb_p, seq_p)
    o_s, s_s = _rwkv_step([t[n_p:] for t in seqs], wkv_s, nb_s, seq_s)
    o = jnp.concatenate([o_p, o_s], axis=0)
    tm = _row_tile(u.shape[0], 512)
    x = _rows_call(_rwkv_out_body, [x, o, gate, bonus],
                   [p['lnx_w'][None], p['lnx_b'][None], e, et, p['w_o'].astype(BF)], [(D_MODEL, F32)], tm,
                   "rwkv_out")[0]
    u3p = u[:n_p].reshape(nb_p, seq_p, D_MODEL)
    u3s = u[n_p:].reshape(nb_s, seq_s, D_MODEL)
    return x, (v if not has_vres else v_first), (s_p, u3p[:, -1], s_s, u3s[:, -1])


def _t5_bias(dist, rel_bias):
    n = jnp.maximum(dist, 0)
    nf = jnp.maximum(n, 1).astype(F32)
    large = REL_MAX_EXACT + (jnp.log(nf / REL_MAX_EXACT) / math.log(REL_MAX_DIST / REL_MAX_EXACT)
                             * (REL_BUCKETS - REL_MAX_EXACT)).astype(jnp.int32)
    large = jnp.minimum(large, REL_BUCKETS - 1)
    bucket = jnp.where(n < REL_MAX_EXACT, n, large)
    onehot = (bucket[..., None] == jnp.arange(REL_BUCKETS)).astype(F32)
    return jnp.einsum('...b,bh->...h', onehot, rel_bias.astype(F32), precision=lax.Precision.HIGHEST)


def _lam_of(lam_ref, lam_init):
    lv = lam_ref[...]
    return (jnp.exp(jnp.sum(lv[0:1] * lv[1:2], axis=1, keepdims=True))
            - jnp.exp(jnp.sum(lv[2:3] * lv[3:4], axis=1, keepdims=True)) + lam_init)


ATTN_HEADS = 2


def _attn_prompt_body(q_ref, k_ref, v_ref, bias_ref, far_ref, lam_ref, sw_ref, o_ref, kb_sc, vt_sc, *, tq, lam_init):
    qi = pl.program_id(2)
    E, dh = DIFF_E, DIFF_DH
    nkv = kb_sc.shape[0]
    chains = [(hh, c) for hh in range(ATTN_HEADS) for c in range(2)]

    @pl.when(qi == 0)
    def _():
        for j in range(nkv):
            kb_sc[j] = k_ref[j * tq:(j + 1) * tq, :].astype(BF)
            vt_sc[j] = v_ref[j * tq:(j + 1) * tq, :].T.astype(BF)

    q_t = (q_ref[...] * (dh ** -0.5)).T
    row = lax.broadcasted_iota(jnp.int32, q_t.shape, 0)
    q_c = {(hh, c): jnp.where((row >= hh * E + c * dh) & (row < hh * E + (c + 1) * dh), q_t, 0.0).astype(BF)
           for hh, c in chains}
    kk, qq = _iota2((tq, tq))

    def step(carry, kj, bias):
        kb = kb_sc[kj]
        s = {ch: jnp.dot(kb, q_c[ch], preferred_element_type=F32) + bias[ch[0]] for ch in chains}
        m_new = {ch: jnp.maximum(carry[ch][0], jnp.max(s[ch], axis=0, keepdims=True)) for ch in chains}
        alpha = {ch: jnp.exp(carry[ch][0] - m_new[ch]) for ch in chains}
        pr = {ch: jnp.exp(s[ch] - m_new[ch]) for ch in chains}
        l_new = {ch: carry[ch][1] * alpha[ch] + jnp.sum(pr[ch], axis=0, keepdims=True) for ch in chains}
        vt = vt_sc[kj]
        pv = {(hh, c): jnp.dot(vt[hh * E:(hh + 1) * E, :], pr[(hh, c)].astype(BF), preferred_element_type=F32)
              for hh, c in chains}
        return {ch: (m_new[ch], l_new[ch], carry[ch][2] * alpha[ch] + pv[ch]) for ch in chains}

    init = {ch: (jnp.full((1, tq), NEG, F32), jnp.zeros((1, tq), F32), jnp.zeros((E, tq), F32)) for ch in chains}
    carry = step(init, qi, [jnp.where(kk <= qq, bias_ref[hh, 0], NEG) for hh in range(ATTN_HEADS)])
    carry = step(carry, jnp.maximum(qi - 1, 0),
                 [jnp.where(qi >= 1, bias_ref[hh, 1], NEG) for hh in range(ATTN_HEADS)])
    far = [far_ref[hh, 0:1, :] for hh in range(ATTN_HEADS)]
    carry = lax.fori_loop(0, jnp.maximum(qi - 1, 0), lambda kj, c: step(c, kj, far), carry)
    lam = _lam_of(lam_ref, lam_init)
    for hh in range(ATTN_HEADS):
        (_, l1, acc1), (_, l2, acc2) = carry[(hh, 0)], carry[(hh, 1)]
        o_t = acc1 / l1 - lam * (acc2 / l2)
        o_t = o_t * lax.rsqrt(jnp.mean(o_t * o_t, axis=0, keepdims=True) + SUBLN_EPS) * sw_ref[...]
        o_ref[:, hh * E:(hh + 1) * E] = o_t.T * (1.0 - lam_init)


def _attn_prompt(q, k, v, nb, seq, rel_bias, lam, subln_w, lam_init):
    tq = _row_tile(seq, 256)
    assert tq >= REL_MAX_DIST or tq == seq
    nq = seq // tq
    width = ATTN_HEADS * DIFF_E
    d0 = jnp.arange(tq)[None, :] - jnp.arange(tq)[:, None]
    bias = jnp.stack([_t5_bias(d0, rel_bias), _t5_bias(d0 + tq, rel_bias)])
    bias = jnp.transpose(bias, (3, 0, 1, 2))
    far = jnp.broadcast_to(rel_bias[REL_BUCKETS - 1][:, None, None], (DIFF_H, 8, tq)).astype(F32)
    qspec = pl.BlockSpec((tq, width), lambda h, b, i: (b * nq + i, h))
    kspec = pl.BlockSpec((seq, width), lambda h, b, i: (b, h))
    return pl.pallas_call(
        functools.partial(_attn_prompt_body, tq=tq, lam_init=lam_init), grid=(DIFF_H // ATTN_HEADS, nb, nq),
        in_specs=[qspec, kspec, kspec,
                  pl.BlockSpec((ATTN_HEADS, 2, tq, tq), lambda h, b, i: (h, 0, 0, 0)),
                  pl.BlockSpec((ATTN_HEADS, 8, tq), lambda h, b, i: (h, 0, 0)),
                  pl.BlockSpec((4, DIFF_DH), lambda h, b, i: (0, 0)),
                  pl.BlockSpec((DIFF_E, 1), lambda h, b, i: (0, 0))],
        out_specs=qspec, out_shape=jax.ShapeDtypeStruct((nb * seq, D_MODEL), F32),
        scratch_shapes=[pltpu.VMEM((nq, tq, width), BF), pltpu.VMEM((nq, width, tq), BF)], name="attn_prompt",
        compiler_params=pltpu.CompilerParams(dimension_semantics=("parallel", "parallel", "arbitrary"),
                                             vmem_limit_bytes=VMEM_LIMIT),
    )(q, k, v, bias, far, lam, subln_w[:, None])


def _attn_sample_body(pt_ref, q_ref, *refs, n_pages, lam_init):
    del pt_ref
    k_refs, v_refs = refs[:n_pages], refs[n_pages:2 * n_pages]
    kn_ref, vn_ref, bias_ref, lam_ref, sw_ref, o_ref = refs[2 * n_pages:]
    page = k_refs[0].shape[0] // DIFF_H
    steps = o_ref.shape[1]
    heads = range(DIFF_H)
    lam = _lam_of(lam_ref, lam_init)
    sw = sw_ref[...]
    zpad = jnp.zeros((page - kn_ref.shape[2], DIFF_E), BF)

    def head_rows(ref, h):
        return ref[pl.ds(h, page, stride=DIFF_H), :].astype(BF)

    def new_rows(ref, h):
        return jnp.concatenate([ref[0, h].astype(BF), zpad], axis=0)

    scores = [[_dot_nt(q_ref[0, h], head_rows(k_refs[i], h)) for i in range(n_pages)]
              + [_dot_nt(q_ref[0, h], new_rows(kn_ref, h))] for h in heads]
    probs, denom = [], []
    for h in heads:
        sc = jnp.concatenate(scores[h], axis=1) + bias_ref[h]
        pr = jnp.exp(sc - jnp.max(sc, axis=1, keepdims=True))
        denom.append(jnp.sum(pr, axis=1, keepdims=True))
        probs.append(pr.astype(BF))
    acc = []
    for h in heads:
        parts = [jnp.dot(probs[h][:, i * page:(i + 1) * page], head_rows(v_refs[i], h), preferred_element_type=F32)
                 for i in range(n_pages)]
        parts.append(jnp.dot(probs[h][:, n_pages * page:], new_rows(vn_ref, h), preferred_element_type=F32))
        acc.append(functools.reduce(jnp.add, parts))
    for h in heads:
        z = acc[h] / denom[h]
        o = z[:steps] - lam * z[steps:]
        o_ref[0, :, h * DIFF_E:(h + 1) * DIFF_E] = _rms(o, sw, SUBLN_EPS) * (1.0 - lam_init)


def _attn_sample(q, k_new, v_new, cache_k, cache_v, layer, page_table, rel_bias, lam, subln_w, lam_init):
    nb, n_pages = page_table.shape
    page = cache_k.shape[2]
    page_ids = page_table + layer * cache_k.shape[1]
    steps = q.shape[0] // nb
    past = n_pages * page
    width = DIFF_H * DIFF_E
    nrow = 2 * steps
    assert nrow % 8 == 0 and steps <= page
    q4 = jnp.transpose((q * (DIFF_DH ** -0.5)).reshape(nb, steps, DIFF_H, DIFF_E), (0, 2, 1, 3))
    first = jnp.arange(DIFF_E) < DIFF_DH
    q8 = jnp.concatenate([jnp.where(first, q4, 0.0), jnp.where(first, 0.0, q4)], axis=2).astype(BF)
    new_rows = lambda t: jnp.pad(jnp.transpose(t.reshape(nb, steps, DIFF_H, DIFF_E), (0, 2, 1, 3)),
                                 ((0, 0), (0, 0), (0, nrow - steps), (0, 0)))
    q_pos = past + jnp.arange(steps)
    key_pos = jnp.arange(past + steps)
    bias = _t5_bias(q_pos[None, :] - key_pos[:, None], rel_bias)
    bias = jnp.where((key_pos[:, None] <= q_pos[None, :])[..., None], bias, NEG)
    bias = jnp.pad(jnp.transpose(bias, (2, 1, 0)), ((0, 0), (0, 0), (0, page - steps)), constant_values=NEG)
    bias = jnp.concatenate([bias, bias], axis=1)
    rows = page * DIFF_H
    page_specs = [pl.BlockSpec((rows, DIFF_E), lambda b, pt, i=i: (pt[b, i], 0)) for i in range(n_pages)]
    head_spec = pl.BlockSpec((1, DIFF_H, nrow, DIFF_E), lambda b, pt: (b, 0, 0, 0))
    grid_spec = pltpu.PrefetchScalarGridSpec(
        num_scalar_prefetch=1, grid=(nb,),
        in_specs=[head_spec] + page_specs + page_specs + [
            head_spec, head_spec,
            pl.BlockSpec((DIFF_H, nrow, past + page), lambda b, pt: (0, 0, 0)),
            pl.BlockSpec((4, DIFF_DH), lambda b, pt: (0, 0)),
            pl.BlockSpec((1, DIFF_E), lambda b, pt: (0, 0))],
        out_specs=pl.BlockSpec((1, steps, width), lambda b, pt: (b, 0, 0)))
    ck = cache_k.reshape(-1, DIFF_E)
    cv = cache_v.reshape(-1, DIFF_E)
    o = pl.pallas_call(
        functools.partial(_attn_sample_body, n_pages=n_pages, lam_init=lam_init), grid_spec=grid_spec,
        out_shape=jax.ShapeDtypeStruct((nb, steps, width), F32), name="attn_sample",
        compiler_params=pltpu.CompilerParams(dimension_semantics=("parallel",), vmem_limit_bytes=VMEM_LIMIT),
    )(page_ids, q8, *([ck] * n_pages), *([cv] * n_pages), new_rows(k_new), new_rows(v_new), bias, lam,
      subln_w[None])
    return o.reshape(nb * steps, width)


def _attn_layer(x, u, dims, cache_k, cache_v, layer, page_table, p, lam_init, rel_bias):
    nb_p, seq_p, nb_s, seq_s = dims
    n_p = nb_p * seq_p
    q, k, v = _linear(u, p['w_qkv'], 3, "attn_qkv")
    o_p = _attn_prompt(q, k, v, nb_p, seq_p, rel_bias, p['lam'], p['subln_w'], lam_init)
    o_s = _attn_sample(q[n_p:], k[n_p:], v[n_p:], cache_k, cache_v, layer, page_table, rel_bias, p['lam'],
                       p['subln_w'], lam_init)
    x = _linear_res(x, jnp.concatenate([o_p, o_s], axis=0), p['w_o'], "attn_out")
    shp_p = (nb_p, seq_p, DIFF_H, DIFF_E)
    shp_s = (nb_s, seq_s, DIFF_H, DIFF_E)
    return x, (k[:n_p].reshape(shp_p), v[:n_p].reshape(shp_p), k[n_p:].reshape(shp_s), v[n_p:].reshape(shp_s))


def _gdn_proj_body(u_ref, wqkv_ref, wz_ref, wb_ref, wa_ref, al_ref, dt_ref, eb_ref, qkv_o, z_o, beta_o, g_o):
    u = u_ref[...].astype(BF)
    qkv_o[...] = jnp.dot(u, wqkv_ref[...], preferred_element_type=F32)
    z_o[...] = jnp.dot(u, wz_ref[...], preferred_element_type=F32)
    beta = _sigmoid(jnp.dot(u, wb_ref[...], preferred_element_type=F32))
    g = -jnp.exp(al_ref[...]) * _softplus(jnp.dot(u, wa_ref[...], preferred_element_type=F32) + dt_ref[...])
    eb = eb_ref[...]
    beta_o[...] = _dot_sel(beta, eb, 3)
    g_o[...] = _dot_sel(g, eb, 3)


def _gdn_post(conv, q_o, k_o, v_o):
    c = conv * _sigmoid(conv)
    hk = GDN_H * GDN_DK
    for h in range(GDN_H):
        qs = c[:, h * GDN_DK:(h + 1) * GDN_DK]
        ks = c[:, hk + h * GDN_DK:hk + (h + 1) * GDN_DK]
        q_o[:, h * GDN_DK:(h + 1) * GDN_DK] = qs * lax.rsqrt(jnp.sum(qs * qs, axis=1, keepdims=True) + 1e-6) * (GDN_DK ** -0.5)
        k_o[:, h * GDN_DK:(h + 1) * GDN_DK] = ks * lax.rsqrt(jnp.sum(ks * ks, axis=1, keepdims=True) + 1e-6)
    v_o[...] = c[:, 2 * hk:]


def _gdn_conv_prompt_body(x_ref, halo_ref, w_ref, q_o, k_o, v_o, *, tiles_per_seq):
    i = pl.program_id(0)
    x = x_ref[...]
    tm = x.shape[0]
    halo = jnp.where(i % tiles_per_seq == 0, 0.0, halo_ref[...])
    row8 = lax.broadcasted_iota(jnp.int32, (8, x.shape[1]), 0)
    w = w_ref[...]
    conv = w[GDN_CONV - 1:GDN_CONV] * x
    for j in range(GDN_CONV - 1):
        sh = GDN_CONV - 1 - j
        rolled = pltpu.roll(x, sh, 0)
        head = jnp.where(row8 < sh, pltpu.roll(halo, sh, 0), rolled[:8])
        tap = jnp.concatenate([head, rolled[8:]], axis=0) if tm > 8 else head
        conv = conv + w[j:j + 1] * tap
    _gdn_post(conv, q_o, k_o, v_o)


def _gdn_conv_taps_body(t0_ref, t1_ref, t2_ref, t3_ref, w_ref, q_o, k_o, v_o):
    w = w_ref[...]
    conv = w[0:1] * t0_ref[...] + w[1:2] * t1_ref[...] + w[2:3] * t2_ref[...] + w[3:4] * t3_ref[...]
    _gdn_post(conv, q_o, k_o, v_o)


def _gdn_chunk_body(q_ref, k_ref, v_ref, beta_ref, g_ref, o_ref, s_ref, st_sc, *, tb, groups):
    L, P = CHUNK, LANES
    t = pl.program_id(2)

    @pl.when(t == 0)
    def _():
        st_sc[...] = jnp.zeros(st_sc.shape, F32)

    ii, jj = _iota2((L, L))
    strict = ii > jj
    incl = ii >= jj
    ci, cj = _iota2((2 * L, L))
    cum_sel = ((cj <= ci) | (ci >= L)).astype(F32).astype(BF)

    gs = range(groups)

    def chunk(c, carry):
        rows = pl.ds(pl.multiple_of(c * L, L), L)
        cols = [slice(g * P, (g + 1) * P) for g in gs]
        st = [st_sc[g] for g in gs]
        cs = [_sel_dot(cum_sel, g_ref[rows, cols[g]]) for g in gs]
        q = [q_ref[rows, cols[g]] for g in gs]
        k = [k_ref[rows, cols[g]] for g in gs]
        kk = [_dot_nt(k[g], k[g]) for g in gs]
        qk = [_dot_nt(q[g], k[g]) for g in gs]
        n_mats, dec_i, e_gc, beta = [], [], [], []
        for g in gs:
            gc = cs[g][:L]
            diff = gc[:, :L] - gc.T[:L, :]
            beta.append(beta_ref[rows, cols[g]])
            n_mats.append(-(beta[g][:, :L] * kk[g] * jnp.exp(jnp.where(strict, diff, NEG))))
            dec_i.append(jnp.exp(jnp.where(incl, diff, NEG)))
            e_gc.append(jnp.exp(gc))
        t_inv = _tri_inv(n_mats, L)
        wv = [_dot(t_inv[g], beta[g] * v_ref[rows, cols[g]]) for g in gs]
        wk = [_dot(t_inv[g], beta[g] * e_gc[g] * k[g]) for g in gs]
        uu = [wv[g] - _dot(wk[g], st[g]) for g in gs]
        qs = [_dot(q[g] * e_gc[g], st[g]) for g in gs]
        intra = [_dot(qk[g] * dec_i[g], uu[g]) for g in gs]
        upd = [_dot_tn(k[g] * jnp.exp(cs[g][L:] - cs[g][:L]), uu[g]) for g in gs]
        for g in gs:
            o_ref[rows, cols[g]] = qs[g] + intra[g]
            st_sc[g] = jnp.exp(cs[g][L:L + 1]) * st[g] + upd[g]
        return carry

    lax.fori_loop(0, tb // L, chunk, 0)

    @pl.when(t == pl.num_programs(2) - 1)
    def _():
        for g in range(groups):
            s_ref[0, g] = st_sc[g]


def _gdn_step_body(q_ref, k_ref, v_ref, beta_ref, g_ref, s0_ref, o_ref, s_ref, *, bb, steps):
    ii, jj = _iota2((GDN_DK, GDN_DK))
    eye = (ii == jj).astype(F32)

    def one_row(b_i):
        s = s0_ref[b_i, 0]
        for t in range(steps):
            row = lambda ref: ref[b_i, pl.ds(t, 1), :]
            k_col = jnp.sum(eye * row(k_ref), axis=1, keepdims=True)
            q_col = jnp.sum(eye * row(q_ref), axis=1, keepdims=True)
            beta = row(beta_ref)
            ks = jnp.sum(k_col * s, axis=0, keepdims=True)
            s = jnp.exp(row(g_ref)) * (s - (beta * k_col) * ks) + (beta * k_col) * row(v_ref)
            o_ref[b_i, pl.ds(t, 1), :] = jnp.sum(q_col * s, axis=0, keepdims=True)
        s_ref[b_i, 0] = s

    together = math.gcd(bb, STEP_HEADS)

    def one(i, carry):
        for r in range(together):
            one_row(i * together + r)
        return carry

    lax.fori_loop(0, bb // together, one, 0)


def _gdn_out_body(x_ref, o_ref, z_ref, nw_ref, wo_ref, y_ref):
    z = z_ref[...]
    nw = nw_ref[...]
    o = o_ref[...]
    parts = []
    for h in range(GDN_H):
        sl = slice(h * GDN_DV, (h + 1) * GDN_DV)
        parts.append(_rms(o[:, sl], nw, NORM_EPS))
    on = jnp.concatenate(parts, axis=1) * (z * _sigmoid(z))
    y_ref[...] = x_ref[...] + _dot(on, wo_ref[...])


def _gdn_layer(x, u, dims, conv_s, state_s, p):
    nb_p, seq_p, nb_s, seq_s = dims
    n_p = nb_p * seq_p
    n = u.shape[0]
    hv = GDN_H * GDN_DV
    w_in = p['w_in']
    pad = lambda w: jnp.pad(w, ((0, 0), (0, LANES - w.shape[-1])))
    eb = (jnp.arange(LANES)[:, None] == jnp.arange(hv)[None, :] // GDN_DV).astype(BF)
    qkv, z, beta, g = _rows_call(
        _gdn_proj_body, [u],
        [w_in[:, :GDN_QKV].astype(BF), w_in[:, GDN_QKV:GDN_QKV + hv].astype(BF),
         pad(w_in[:, GDN_QKV + hv:GDN_QKV + hv + GDN_H]).astype(BF), pad(w_in[:, GDN_QKV + hv + GDN_H:]).astype(BF),
         pad(p['a_log'][None]), pad(p['dt_bias'][None]), eb],
        [(GDN_QKV, F32), (hv, F32), (hv, F32), (hv, F32)], _row_tile(n, 256), "gdn_proj")
    tm = _row_tile(seq_p, 256)
    qkv_p = qkv[:n_p]
    cw = p['conv_w']
    outs3 = [jax.ShapeDtypeStruct((n_p, hv), F32)] * 3
    q_p, k_p, v_p = pl.pallas_call(
        functools.partial(_gdn_conv_prompt_body, tiles_per_seq=seq_p // tm), grid=(n_p // tm,),
        in_specs=[pl.BlockSpec((tm, GDN_QKV), lambda i: (i, 0)),
                  pl.BlockSpec((8, GDN_QKV), lambda i: (jnp.maximum(i * (tm // 8) - 1, 0), 0)),
                  pl.BlockSpec((GDN_CONV, GDN_QKV), lambda i: (0, 0))],
        out_specs=[pl.BlockSpec((tm, hv), lambda i: (i, 0))] * 3, out_shape=outs3, name="gdn_conv_prompt",
        compiler_params=pltpu.CompilerParams(dimension_semantics=("parallel",), vmem_limit_bytes=VMEM_LIMIT),
    )(qkv_p, qkv_p, cw)
    qkv_s = qkv[n_p:].reshape(nb_s, seq_s, GDN_QKV)
    xp_s = jnp.concatenate([conv_s, qkv_s], axis=1)
    taps = [xp_s[:, j:j + seq_s].reshape(nb_s * seq_s, GDN_QKV) for j in range(GDN_CONV)]
    q_s, k_s, v_s = _rows_call(_gdn_conv_taps_body, taps, [cw], [(hv, F32)] * 3, _row_tile(nb_s * seq_s, 256),
                               "gdn_conv_sample")
    groups = GDN_GROUPS
    tb = _row_tile(seq_p, SEQ_BLOCK)
    nt = seq_p // tb
    spec = pl.BlockSpec((tb, groups * LANES), lambda b, h, t: (b * nt + t, h))
    o_p, s_p = pl.pallas_call(
        functools.partial(_gdn_chunk_body, tb=tb, groups=groups), grid=(nb_p, GDN_H // groups, nt),
        in_specs=[spec] * 5,
        out_specs=[spec, pl.BlockSpec((1, groups, GDN_DK, GDN_DV), lambda b, h, t: (b, h, 0, 0))],
        out_shape=[jax.ShapeDtypeStruct((n_p, hv), F32), jax.ShapeDtypeStruct((nb_p, GDN_H, GDN_DK, GDN_DV), F32)],
        scratch_shapes=[pltpu.VMEM((groups, GDN_DK, GDN_DV), F32)],
        name="gdn_chunk",
        compiler_params=pltpu.CompilerParams(dimension_semantics=("parallel", "parallel", "arbitrary"),
                                             vmem_limit_bytes=VMEM_LIMIT),
    )(q_p, k_p, v_p, beta, g)
    bb = _row_tile(nb_s, 8)
    sspec = pl.BlockSpec((bb, seq_s, LANES), lambda b, h: (b, 0, h))
    stspec = pl.BlockSpec((bb, 1, GDN_DK, GDN_DV), lambda b, h: (b, h, 0, 0))
    r3 = lambda t: t.reshape(nb_s, seq_s, hv)
    o_s, s_s = pl.pallas_call(
        functools.partial(_gdn_step_body, bb=bb, steps=seq_s), grid=(nb_s // bb, GDN_H),
        in_specs=[sspec] * 5 + [stspec], out_specs=[sspec, stspec],
        out_shape=[jax.ShapeDtypeStruct((nb_s, seq_s, hv), F32),
                   jax.ShapeDtypeStruct((nb_s, GDN_H, GDN_DK, GDN_DV), F32)],
        name="gdn_step",
        compiler_params=pltpu.CompilerParams(dimension_semantics=("parallel", "parallel"),
                                             vmem_limit_bytes=VMEM_LIMIT),
    )(r3(q_s), r3(k_s), r3(v_s), r3(beta[n_p:]), r3(g[n_p:]), state_s)
    o = jnp.concatenate([o_p, o_s.reshape(nb_s * seq_s, hv)], axis=0)
    x = _rows_call(_gdn_out_body, [x, o, z], [p['norm_w'][None], p['w_o'].astype(BF)], [(D_MODEL, F32)],
                   _row_tile(n, 512), "gdn_out")[0]
    keep = GDN_CONV - 1
    conv_p = jnp.concatenate([jnp.zeros((nb_p, keep, GDN_QKV), F32), qkv_p.reshape(nb_p, seq_p, GDN_QKV)],
                             axis=1)[:, -keep:]
    return x, (s_p, conv_p, s_s, xp_s[:, -keep:])


def kernel(x_prompt, x_sample, state_rwkv_wkv, state_rwkv_shift, cache_attn_k, cache_attn_v, state_gdn, state_gdn_conv, page_table, norm_w, final_norm_w, ffn_w_gate, ffn_w_up, ffn_w_down, rwkv_mu, rwkv_w_rkv, rwkv_w_o, rwkv_w0, rwkv_w1, rwkv_w2, rwkv_a0, rwkv_a1, rwkv_a2, rwkv_g1, rwkv_g2, rwkv_k_k, rwkv_k_a, rwkv_r_k, rwkv_lnx_w, rwkv_lnx_b, rwkv_v0, rwkv_v1, rwkv_v2, attn_w_qkv, attn_w_o, attn_lambda, attn_subln_w, rel_bias, gdn_w_in, gdn_conv_w, gdn_a_log, gdn_dt_bias, gdn_norm_w, gdn_w_o):
    nb_p, seq_p, _ = x_prompt.shape
    nb_s, seq_s, _ = x_sample.shape
    dims = (nb_p, seq_p, nb_s, seq_s)
    n_p = nb_p * seq_p
    depth = norm_w.shape[0]
    x = jnp.concatenate([x_prompt.reshape(n_p, D_MODEL), x_sample.reshape(nb_s * seq_s, D_MODEL)], axis=0)
    v_first = None
    rw, at, gd = [], [], []
    for i in range(depth):
        kind, j = i % 3, i // 3
        x, u = _ffn(x, norm_w[i, 0], ffn_w_gate[i, 0], ffn_w_up[i, 0], ffn_w_down[i, 0], norm_w[i, 1])
        if kind == 0:
            p = dict(mu=rwkv_mu[j], w_rkv=rwkv_w_rkv[j], w_o=rwkv_w_o[j], w0=rwkv_w0[j], w1=rwkv_w1[j],
                     w2=rwkv_w2[j], a0=rwkv_a0[j], a1=rwkv_a1[j], a2=rwkv_a2[j], g1=rwkv_g1[j], g2=rwkv_g2[j],
                     k_k=rwkv_k_k[j], k_a=rwkv_k_a[j], r_k=rwkv_r_k[j], lnx_w=rwkv_lnx_w[j], lnx_b=rwkv_lnx_b[j])
            vres = None if j == 0 else (rwkv_v0[j - 1], rwkv_v1[j - 1], rwkv_v2[j - 1])
            x, v_first, st = _rwkv_layer(x, u, dims, state_rwkv_shift[j], state_rwkv_wkv[j], p, v_first, vres)
            rw.append(st)
        elif kind == 1:
            p = dict(w_qkv=attn_w_qkv[j], w_o=attn_w_o[j], lam=attn_lambda[j], subln_w=attn_subln_w[j])
            lam_init = 0.8 - 0.6 * math.exp(-0.3 * i)
            x, st = _attn_layer(x, u, dims, cache_attn_k, cache_attn_v, j, page_table, p, lam_init, rel_bias)
            at.append(st)
        else:
            p = dict(w_in=gdn_w_in[j], conv_w=gdn_conv_w[j], a_log=gdn_a_log[j], dt_bias=gdn_dt_bias[j],
                     norm_w=gdn_norm_w[j], w_o=gdn_w_o[j])
            x, st = _gdn_layer(x, u, dims, state_gdn_conv[j], state_gdn[j], p)
            gd.append(st)
        g_next = final_norm_w if i == depth - 1 else jnp.ones((D_MODEL,), F32)
        x, y = _ffn(x, norm_w[i, 2], ffn_w_gate[i, 1], ffn_w_up[i, 1], ffn_w_down[i, 1], g_next)
    stack = lambda lst, k: jnp.stack([t[k] for t in lst])
    return (y[:n_p].reshape(x_prompt.shape), y[n_p:].reshape(x_sample.shape),
            stack(rw, 0), stack(rw, 1), stack(at, 0), stack(at, 1), stack(gd, 0), stack(gd, 1),
            stack(rw, 2), stack(rw, 3), stack(at, 2), stack(at, 3), stack(gd, 2), stack(gd, 3))
```

```python
import functools
import math

import jax
import jax.numpy as jnp
from jax import lax
from jax.experimental import pallas as pl
from jax.experimental.pallas import tpu as pltpu

F32 = jnp.float32
BF = jnp.bfloat16

D_MODEL = 1024
NORM_EPS = 1e-6
RWKV_N = 64
RWKV_H = D_MODEL // RWKV_N
RWKV_LNX_EPS = 64e-5
DIFF_H = 8
DIFF_DH = 64
DIFF_E = 128
SUBLN_EPS = 1e-5
REL_BUCKETS = 32
REL_MAX_EXACT = 16
REL_MAX_DIST = 128
GDN_H = 8
GDN_DK = 128
GDN_DV = 128
GDN_CONV = 4
GDN_QKV = GDN_H * (2 * GDN_DK + GDN_DV)
CHUNK = 64
SEQ_BLOCK = 512
RWKV_GROUPS = 8
GDN_GROUPS = 8
STEP_HEADS = 4
STEP_ROWS = 8
LANES = 128
NEG = -1e30
VMEM_LIMIT = 56 * 1024 * 1024


def _dot(a, b):
    return jnp.dot(a.astype(BF), b.astype(BF), preferred_element_type=F32)


def _dot_nt(a, b):
    return lax.dot_general(a.astype(BF), b.astype(BF), (((1,), (1,)), ((), ())), preferred_element_type=F32)


def _dot_tn(a, b):
    return lax.dot_general(a.astype(BF), b.astype(BF), (((0,), (0,)), ((), ())), preferred_element_type=F32)


def _split(x, n):
    out = []
    for _ in range(n):
        h = x.astype(BF)
        out.append(h)
        x = x - h.astype(F32)
    return out


def _sel_dot(sel, x, n=3):
    return sum(jnp.dot(sel, t, preferred_element_type=F32) for t in _split(x, n))


def _dot_sel(x, sel, n=2):
    return sum(jnp.dot(t, sel, preferred_element_type=F32) for t in _split(x, n))


def _rms(x, g, eps):
    return x * lax.rsqrt(jnp.mean(x * x, axis=-1, keepdims=True) + eps) * g


def _sigmoid(x):
    return 1.0 / (1.0 + jnp.exp(-x))


def _softplus(x):
    return jnp.maximum(x, 0.0) + jnp.log(1.0 + jnp.exp(-jnp.abs(x)))


def _iota2(shape):
    return lax.broadcasted_iota(jnp.int32, shape, 0), lax.broadcasted_iota(jnp.int32, shape, 1)


def _tri_inv(n_mats, size):
    ii, jj = _iota2((size, size))
    eye = (ii == jj).astype(F32)
    xs = [eye + jnp.where((ii >> 1) == (jj >> 1), n, 0.0) for n in n_mats]
    lvl = 1
    while (2 << lvl) <= size:
        sel = ((ii >> (lvl + 1)) == (jj >> (lvl + 1))) & (((ii >> lvl) & 1) == 1) & (((jj >> lvl) & 1) == 0)
        ys = [_dot(x, jnp.where(sel, n, 0.0)) for x, n in zip(xs, n_mats)]
        xs = [x + _dot(y, x) for x, y in zip(xs, ys)]
        lvl += 1
    return xs


def _row_tile(n, want):
    tm = min(want, n)
    while n % tm:
        tm //= 2
    return tm


HALO = 8
SAMPLE_TILE = 128


def _rows_call(body, rows, consts, outs, tm, name, halo=()):
    n_p, n_s = rows[0][0].shape[0], rows[0][1].shape[0]
    tm_p, tm_s = _row_tile(n_p, tm), _row_tile(n_s, min(tm, SAMPLE_TILE))
    tp, ts = n_p // tm_p, n_s // tm_s
    p_idx = lambda i: jnp.minimum(i, tp - 1)
    s_idx = lambda i: jnp.maximum(i - tp, 0)
    in_specs, args = [], []
    for k, (a_p, a_s) in enumerate(rows):
        if k in halo:
            in_specs.append(pl.BlockSpec((HALO, a_p.shape[1]),
                                         lambda i: (jnp.maximum(p_idx(i) * (tm_p // HALO) - 1, 0), 0)))
        else:
            in_specs.append(pl.BlockSpec((tm_p, a_p.shape[1]), lambda i: (p_idx(i), 0)))
        in_specs.append(pl.BlockSpec((tm_s, a_s.shape[1]), lambda i: (s_idx(i), 0)))
        args += [a_p, a_s]
    for a in consts:
        in_specs.append(pl.BlockSpec(a.shape, lambda i, nd=a.ndim: (0,) * nd, pipeline_mode=pl.Buffered(1)))
    out_specs, out_shape = [], []
    for c, dt in outs:
        out_specs += [pl.BlockSpec((tm_p, c), lambda i: (p_idx(i), 0)), pl.BlockSpec((tm_s, c), lambda i: (s_idx(i), 0))]
        out_shape += [jax.ShapeDtypeStruct((n_p, c), dt), jax.ShapeDtypeStruct((n_s, c), dt)]
    nr, nc = len(rows), len(consts)

    def kern(*refs):
        i = pl.program_id(0)
        cs = refs[2 * nr:2 * nr + nc]
        o = refs[2 * nr + nc:]

        @pl.when(i < tp)
        def _():
            body(True, i, *refs[0:2 * nr:2], *cs, *o[0::2])

        @pl.when(i >= tp)
        def _():
            body(False, i - tp, *refs[1:2 * nr:2], *cs, *o[1::2])

    res = pl.pallas_call(
        kern, grid=(tp + ts,), in_specs=in_specs, out_specs=out_specs, out_shape=out_shape, name=name,
        compiler_params=pltpu.CompilerParams(dimension_semantics=("arbitrary",), vmem_limit_bytes=VMEM_LIMIT),
    )(*args, *consts)
    return [(res[2 * k], res[2 * k + 1]) for k in range(len(outs))]


def _ffn_body(is_prompt, tile, x_ref, g_ref, wg_ref, wu_ref, wd_ref, g2_ref, o_ref, u_ref, *, nchunk):
    del is_prompt, tile
    x = x_ref[...]
    h = _rms(x, g_ref[...], NORM_EPS).astype(BF)
    fc = wg_ref.shape[1] // nchunk
    acc = jnp.zeros_like(x)
    for c in range(nchunk):
        sl = slice(c * fc, (c + 1) * fc)
        gate = jnp.dot(h, wg_ref[:, sl], preferred_element_type=F32)
        up = jnp.dot(h, wu_ref[:, sl], preferred_element_type=F32)
        act = (gate * _sigmoid(gate) * up).astype(BF)
        acc = acc + jnp.dot(act, wd_ref[sl, :], preferred_element_type=F32)
    y = x + 0.5 * acc
    o_ref[...] = y
    u_ref[...] = _rms(y, g2_ref[...], NORM_EPS)


def _ffn(x, g, wg, wu, wd, g_next):
    nchunk = 2 if wg.shape[1] % 256 == 0 else 1
    body = functools.partial(_ffn_body, nchunk=nchunk)
    return _rows_call(body, [x], [g[None], wg.astype(BF), wu.astype(BF), wd.astype(BF), g_next[None]],
                      [(D_MODEL, F32), (D_MODEL, F32)], 512, "ffn_half")


def _linear_body(is_prompt, tile, *refs, n_out, residual):
    del is_prompt, tile
    if residual:
        x_ref, u_ref, w_ref = refs[:3]
        outs = refs[3:]
    else:
        u_ref, w_ref = refs[:2]
        outs = refs[2:]
    y = jnp.dot(u_ref[...].astype(BF), w_ref[...], preferred_element_type=F32)
    if residual:
        outs[0][...] = x_ref[...] + y
    else:
        c = y.shape[1] // n_out
        for k, o in enumerate(outs):
            o[...] = y[:, k * c:(k + 1) * c]


def _linear(u, w, n_out, name):
    body = functools.partial(_linear_body, n_out=n_out, residual=False)
    c = w.shape[1] // n_out
    return _rows_call(body, [u], [w.astype(BF)], [(c, F32)] * n_out, 512, name)


def _linear_res(x, u, w, name):
    body = functools.partial(_linear_body, n_out=1, residual=True)
    return _rows_call(body, [x, u], [w.astype(BF)], [(w.shape[1], F32)], 512, name)[0]


def _head_sel(width, heads):
    e = (jnp.arange(width)[:, None] // (width // heads) == jnp.arange(heads)[None, :]).astype(BF)
    return e, e.T


def _rwkv_proj_body(is_prompt, tile, *refs, has_vres, tiles_per_seq):
    (u_ref, prev_ref) = refs[:2]
    k0 = 2
    if has_vres:
        vf_ref = refs[2]
        k0 = 3
    (mu_ref, wr_ref, wk_ref, wv_ref, w0_ref, w1_ref, w2_ref, a0_ref, a1_ref, a2_ref, g1_ref, g2_ref,
     kk_ref, ka_ref, rk_ref, e_ref, et_ref) = refs[k0:k0 + 17]
    k1 = k0 + 17
    if has_vres:
        v0_ref, v1_ref, v2_ref = refs[k1:k1 + 3]
        k1 += 3
    r_o, lw_o, k_o, v_o, a_o, b_o, g_o, bon_o = refs[k1:]
    u = u_ref[...]
    if is_prompt:
        before = jnp.where(tile % tiles_per_seq == 0, 0.0, prev_ref[HALO - 1:HALO, :])
        row = lax.broadcasted_iota(jnp.int32, u.shape, 0)
        prev = jnp.where(row == 0, before, pltpu.roll(u, 1, 0))
    else:
        prev = prev_ref[...]
    xx = prev - u
    mu = mu_ref[...]
    xr, xw, xk = u + xx * mu[0:1], u + xx * mu[1:2], u + xx * mu[2:3]
    xv, xa, xg = u + xx * mu[3:4], u + xx * mu[4:5], u + xx * mu[5:6]
    r = _dot(xr, wr_ref[...])
    k = _dot(xk, wk_ref[...])
    v = _dot(xv, wv_ref[...])
    wlog = -_softplus(-(w0_ref[...] + _dot(jnp.tanh(_dot(xw, w1_ref[...])), w2_ref[...]))) - 0.5
    lw_o[...] = -jnp.exp(wlog)
    if has_vres:
        v = v + (vf_ref[...] - v) * _sigmoid(v0_ref[...] + _dot(_dot(xv, v1_ref[...]), v2_ref[...]))
    a = _sigmoid(a0_ref[...] + _dot(_dot(xa, a1_ref[...]), a2_ref[...]))
    g_o[...] = _dot(_sigmoid(_dot(xg, g1_ref[...])), g2_ref[...])
    e, et = e_ref[...], et_ref[...]
    kk = k * kk_ref[...]
    kk = kk * lax.rsqrt(_dot_sel(_dot_sel(kk * kk, e), et) + 1e-24)
    k_mod = k * (1.0 + (a - 1.0) * ka_ref[...])
    r_o[...] = r
    k_o[...] = k_mod
    v_o[...] = v
    a_o[...] = -kk
    b_o[...] = kk * a
    bon_o[...] = _dot_sel(_dot_sel(r * k_mod * rk_ref[...], e), et) * v


def _rwkv_chunk_body(r_ref, lw_ref, k_ref, v_ref, a_ref, b_ref, o_ref, s_ref, st_sc, *, tb, groups):
    L, P, N = CHUNK, LANES, RWKV_N
    t = pl.program_id(2)

    @pl.when(t == 0)
    def _():
        st_sc[...] = jnp.zeros(st_sc.shape, F32)

    ii, jj = _iota2((L, L))
    strict = (ii > jj).astype(F32)
    incl = (ii >= jj).astype(F32)
    ci, cj = _iota2((2 * L, L))
    cum_sel = ((cj <= ci) | (ci >= L)).astype(F32).astype(BF)
    lane = lax.broadcasted_iota(jnp.int32, (L, P), 1)
    lane2 = lax.broadcasted_iota(jnp.int32, (2 * L, P), 1)
    bi, bj = _iota2((P, P))
    bd = ((bi >= N) == (bj >= N)).astype(F32)

    gs = range(groups)
    chains = [(g, h) for g in gs for h in range(2)]
    head_lanes = [(lane >= N) == bool(h) for h in range(2)]
    head_lanes2 = [(lane2 >= N) == bool(h) for h in range(2)]

    def chunk(c, carry):
        rows = pl.ds(pl.multiple_of(c * L, L), L)
        cols = [slice(g * P, (g + 1) * P) for g in gs]
        st = [st_sc[g] for g in gs]
        lw = [lw_ref[rows, cols[g]] for g in gs]
        cs = [_sel_dot(cum_sel, lw[g]) for g in gs]
        x, bt, kt, v, b_end, k_end, w_end = [], [], [], [], [], [], []
        for g in gs:
            gc, gl = cs[g][:L], cs[g][L:]
            r, k, a, b = r_ref[rows, cols[g]], k_ref[rows, cols[g]], a_ref[rows, cols[g]], b_ref[rows, cols[g]]
            e_neg = jnp.exp(-gc)
            e_end = jnp.exp(gl - gc)
            x.append(jnp.concatenate([a * jnp.exp(gc - lw[g]), r * jnp.exp(gc)], axis=0).astype(BF))
            bt.append((b * e_neg).astype(BF))
            kt.append((k * e_neg).astype(BF))
            v.append(v_ref[rows, cols[g]])
            b_end.append(b * e_end)
            k_end.append(k * e_end)
            w_end.append(jnp.exp(gl[0:1]))
        xs = [_dot_nt(x[g], st[g]) for g in gs]
        xh = {(g, h): jnp.where(head_lanes2[h], x[g], jnp.zeros_like(x[g])) for g, h in chains}
        ab = {ch: _dot_nt(xh[ch], bt[ch[0]]) for ch in chains}
        ak = {ch: _dot_nt(xh[ch], kt[ch[0]]) for ch in chains}
        t_inv = dict(zip(chains, _tri_inv([ab[ch][:L] * strict for ch in chains], L)))
        vh = {(g, h): jnp.where(head_lanes[h], v[g], 0.0) for g, h in chains}
        rhs_h = {ch: _dot(ak[ch][:L] * strict, vh[ch]) for ch in chains}
        out_h = {ch: _dot(ak[ch][L:] * incl, vh[ch]) for ch in chains}
        rhs = [xs[g][:L] + rhs_h[(g, 0)] + rhs_h[(g, 1)] for g in gs]
        uu_h = {(g, h): _dot(t_inv[(g, h)], jnp.where(head_lanes[h], rhs[g], 0.0)) for g, h in chains}
        uu = [uu_h[(g, 0)] + uu_h[(g, 1)] for g in gs]
        o_h = {(g, h): _dot(ab[(g, h)][L:] * incl, jnp.where(head_lanes[h], uu[g], 0.0)) for g, h in chains}
        upd = [_dot_tn(uu[g], b_end[g]) + _dot_tn(v[g], k_end[g]) for g in gs]
        for g in gs:
            o_ref[rows, cols[g]] = xs[g][L:] + out_h[(g, 0)] + out_h[(g, 1)] + o_h[(g, 0)] + o_h[(g, 1)]
            st_sc[g] = st[g] * w_end[g] + bd * upd[g]
        return carry

    lax.fori_loop(0, tb // L, chunk, 0)

    @pl.when(t == pl.num_programs(2) - 1)
    def _():
        for g in range(groups):
            s = st_sc[g]
            s_ref[0, 2 * g] = s[:N, :N]
            s_ref[0, 2 * g + 1] = s[N:, N:]


def _rwkv_chunk(seqs, nb, seq):
    groups = RWKV_GROUPS
    width = groups * LANES
    tb = _row_tile(seq, SEQ_BLOCK)
    nt = seq // tb
    spec = pl.BlockSpec((tb, width), lambda b, h, t: (b * nt + t, h))
    return pl.pallas_call(
        functools.partial(_rwkv_chunk_body, tb=tb, groups=groups), grid=(nb, D_MODEL // width, nt),
        in_specs=[spec] * 6,
        out_specs=[spec, pl.BlockSpec((1, 2 * groups, RWKV_N, RWKV_N), lambda b, h, t: (b, h, 0, 0))],
        out_shape=[jax.ShapeDtypeStruct((nb * seq, D_MODEL), F32),
                   jax.ShapeDtypeStruct((nb, RWKV_H, RWKV_N, RWKV_N), F32)],
        scratch_shapes=[pltpu.VMEM((groups, LANES, LANES), F32)],
        name="rwkv_chunk",
        compiler_params=pltpu.CompilerParams(dimension_semantics=("parallel", "parallel", "arbitrary"),
                                             vmem_limit_bytes=VMEM_LIMIT),
    )(*seqs)


def _rwkv_step_body(r_ref, lw_ref, k_ref, v_ref, a_ref, b_ref, s0_ref, o_ref, s_ref, *, bb, steps):
    N = RWKV_N
    ii, jj = _iota2((N, N))
    eye = (ii == jj).astype(F32)

    def head(b_i, h):
        s = s0_ref[b_i, h]
        for t in range(steps):
            row = lambda ref: ref[b_i, t, pl.ds(h, 1), :]
            sa = jnp.sum(s * row(a_ref), axis=1, keepdims=True)
            v_col = jnp.sum(eye * row(v_ref), axis=1, keepdims=True)
            s = s * jnp.exp(row(lw_ref)) + sa * row(b_ref) + v_col * row(k_ref)
            o_col = jnp.sum(s * row(r_ref), axis=1, keepdims=True)
            o_ref[b_i, t, pl.ds(h, 1), :] = jnp.sum(eye * o_col, axis=0, keepdims=True)
        s_ref[b_i, h] = s

    per_row = RWKV_H // STEP_HEADS

    def one(i, carry):
        b_i = i // per_row
        h0 = (i % per_row) * STEP_HEADS
        for hh in range(STEP_HEADS):
            head(b_i, h0 + hh)
        return carry

    lax.fori_loop(0, bb * per_row, one, 0)


def _rwkv_step(seqs, s0, nb, steps):
    bb = _row_tile(nb, 8)
    spec = pl.BlockSpec((bb, steps, RWKV_H, RWKV_N), lambda b: (b, 0, 0, 0))
    sspec = pl.BlockSpec((bb, RWKV_H, RWKV_N, RWKV_N), lambda b: (b, 0, 0, 0))
    seqs = [t.reshape(nb, steps, RWKV_H, RWKV_N) for t in seqs]
    o, s = pl.pallas_call(
        functools.partial(_rwkv_step_body, bb=bb, steps=steps), grid=(nb // bb,),
        in_specs=[spec] * 6 + [sspec], out_specs=[spec, sspec],
        out_shape=[jax.ShapeDtypeStruct((nb, steps, RWKV_H, RWKV_N), F32),
                   jax.ShapeDtypeStruct((nb, RWKV_H, RWKV_N, RWKV_N), F32)],
        name="rwkv_step",
        compiler_params=pltpu.CompilerParams(dimension_semantics=("parallel",), vmem_limit_bytes=VMEM_LIMIT),
    )(*seqs, s0)
    return o.reshape(nb * steps, D_MODEL), s


def _rwkv_out_body(is_prompt, tile, x_ref, o_ref, g_ref, bon_ref, lw_ref, lb_ref, e_ref, et_ref, wo_ref, y_ref):
    del is_prompt, tile
    e, et = e_ref[...], et_ref[...]
    o = o_ref[...]
    inv_n = 1.0 / RWKV_N
    d = o - _dot_sel(_dot_sel(o, e), et) * inv_n
    var = _dot_sel(_dot_sel(d * d, e), et) * inv_n
    on = d * lax.rsqrt(var + RWKV_LNX_EPS) * lw_ref[...] + lb_ref[...]
    y_ref[...] = x_ref[...] + _dot((on + bon_ref[...]) * g_ref[...], wo_ref[...])


def _rwkv_layer(x, u, dims, shift_s, wkv_s, p, v_first, vres):
    nb_p, seq_p, nb_s, seq_s = dims
    u_p, u_s = u
    u3s = u_s.reshape(nb_s, seq_s, D_MODEL)
    prev_s = jnp.concatenate([shift_s[:, None, :], u3s[:, :-1]], axis=1).reshape(nb_s * seq_s, D_MODEL)
    e, et = _head_sel(D_MODEL, RWKV_H)
    has_vres = vres is not None
    tm = _row_tile(seq_p, 256)
    rows = [u, (u_p, prev_s)] + ([v_first] if has_vres else [])
    consts = [p['mu'], p['w_rkv'][0].astype(BF), p['w_rkv'][1].astype(BF), p['w_rkv'][2].astype(BF),
              p['w0'][None], p['w1'].astype(BF), p['w2'].astype(BF), p['a0'][None], p['a1'].astype(BF),
              p['a2'].astype(BF), p['g1'].astype(BF), p['g2'].astype(BF), p['k_k'][None], p['k_a'][None],
              p['r_k'].reshape(1, D_MODEL), e, et]
    if has_vres:
        consts += [vres[0][None], vres[1].astype(BF), vres[2].astype(BF)]
    body = functools.partial(_rwkv_proj_body, has_vres=has_vres, tiles_per_seq=seq_p // tm)
    r, lw, k, v, a, b, gate, bonus = _rows_call(body, rows, consts, [(D_MODEL, F32)] * 8, tm, "rwkv_proj", halo=(1,))
    seqs = (r, lw, k, v, a, b)
    o_p, s_p = _rwkv_chunk([t[0] for t in seqs], nb_p, seq_p)
    o_s, s_s = _rwkv_step([t[1] for t in seqs], wkv_s, nb_s, seq_s)
    x = _rows_call(_rwkv_out_body, [x, (o_p, o_s), gate, bonus],
                   [p['lnx_w'][None], p['lnx_b'][None], e, et, p['w_o'].astype(BF)], [(D_MODEL, F32)], 512,
                   "rwkv_out")[0]
    u3p = u_p.reshape(nb_p, seq_p, D_MODEL)
    return x, (v if not has_vres else v_first), (s_p, u3p[:, -1], s_s, u3s[:, -1])


def _t5_bias(dist, rel_bias):
    n = jnp.maximum(dist, 0)
    nf = jnp.maximum(n, 1).astype(F32)
    large = REL_MAX_EXACT + (jnp.log(nf / REL_MAX_EXACT) / math.log(REL_MAX_DIST / REL_MAX_EXACT)
                             * (REL_BUCKETS - REL_MAX_EXACT)).astype(jnp.int32)
    large = jnp.minimum(large, REL_BUCKETS - 1)
    bucket = jnp.where(n < REL_MAX_EXACT, n, large)
    onehot = (bucket[..., None] == jnp.arange(REL_BUCKETS)).astype(F32)
    return jnp.einsum('...b,bh->...h', onehot, rel_bias.astype(F32), precision=lax.Precision.HIGHEST)


def _lam_of(lam_ref, lam_init):
    lv = lam_ref[...]
    return (jnp.exp(jnp.sum(lv[0:1] * lv[1:2], axis=1, keepdims=True))
            - jnp.exp(jnp.sum(lv[2:3] * lv[3:4], axis=1, keepdims=True)) + lam_init)


ATTN_HEADS = 2


def _attn_prompt_body(q_ref, k_ref, v_ref, bias_ref, far_ref, lam_ref, sw_ref, o_ref, kb_sc, vt_sc, *, tq, lam_init):
    qi = pl.program_id(2)
    E, dh = DIFF_E, DIFF_DH
    nkv = kb_sc.shape[0]
    chains = [(hh, c) for hh in range(ATTN_HEADS) for c in range(2)]

    @pl.when(qi == 0)
    def _():
        for j in range(nkv):
            kb_sc[j] = k_ref[j * tq:(j + 1) * tq, :].astype(BF)
            vt_sc[j] = v_ref[j * tq:(j + 1) * tq, :].T.astype(BF)

    q_t = (q_ref[...] * (dh ** -0.5)).T
    row = lax.broadcasted_iota(jnp.int32, q_t.shape, 0)
    q_c = {(hh, c): jnp.where((row >= hh * E + c * dh) & (row < hh * E + (c + 1) * dh), q_t, 0.0).astype(BF)
           for hh, c in chains}
    kk, qq = _iota2((tq, tq))

    def step(carry, kj, bias):
        kb = kb_sc[kj]
        s = {ch: jnp.dot(kb, q_c[ch], preferred_element_type=F32) + bias[ch[0]] for ch in chains}
        m_new = {ch: jnp.maximum(carry[ch][0], jnp.max(s[ch], axis=0, keepdims=True)) for ch in chains}
        alpha = {ch: jnp.exp(carry[ch][0] - m_new[ch]) for ch in chains}
        pr = {ch: jnp.exp(s[ch] - m_new[ch]) for ch in chains}
        l_new = {ch: carry[ch][1] * alpha[ch] + jnp.sum(pr[ch], axis=0, keepdims=True) for ch in chains}
        vt = vt_sc[kj]
        pv = {(hh, c): jnp.dot(vt[hh * E:(hh + 1) * E, :], pr[(hh, c)].astype(BF), preferred_element_type=F32)
              for hh, c in chains}
        return {ch: (m_new[ch], l_new[ch], carry[ch][2] * alpha[ch] + pv[ch]) for ch in chains}

    init = {ch: (jnp.full((1, tq), NEG, F32), jnp.zeros((1, tq), F32), jnp.zeros((E, tq), F32)) for ch in chains}
    carry = step(init, qi, [jnp.where(kk <= qq, bias_ref[hh, 0], NEG) for hh in range(ATTN_HEADS)])
    carry = step(carry, jnp.maximum(qi - 1, 0),
                 [jnp.where(qi >= 1, bias_ref[hh, 1], NEG) for hh in range(ATTN_HEADS)])
    far = [far_ref[hh, 0:1, :] for hh in range(ATTN_HEADS)]
    carry = lax.fori_loop(0, jnp.maximum(qi - 1, 0), lambda kj, c: step(c, kj, far), carry)
    lam = _lam_of(lam_ref, lam_init)
    for hh in range(ATTN_HEADS):
        (_, l1, acc1), (_, l2, acc2) = carry[(hh, 0)], carry[(hh, 1)]
        o_t = acc1 / l1 - lam * (acc2 / l2)
        o_t = o_t * lax.rsqrt(jnp.mean(o_t * o_t, axis=0, keepdims=True) + SUBLN_EPS) * sw_ref[...]
        o_ref[:, hh * E:(hh + 1) * E] = o_t.T * (1.0 - lam_init)


def _attn_prompt(q, k, v, nb, seq, rel_bias, lam, subln_w, lam_init):
    tq = _row_tile(seq, 256)
    assert tq >= REL_MAX_DIST or tq == seq
    nq = seq // tq
    width = ATTN_HEADS * DIFF_E
    d0 = jnp.arange(tq)[None, :] - jnp.arange(tq)[:, None]
    bias = jnp.stack([_t5_bias(d0, rel_bias), _t5_bias(d0 + tq, rel_bias)])
    bias = jnp.transpose(bias, (3, 0, 1, 2))
    far = jnp.broadcast_to(rel_bias[REL_BUCKETS - 1][:, None, None], (DIFF_H, 8, tq)).astype(F32)
    qspec = pl.BlockSpec((tq, width), lambda h, b, i: (b * nq + i, h))
    kspec = pl.BlockSpec((seq, width), lambda h, b, i: (b, h))
    return pl.pallas_call(
        functools.partial(_attn_prompt_body, tq=tq, lam_init=lam_init), grid=(DIFF_H // ATTN_HEADS, nb, nq),
        in_specs=[qspec, kspec, kspec,
                  pl.BlockSpec((ATTN_HEADS, 2, tq, tq), lambda h, b, i: (h, 0, 0, 0)),
                  pl.BlockSpec((ATTN_HEADS, 8, tq), lambda h, b, i: (h, 0, 0)),
                  pl.BlockSpec((4, DIFF_DH), lambda h, b, i: (0, 0)),
                  pl.BlockSpec((DIFF_E, 1), lambda h, b, i: (0, 0))],
        out_specs=qspec, out_shape=jax.ShapeDtypeStruct((nb * seq, D_MODEL), F32),
        scratch_shapes=[pltpu.VMEM((nq, tq, width), BF), pltpu.VMEM((nq, width, tq), BF)], name="attn_prompt",
        compiler_params=pltpu.CompilerParams(dimension_semantics=("parallel", "parallel", "arbitrary"),
                                             vmem_limit_bytes=VMEM_LIMIT),
    )(q, k, v, bias, far, lam, subln_w[:, None])


def _attn_sample_body(pt_ref, q_ref, *refs, n_pages, lam_init):
    del pt_ref
    k_refs, v_refs = refs[:n_pages], refs[n_pages:2 * n_pages]
    kn_ref, vn_ref, bias_ref, lam_ref, sw_ref, o_ref = refs[2 * n_pages:]
    page = k_refs[0].shape[0] // DIFF_H
    steps = o_ref.shape[1]
    heads = range(DIFF_H)
    lam = _lam_of(lam_ref, lam_init)
    sw = sw_ref[...]
    zpad = jnp.zeros((page - kn_ref.shape[2], DIFF_E), BF)

    def head_rows(ref, h):
        return ref[pl.ds(h, page, stride=DIFF_H), :].astype(BF)

    def new_rows(ref, h):
        return jnp.concatenate([ref[0, h].astype(BF), zpad], axis=0)

    scores = [[_dot_nt(q_ref[0, h], head_rows(k_refs[i], h)) for i in range(n_pages)]
              + [_dot_nt(q_ref[0, h], new_rows(kn_ref, h))] for h in heads]
    probs, denom = [], []
    for h in heads:
        sc = jnp.concatenate(scores[h], axis=1) + bias_ref[h]
        pr = jnp.exp(sc - jnp.max(sc, axis=1, keepdims=True))
        denom.append(jnp.sum(pr, axis=1, keepdims=True))
        probs.append(pr.astype(BF))
    acc = []
    for h in heads:
        parts = [jnp.dot(probs[h][:, i * page:(i + 1) * page], head_rows(v_refs[i], h), preferred_element_type=F32)
                 for i in range(n_pages)]
        parts.append(jnp.dot(probs[h][:, n_pages * page:], new_rows(vn_ref, h), preferred_element_type=F32))
        acc.append(functools.reduce(jnp.add, parts))
    for h in heads:
        z = acc[h] / denom[h]
        o = z[:steps] - lam * z[steps:]
        o_ref[0, :, h * DIFF_E:(h + 1) * DIFF_E] = _rms(o, sw, SUBLN_EPS) * (1.0 - lam_init)


def _attn_sample(q, k_new, v_new, cache_k, cache_v, layer, page_table, rel_bias, lam, subln_w, lam_init):
    nb, n_pages = page_table.shape
    page = cache_k.shape[2]
    page_ids = page_table + layer * cache_k.shape[1]
    steps = q.shape[0] // nb
    past = n_pages * page
    width = DIFF_H * DIFF_E
    nrow = 2 * steps
    assert nrow % 8 == 0 and steps <= page
    q4 = jnp.transpose((q * (DIFF_DH ** -0.5)).reshape(nb, steps, DIFF_H, DIFF_E), (0, 2, 1, 3))
    first = jnp.arange(DIFF_E) < DIFF_DH
    q8 = jnp.concatenate([jnp.where(first, q4, 0.0), jnp.where(first, 0.0, q4)], axis=2).astype(BF)
    new_rows = lambda t: jnp.pad(jnp.transpose(t.reshape(nb, steps, DIFF_H, DIFF_E), (0, 2, 1, 3)),
                                 ((0, 0), (0, 0), (0, nrow - steps), (0, 0)))
    q_pos = past + jnp.arange(steps)
    key_pos = jnp.arange(past + steps)
    bias = _t5_bias(q_pos[None, :] - key_pos[:, None], rel_bias)
    bias = jnp.where((key_pos[:, None] <= q_pos[None, :])[..., None], bias, NEG)
    bias = jnp.pad(jnp.transpose(bias, (2, 1, 0)), ((0, 0), (0, 0), (0, page - steps)), constant_values=NEG)
    bias = jnp.concatenate([bias, bias], axis=1)
    rows = page * DIFF_H
    page_specs = [pl.BlockSpec((rows, DIFF_E), lambda b, pt, i=i: (pt[b, i], 0)) for i in range(n_pages)]
    head_spec = pl.BlockSpec((1, DIFF_H, nrow, DIFF_E), lambda b, pt: (b, 0, 0, 0))
    grid_spec = pltpu.PrefetchScalarGridSpec(
        num_scalar_prefetch=1, grid=(nb,),
        in_specs=[head_spec] + page_specs + page_specs + [
            head_spec, head_spec,
            pl.BlockSpec((DIFF_H, nrow, past + page), lambda b, pt: (0, 0, 0)),
            pl.BlockSpec((4, DIFF_DH), lambda b, pt: (0, 0)),
            pl.BlockSpec((1, DIFF_E), lambda b, pt: (0, 0))],
        out_specs=pl.BlockSpec((1, steps, width), lambda b, pt: (b, 0, 0)))
    ck = cache_k.reshape(-1, DIFF_E)
    cv = cache_v.reshape(-1, DIFF_E)
    o = pl.pallas_call(
        functools.partial(_attn_sample_body, n_pages=n_pages, lam_init=lam_init), grid_spec=grid_spec,
        out_shape=jax.ShapeDtypeStruct((nb, steps, width), F32), name="attn_sample",
        compiler_params=pltpu.CompilerParams(dimension_semantics=("parallel",), vmem_limit_bytes=VMEM_LIMIT),
    )(page_ids, q8, *([ck] * n_pages), *([cv] * n_pages), new_rows(k_new), new_rows(v_new), bias, lam,
      subln_w[None])
    return o.reshape(nb * steps, width)


def _attn_layer(x, u, dims, cache_k, cache_v, layer, page_table, p, lam_init, rel_bias):
    nb_p, seq_p, nb_s, seq_s = dims
    (q_p, q_s), (k_p, k_s), (v_p, v_s) = _linear(u, p['w_qkv'], 3, "attn_qkv")
    o_p = _attn_prompt(q_p, k_p, v_p, nb_p, seq_p, rel_bias, p['lam'], p['subln_w'], lam_init)
    o_s = _attn_sample(q_s, k_s, v_s, cache_k, cache_v, layer, page_table, rel_bias, p['lam'], p['subln_w'],
                       lam_init)
    x = _linear_res(x, (o_p, o_s), p['w_o'], "attn_out")
    shp_p = (nb_p, seq_p, DIFF_H, DIFF_E)
    shp_s = (nb_s, seq_s, DIFF_H, DIFF_E)
    return x, (k_p.reshape(shp_p), v_p.reshape(shp_p), k_s.reshape(shp_s), v_s.reshape(shp_s))


def _gdn_proj_body(is_prompt, tile, u_ref, wqkv_ref, wz_ref, wb_ref, wa_ref, al_ref, dt_ref, eb_ref,
                   qkv_o, z_o, beta_o, g_o):
    del is_prompt, tile
    u = u_ref[...].astype(BF)
    qkv_o[...] = jnp.dot(u, wqkv_ref[...], preferred_element_type=F32)
    z_o[...] = jnp.dot(u, wz_ref[...], preferred_element_type=F32)
    beta = _sigmoid(jnp.dot(u, wb_ref[...], preferred_element_type=F32))
    g = -jnp.exp(al_ref[...]) * _softplus(jnp.dot(u, wa_ref[...], preferred_element_type=F32) + dt_ref[...])
    eb = eb_ref[...]
    beta_o[...] = _dot_sel(beta, eb, 3)
    g_o[...] = _dot_sel(g, eb, 3)


def _gdn_post(conv, q_o, k_o, v_o):
    c = conv * _sigmoid(conv)
    hk = GDN_H * GDN_DK
    for h in range(GDN_H):
        qs = c[:, h * GDN_DK:(h + 1) * GDN_DK]
        ks = c[:, hk + h * GDN_DK:hk + (h + 1) * GDN_DK]
        q_o[:, h * GDN_DK:(h + 1) * GDN_DK] = qs * lax.rsqrt(jnp.sum(qs * qs, axis=1, keepdims=True) + 1e-6) * (GDN_DK ** -0.5)
        k_o[:, h * GDN_DK:(h + 1) * GDN_DK] = ks * lax.rsqrt(jnp.sum(ks * ks, axis=1, keepdims=True) + 1e-6)
    v_o[...] = c[:, 2 * hk:]


def _gdn_conv_prompt_body(x_ref, halo_ref, w_ref, q_o, k_o, v_o, *, tiles_per_seq):
    i = pl.program_id(0)
    x = x_ref[...]
    tm = x.shape[0]
    halo = jnp.where(i % tiles_per_seq == 0, 0.0, halo_ref[...])
    row8 = lax.broadcasted_iota(jnp.int32, (8, x.shape[1]), 0)
    w = w_ref[...]
    conv = w[GDN_CONV - 1:GDN_CONV] * x
    for j in range(GDN_CONV - 1):
        sh = GDN_CONV - 1 - j
        rolled = pltpu.roll(x, sh, 0)
        head = jnp.where(row8 < sh, pltpu.roll(halo, sh, 0), rolled[:8])
        tap = jnp.concatenate([head, rolled[8:]], axis=0) if tm > 8 else head
        conv = conv + w[j:j + 1] * tap
    _gdn_post(conv, q_o, k_o, v_o)


def _gdn_conv_taps_body(t0_ref, t1_ref, t2_ref, t3_ref, w_ref, q_o, k_o, v_o):
    w = w_ref[...]
    conv = w[0:1] * t0_ref[...] + w[1:2] * t1_ref[...] + w[2:3] * t2_ref[...] + w[3:4] * t3_ref[...]
    _gdn_post(conv, q_o, k_o, v_o)


def _gdn_chunk_body(q_ref, k_ref, v_ref, beta_ref, g_ref, o_ref, s_ref, st_sc, *, tb, groups):
    L, P = CHUNK, LANES
    t = pl.program_id(2)

    @pl.when(t == 0)
    def _():
        st_sc[...] = jnp.zeros(st_sc.shape, F32)

    ii, jj = _iota2((L, L))
    strict = ii > jj
    incl = ii >= jj
    ci, cj = _iota2((2 * L, L))
    cum_sel = ((cj <= ci) | (ci >= L)).astype(F32).astype(BF)

    gs = range(groups)

    def chunk(c, carry):
        rows = pl.ds(pl.multiple_of(c * L, L), L)
        cols = [slice(g * P, (g + 1) * P) for g in gs]
        st = [st_sc[g] for g in gs]
        cs = [_sel_dot(cum_sel, g_ref[rows, cols[g]]) for g in gs]
        q = [q_ref[rows, cols[g]] for g in gs]
        k = [k_ref[rows, cols[g]] for g in gs]
        kk = [_dot_nt(k[g], k[g]) for g in gs]
        qk = [_dot_nt(q[g], k[g]) for g in gs]
        n_mats, dec_i, e_gc, beta = [], [], [], []
        for g in gs:
            gc = cs[g][:L]
            diff = gc[:, :L] - gc.T[:L, :]
            beta.append(beta_ref[rows, cols[g]])
            n_mats.append(-(beta[g][:, :L] * kk[g] * jnp.exp(jnp.where(strict, diff, NEG))))
            dec_i.append(jnp.exp(jnp.where(incl, diff, NEG)))
            e_gc.append(jnp.exp(gc))
        t_inv = _tri_inv(n_mats, L)
        wv = [_dot(t_inv[g], beta[g] * v_ref[rows, cols[g]]) for g in gs]
        wk = [_dot(t_inv[g], beta[g] * e_gc[g] * k[g]) for g in gs]
        uu = [wv[g] - _dot(wk[g], st[g]) for g in gs]
        qs = [_dot(q[g] * e_gc[g], st[g]) for g in gs]
        intra = [_dot(qk[g] * dec_i[g], uu[g]) for g in gs]
        upd = [_dot_tn(k[g] * jnp.exp(cs[g][L:] - cs[g][:L]), uu[g]) for g in gs]
        for g in gs:
            o_ref[rows, cols[g]] = qs[g] + intra[g]
            st_sc[g] = jnp.exp(cs[g][L:L + 1]) * st[g] + upd[g]
        return carry

    lax.fori_loop(0, tb // L, chunk, 0)

    @pl.when(t == pl.num_programs(2) - 1)
    def _():
        for g in range(groups):
            s_ref[0, g] = st_sc[g]


def _gdn_step_body(q_ref, k_ref, v_ref, beta_ref, g_ref, s0_ref, o_ref, s_ref, *, bb, steps):
    ii, jj = _iota2((GDN_DK, GDN_DK))
    eye = (ii == jj).astype(F32)

    def one_row(b_i):
        s = s0_ref[b_i, 0]
        for t in range(steps):
            row = lambda ref: ref[b_i, pl.ds(t, 1), :]
            k_col = jnp.sum(eye * row(k_ref), axis=1, keepdims=True)
            q_col = jnp.sum(eye * row(q_ref), axis=1, keepdims=True)
            beta = row(beta_ref)
            ks = jnp.sum(k_col * s, axis=0, keepdims=True)
            s = jnp.exp(row(g_ref)) * (s - (beta * k_col) * ks) + (beta * k_col) * row(v_ref)
            o_ref[b_i, pl.ds(t, 1), :] = jnp.sum(q_col * s, axis=0, keepdims=True)
        s_ref[b_i, 0] = s

    together = math.gcd(bb, STEP_ROWS)

    def one(i, carry):
        for r in range(together):
            one_row(i * together + r)
        return carry

    lax.fori_loop(0, bb // together, one, 0)


def _gdn_out_body(is_prompt, tile, x_ref, o_ref, z_ref, nw_ref, wo_ref, y_ref):
    del is_prompt, tile
    z = z_ref[...]
    nw = nw_ref[...]
    o = o_ref[...]
    parts = []
    for h in range(GDN_H):
        sl = slice(h * GDN_DV, (h + 1) * GDN_DV)
        parts.append(_rms(o[:, sl], nw, NORM_EPS))
    on = jnp.concatenate(parts, axis=1) * (z * _sigmoid(z))
    y_ref[...] = x_ref[...] + _dot(on, wo_ref[...])


def _gdn_layer(x, u, dims, conv_s, state_s, p):
    nb_p, seq_p, nb_s, seq_s = dims
    n_p, n_s = nb_p * seq_p, nb_s * seq_s
    hv = GDN_H * GDN_DV
    w_in = p['w_in']
    pad = lambda w: jnp.pad(w, ((0, 0), (0, LANES - w.shape[-1])))
    eb = (jnp.arange(LANES)[:, None] == jnp.arange(hv)[None, :] // GDN_DV).astype(BF)
    (qkv_p, qkv_s), z, (beta_p, beta_s), (g_p, g_s) = _rows_call(
        _gdn_proj_body, [u],
        [w_in[:, :GDN_QKV].astype(BF), w_in[:, GDN_QKV:GDN_QKV + hv].astype(BF),
         pad(w_in[:, GDN_QKV + hv:GDN_QKV + hv + GDN_H]).astype(BF), pad(w_in[:, GDN_QKV + hv + GDN_H:]).astype(BF),
         pad(p['a_log'][None]), pad(p['dt_bias'][None]), eb],
        [(GDN_QKV, F32), (hv, F32), (hv, F32), (hv, F32)], 256, "gdn_proj")
    tm = _row_tile(seq_p, 256)
    cw = p['conv_w']
    outs3 = [jax.ShapeDtypeStruct((n_p, hv), F32)] * 3
    q_p, k_p, v_p = pl.pallas_call(
        functools.partial(_gdn_conv_prompt_body, tiles_per_seq=seq_p // tm), grid=(n_p // tm,),
        in_specs=[pl.BlockSpec((tm, GDN_QKV), lambda i: (i, 0)),
                  pl.BlockSpec((8, GDN_QKV), lambda i: (jnp.maximum(i * (tm // 8) - 1, 0), 0)),
                  pl.BlockSpec((GDN_CONV, GDN_QKV), lambda i: (0, 0))],
        out_specs=[pl.BlockSpec((tm, hv), lambda i: (i, 0))] * 3, out_shape=outs3, name="gdn_conv_prompt",
        compiler_params=pltpu.CompilerParams(dimension_semantics=("parallel",), vmem_limit_bytes=VMEM_LIMIT),
    )(qkv_p, qkv_p, cw)
    xp_s = jnp.concatenate([conv_s, qkv_s.reshape(nb_s, seq_s, GDN_QKV)], axis=1)
    taps = [xp_s[:, j:j + seq_s].reshape(n_s, GDN_QKV) for j in range(GDN_CONV)]
    ts = _row_tile(n_s, 256)
    q_s, k_s, v_s = pl.pallas_call(
        _gdn_conv_taps_body, grid=(n_s // ts,),
        in_specs=[pl.BlockSpec((ts, GDN_QKV), lambda i: (i, 0))] * GDN_CONV
        + [pl.BlockSpec((GDN_CONV, GDN_QKV), lambda i: (0, 0))],
        out_specs=[pl.BlockSpec((ts, hv), lambda i: (i, 0))] * 3,
        out_shape=[jax.ShapeDtypeStruct((n_s, hv), F32)] * 3, name="gdn_conv_sample",
        compiler_params=pltpu.CompilerParams(dimension_semantics=("parallel",), vmem_limit_bytes=VMEM_LIMIT),
    )(*taps, cw)
    groups = GDN_GROUPS
    tb = _row_tile(seq_p, SEQ_BLOCK)
    nt = seq_p // tb
    spec = pl.BlockSpec((tb, groups * LANES), lambda b, h, t: (b * nt + t, h))
    o_p, s_p = pl.pallas_call(
        functools.partial(_gdn_chunk_body, tb=tb, groups=groups), grid=(nb_p, GDN_H // groups, nt),
        in_specs=[spec] * 5,
        out_specs=[spec, pl.BlockSpec((1, groups, GDN_DK, GDN_DV), lambda b, h, t: (b, h, 0, 0))],
        out_shape=[jax.ShapeDtypeStruct((n_p, hv), F32), jax.ShapeDtypeStruct((nb_p, GDN_H, GDN_DK, GDN_DV), F32)],
        scratch_shapes=[pltpu.VMEM((groups, GDN_DK, GDN_DV), F32)],
        name="gdn_chunk",
        compiler_params=pltpu.CompilerParams(dimension_semantics=("parallel", "parallel", "arbitrary"),
                                             vmem_limit_bytes=VMEM_LIMIT),
    )(q_p, k_p, v_p, beta_p, g_p)
    bb = _row_tile(nb_s, 8)
    sspec = pl.BlockSpec((bb, seq_s, LANES), lambda b, h: (b, 0, h))
    stspec = pl.BlockSpec((bb, 1, GDN_DK, GDN_DV), lambda b, h: (b, h, 0, 0))
    r3 = lambda t: t.reshape(nb_s, seq_s, hv)
    o_s, s_s = pl.pallas_call(
        functools.partial(_gdn_step_body, bb=bb, steps=seq_s), grid=(nb_s // bb, GDN_H),
        in_specs=[sspec] * 5 + [stspec], out_specs=[sspec, stspec],
        out_shape=[jax.ShapeDtypeStruct((nb_s, seq_s, hv), F32),
                   jax.ShapeDtypeStruct((nb_s, GDN_H, GDN_DK, GDN_DV), F32)],
        name="gdn_step",
        compiler_params=pltpu.CompilerParams(dimension_semantics=("parallel", "parallel"),
                                             vmem_limit_bytes=VMEM_LIMIT),
    )(r3(q_s), r3(k_s), r3(v_s), r3(beta_s), r3(g_s), state_s)
    x = _rows_call(_gdn_out_body, [x, (o_p, o_s.reshape(n_s, hv)), z], [p['norm_w'][None], p['w_o'].astype(BF)],
                   [(D_MODEL, F32)], 512, "gdn_out")[0]
    keep = GDN_CONV - 1
    tail_p = qkv_p.reshape(nb_p, seq_p, GDN_QKV)[:, -keep:]
    conv_p = jnp.pad(tail_p, ((0, 0), (max(keep - seq_p, 0), 0), (0, 0)))
    return x, (s_p, conv_p, s_s, xp_s[:, -keep:])


def kernel(x_prompt, x_sample, state_rwkv_wkv, state_rwkv_shift, cache_attn_k, cache_attn_v, state_gdn, state_gdn_conv, page_table, norm_w, final_norm_w, ffn_w_gate, ffn_w_up, ffn_w_down, rwkv_mu, rwkv_w_rkv, rwkv_w_o, rwkv_w0, rwkv_w1, rwkv_w2, rwkv_a0, rwkv_a1, rwkv_a2, rwkv_g1, rwkv_g2, rwkv_k_k, rwkv_k_a, rwkv_r_k, rwkv_lnx_w, rwkv_lnx_b, rwkv_v0, rwkv_v1, rwkv_v2, attn_w_qkv, attn_w_o, attn_lambda, attn_subln_w, rel_bias, gdn_w_in, gdn_conv_w, gdn_a_log, gdn_dt_bias, gdn_norm_w, gdn_w_o):
    nb_p, seq_p, _ = x_prompt.shape
    nb_s, seq_s, _ = x_sample.shape
    dims = (nb_p, seq_p, nb_s, seq_s)
    depth = norm_w.shape[0]
    x = (x_prompt.reshape(nb_p * seq_p, D_MODEL), x_sample.reshape(nb_s * seq_s, D_MODEL))
    v_first = None
    rw, at, gd = [], [], []
    for i in range(depth):
        kind, j = i % 3, i // 3
        x, u = _ffn(x, norm_w[i, 0], ffn_w_gate[i, 0], ffn_w_up[i, 0], ffn_w_down[i, 0], norm_w[i, 1])
        if kind == 0:
            p = dict(mu=rwkv_mu[j], w_rkv=rwkv_w_rkv[j], w_o=rwkv_w_o[j], w0=rwkv_w0[j], w1=rwkv_w1[j],
                     w2=rwkv_w2[j], a0=rwkv_a0[j], a1=rwkv_a1[j], a2=rwkv_a2[j], g1=rwkv_g1[j], g2=rwkv_g2[j],
                     k_k=rwkv_k_k[j], k_a=rwkv_k_a[j], r_k=rwkv_r_k[j], lnx_w=rwkv_lnx_w[j], lnx_b=rwkv_lnx_b[j])
            vres = None if j == 0 else (rwkv_v0[j - 1], rwkv_v1[j - 1], rwkv_v2[j - 1])
            x, v_first, st = _rwkv_layer(x, u, dims, state_rwkv_shift[j], state_rwkv_wkv[j], p, v_first, vres)
            rw.append(st)
        elif kind == 1:
            p = dict(w_qkv=attn_w_qkv[j], w_o=attn_w_o[j], lam=attn_lambda[j], subln_w=attn_subln_w[j])
            lam_init = 0.8 - 0.6 * math.exp(-0.3 * i)
            x, st = _attn_layer(x, u, dims, cache_attn_k, cache_attn_v, j, page_table, p, lam_init, rel_bias)
            at.append(st)
        else:
            p = dict(w_in=gdn_w_in[j], conv_w=gdn_conv_w[j], a_log=gdn_a_log[j], dt_bias=gdn_dt_bias[j],
                     norm_w=gdn_norm_w[j], w_o=gdn_w_o[j])
            x, st = _gdn_layer(x, u, dims, state_gdn_conv[j], state_gdn[j], p)
            gd.append(st)
        g_next = final_norm_w if i == depth - 1 else jnp.ones((D_MODEL,), F32)
        x, y = _ffn(x, norm_w[i, 2], ffn_w_gate[i, 1], ffn_w_up[i, 1], ffn_w_down[i, 1], g_next)
    stack = lambda lst, k: jnp.stack([t[k] for t in lst])
    return (y[0].reshape(x_prompt.shape), y[1].reshape(x_sample.shape),
            stack(rw, 0), stack(rw, 1), stack(at, 0), stack(at, 1), stack(gd, 0), stack(gd, 1),
            stack(rw, 2), stack(rw, 3), stack(at, 2), stack(at, 3), stack(gd, 2), stack(gd, 3))
```

```python
import functools
import math

import jax
import jax.numpy as jnp
from jax import lax
from jax.experimental import pallas as pl
from jax.experimental.pallas import tpu as pltpu

F32 = jnp.float32
BF = jnp.bfloat16

D_MODEL = 1024
NORM_EPS = 1e-6
RWKV_N = 64
RWKV_H = D_MODEL // RWKV_N
RWKV_LNX_EPS = 64e-5
DIFF_H = 8
DIFF_DH = 64
DIFF_E = 128
SUBLN_EPS = 1e-5
REL_BUCKETS = 32
REL_MAX_EXACT = 16
REL_MAX_DIST = 128
GDN_H = 8
GDN_DK = 128
GDN_DV = 128
GDN_CONV = 4
GDN_QKV = GDN_H * (2 * GDN_DK + GDN_DV)
FFN_TILE = 1024
FFN_CHUNK_BIG_TILE = 256
FFN_CHUNK_SMALL_TILE = 1408
CHUNK = 64
SEQ_BLOCK = 512
RWKV_GROUPS = 8
GDN_GROUPS = 8
STEP_HEADS = 4
STEP_ROWS = 8
LANES = 128
NEG = -1e30
LOG2E = math.log2(math.e)
VMEM_LIMIT = 56 * 1024 * 1024


def _dot(a, b):
    return jnp.dot(a.astype(BF), b.astype(BF), preferred_element_type=F32)


def _dot_nt(a, b):
    return lax.dot_general(a.astype(BF), b.astype(BF), (((1,), (1,)), ((), ())), preferred_element_type=F32)


def _dot_tn(a, b):
    return lax.dot_general(a.astype(BF), b.astype(BF), (((0,), (0,)), ((), ())), preferred_element_type=F32)


def _split(x, n):
    out = []
    for _ in range(n):
        h = x.astype(BF)
        out.append(h)
        x = x - h.astype(F32)
    return out


def _sel_dot(sel, x, n=3):
    return sum(jnp.dot(sel, t, preferred_element_type=F32) for t in _split(x, n))


def _dot_sel(x, sel, n=2):
    return sum(jnp.dot(t, sel, preferred_element_type=F32) for t in _split(x, n))


def _rms(x, g, eps):
    return x * lax.rsqrt(jnp.mean(x * x, axis=-1, keepdims=True) + eps) * g


def _sigmoid(x):
    return 1.0 / (1.0 + jnp.exp(-x))


def _softplus(x):
    return jnp.maximum(x, 0.0) + jnp.log(1.0 + jnp.exp(-jnp.abs(x)))


def _iota2(shape):
    return lax.broadcasted_iota(jnp.int32, shape, 0), lax.broadcasted_iota(jnp.int32, shape, 1)


def _cumsum_rows(x):
    n = x.shape[0]
    row = lax.broadcasted_iota(jnp.int32, x.shape, 0)
    shift = 1
    while shift < n:
        x = x + jnp.where(row >= shift, pltpu.roll(x, shift, 0), 0.0)
        shift *= 2
    return x


def _tri_inv(n_mats, size):
    ii, jj = _iota2((size, size))
    eye = (ii == jj).astype(F32)
    xs = [eye + jnp.where((ii >> 1) == (jj >> 1), n, 0.0) for n in n_mats]
    lvl = 1
    while (2 << lvl) <= size:
        sel = ((ii >> (lvl + 1)) == (jj >> (lvl + 1))) & (((ii >> lvl) & 1) == 1) & (((jj >> lvl) & 1) == 0)
        ys = [_dot(x, jnp.where(sel, n, 0.0)) for x, n in zip(xs, n_mats)]
        xs = [x + _dot(y, x) for x, y in zip(xs, ys)]
        lvl += 1
    return xs


def _row_tile(n, want):
    tm = min(want, n)
    while n % tm:
        tm //= 2
    return tm


HALO = 8
SAMPLE_TILE = 128


def _rows_call(body, rows, consts, outs, tm, name, halo=()):
    n_p, n_s = rows[0][0].shape[0], rows[0][1].shape[0]
    tm_p, tm_s = _row_tile(n_p, tm), _row_tile(n_s, min(tm, SAMPLE_TILE))
    tp, ts = n_p // tm_p, n_s // tm_s
    p_idx = lambda i: jnp.minimum(i, tp - 1)
    s_idx = lambda i: jnp.maximum(i - tp, 0)
    in_specs, args = [], []
    for k, (a_p, a_s) in enumerate(rows):
        if k in halo:
            in_specs.append(pl.BlockSpec((HALO, a_p.shape[1]),
                                         lambda i: (jnp.maximum(p_idx(i) * (tm_p // HALO) - 1, 0), 0)))
        else:
            in_specs.append(pl.BlockSpec((tm_p, a_p.shape[1]), lambda i: (p_idx(i), 0)))
        in_specs.append(pl.BlockSpec((tm_s, a_s.shape[1]), lambda i: (s_idx(i), 0)))
        args += [a_p, a_s]
    for a in consts:
        in_specs.append(pl.BlockSpec(a.shape, lambda i, nd=a.ndim: (0,) * nd, pipeline_mode=pl.Buffered(1)))
    out_specs, out_shape = [], []
    for c, dt in outs:
        out_specs += [pl.BlockSpec((tm_p, c), lambda i: (p_idx(i), 0)), pl.BlockSpec((tm_s, c), lambda i: (s_idx(i), 0))]
        out_shape += [jax.ShapeDtypeStruct((n_p, c), dt), jax.ShapeDtypeStruct((n_s, c), dt)]
    nr, nc = len(rows), len(consts)

    def kern(*refs):
        i = pl.program_id(0)
        cs = refs[2 * nr:2 * nr + nc]
        o = refs[2 * nr + nc:]

        @pl.when(i < tp)
        def _():
            body(True, i, *refs[0:2 * nr:2], *cs, *o[0::2])

        @pl.when(i >= tp)
        def _():
            body(False, i - tp, *refs[1:2 * nr:2], *cs, *o[1::2])

    res = pl.pallas_call(
        kern, grid=(tp + ts,), in_specs=in_specs, out_specs=out_specs, out_shape=out_shape, name=name,
        compiler_params=pltpu.CompilerParams(dimension_semantics=("arbitrary",), vmem_limit_bytes=VMEM_LIMIT),
    )(*args, *consts)
    return [(res[2 * k], res[2 * k + 1]) for k in range(len(outs))]


def _ffn_body(is_prompt, tile, x_ref, g_ref, wg_ref, wu_ref, wd_ref, g2_ref, o_ref, u_ref):
    del is_prompt, tile
    x = x_ref[...]
    h = _rms(x, g_ref[...], NORM_EPS).astype(BF)
    hidden = wg_ref.shape[1]
    want = FFN_CHUNK_BIG_TILE if x.shape[0] >= FFN_TILE else FFN_CHUNK_SMALL_TILE
    nchunk = max(n for n in range(1, hidden // LANES + 1) if hidden % (n * LANES) == 0 and hidden // n >= want)
    fc = hidden // nchunk
    acc = jnp.zeros_like(x)
    for c in range(nchunk):
        sl = slice(c * fc, (c + 1) * fc)
        gate = jnp.dot(h, wg_ref[:, sl], preferred_element_type=F32)
        up = jnp.dot(h, wu_ref[:, sl], preferred_element_type=F32)
        act = (gate * _sigmoid(gate) * up).astype(BF)
        acc = acc + jnp.dot(act, wd_ref[sl, :], preferred_element_type=F32)
    y = x + 0.5 * acc
    o_ref[...] = y
    u_ref[...] = _rms(y, g2_ref[...], NORM_EPS)


def _ffn(x, g, wg, wu, wd, g_next):
    return _rows_call(_ffn_body, [x], [g[None], wg.astype(BF), wu.astype(BF), wd.astype(BF), g_next[None]],
                      [(D_MODEL, F32), (D_MODEL, F32)], FFN_TILE, "ffn_half")


def _linear_body(is_prompt, tile, *refs, n_out, residual):
    del is_prompt, tile
    if residual:
        x_ref, u_ref, w_ref = refs[:3]
        outs = refs[3:]
    else:
        u_ref, w_ref = refs[:2]
        outs = refs[2:]
    y = jnp.dot(u_ref[...].astype(BF), w_ref[...], preferred_element_type=F32)
    if residual:
        outs[0][...] = x_ref[...] + y
    else:
        c = y.shape[1] // n_out
        for k, o in enumerate(outs):
            o[...] = y[:, k * c:(k + 1) * c]


def _linear(u, w, n_out, name):
    body = functools.partial(_linear_body, n_out=n_out, residual=False)
    c = w.shape[1] // n_out
    return _rows_call(body, [u], [w.astype(BF)], [(c, F32)] * n_out, 512, name)


def _linear_res(x, u, w, name):
    body = functools.partial(_linear_body, n_out=1, residual=True)
    return _rows_call(body, [x, u], [w.astype(BF)], [(w.shape[1], F32)], 512, name)[0]


def _head_sel(width, heads):
    e = (jnp.arange(width)[:, None] // (width // heads) == jnp.arange(heads)[None, :]).astype(BF)
    return e, e.T


def _rwkv_proj_body(is_prompt, tile, *refs, has_vres, tiles_per_seq):
    (u_ref, prev_ref) = refs[:2]
    k0 = 2
    if has_vres:
        vf_ref = refs[2]
        k0 = 3
    (mu_ref, wr_ref, wk_ref, wv_ref, w0_ref, w1_ref, w2_ref, a0_ref, a1_ref, a2_ref, g1_ref, g2_ref,
     kk_ref, ka_ref, rk_ref, e_ref, et_ref) = refs[k0:k0 + 17]
    k1 = k0 + 17
    if has_vres:
        v0_ref, v1_ref, v2_ref = refs[k1:k1 + 3]
        k1 += 3
    r_o, lw_o, k_o, v_o, a_o, b_o, g_o, bon_o = refs[k1:]
    u = u_ref[...]
    if is_prompt:
        before = jnp.where(tile % tiles_per_seq == 0, 0.0, prev_ref[HALO - 1:HALO, :])
        row = lax.broadcasted_iota(jnp.int32, u.shape, 0)
        prev = jnp.where(row == 0, before, pltpu.roll(u, 1, 0))
    else:
        prev = prev_ref[...]
    xx = prev - u
    mu = mu_ref[...]
    xr, xw, xk = u + xx * mu[0:1], u + xx * mu[1:2], u + xx * mu[2:3]
    xv, xa, xg = u + xx * mu[3:4], u + xx * mu[4:5], u + xx * mu[5:6]
    r = _dot(xr, wr_ref[...])
    k = _dot(xk, wk_ref[...])
    v = _dot(xv, wv_ref[...])
    wlog = -_softplus(-(w0_ref[...] + _dot(jnp.tanh(_dot(xw, w1_ref[...])), w2_ref[...]))) - 0.5
    lw_o[...] = -jnp.exp(wlog)
    if has_vres:
        v = v + (vf_ref[...] - v) * _sigmoid(v0_ref[...] + _dot(_dot(xv, v1_ref[...]), v2_ref[...]))
    a = _sigmoid(a0_ref[...] + _dot(_dot(xa, a1_ref[...]), a2_ref[...]))
    g_o[...] = _dot(_sigmoid(_dot(xg, g1_ref[...])), g2_ref[...])
    e, et = e_ref[...], et_ref[...]
    kk = k * kk_ref[...]
    kk = kk * lax.rsqrt(_dot_sel(_dot_sel(kk * kk, e), et) + 1e-24)
    k_mod = k * (1.0 + (a - 1.0) * ka_ref[...])
    r_o[...] = r
    k_o[...] = k_mod
    v_o[...] = v
    a_o[...] = -kk
    b_o[...] = kk * a
    bon_o[...] = _dot_sel(_dot_sel(r * k_mod * rk_ref[...], e), et) * v


def _rwkv_chunk_body(r_ref, lw_ref, k_ref, v_ref, a_ref, b_ref, o_ref, s_ref, st_sc, *, tb, groups):
    L, P, N = CHUNK, LANES, RWKV_N
    t = pl.program_id(2)

    @pl.when(t == 0)
    def _():
        st_sc[...] = jnp.zeros(st_sc.shape, F32)

    ii, jj = _iota2((L, L))
    strict = (ii > jj).astype(F32)
    incl = (ii >= jj).astype(F32)
    mask2 = jnp.concatenate([strict, incl], axis=0)
    lane = lax.broadcasted_iota(jnp.int32, (L, P), 1)
    lane2 = lax.broadcasted_iota(jnp.int32, (2 * L, P), 1)
    bi, bj = _iota2((P, P))
    bd = ((bi >= N) == (bj >= N)).astype(F32)

    gs = range(groups)
    chains = [(g, h) for g in gs for h in range(2)]
    head_lanes = [(lane >= N) == bool(h) for h in range(2)]
    head_lanes2 = [(lane2 >= N) == bool(h) for h in range(2)]

    def chunk(c, carry):
        rows = pl.ds(pl.multiple_of(c * L, L), L)
        cols = [slice(g * P, (g + 1) * P) for g in gs]
        st = [st_sc[g] for g in gs]
        lw = [lw_ref[rows, cols[g]] for g in gs]
        x, bk, v, bk_end, w_end = [], [], [], [], []
        for g in gs:
            gc = _cumsum_rows(lw[g])
            gl = gc[L - 1:L, :]
            r, k, a, b = r_ref[rows, cols[g]], k_ref[rows, cols[g]], a_ref[rows, cols[g]], b_ref[rows, cols[g]]
            e_neg = jnp.exp(-gc)
            e_end = jnp.exp(gl - gc)
            x.append(jnp.concatenate([a * jnp.exp(gc - lw[g]), r * jnp.exp(gc)], axis=0).astype(BF))
            bk.append(jnp.concatenate([b * e_neg, k * e_neg], axis=0).astype(BF))
            v.append(v_ref[rows, cols[g]])
            bk_end.append(jnp.concatenate([b * e_end, k * e_end], axis=0))
            w_end.append(jnp.exp(gl))
        xs = [_dot_nt(x[g], st[g]) for g in gs]
        xh = {(g, h): jnp.where(head_lanes2[h], x[g], jnp.zeros_like(x[g])) for g, h in chains}
        abk = {ch: _dot_nt(xh[ch], bk[ch[0]]) for ch in chains}
        ab = {ch: abk[ch][:, :L] * mask2 for ch in chains}
        ak = {ch: abk[ch][:, L:] * mask2 for ch in chains}
        t_inv = dict(zip(chains, _tri_inv([ab[ch][:L] for ch in chains], L)))
        vh = {(g, h): jnp.where(head_lanes[h], v[g], 0.0) for g, h in chains}
        akv = {ch: _dot(ak[ch], vh[ch]) for ch in chains}
        rhs = [xs[g][:L] + akv[(g, 0)][:L] + akv[(g, 1)][:L] for g in gs]
        uu_h = {(g, h): _dot(t_inv[(g, h)], jnp.where(head_lanes[h], rhs[g], 0.0)) for g, h in chains}
        uu = [uu_h[(g, 0)] + uu_h[(g, 1)] for g in gs]
        o_h = {(g, h): _dot(ab[(g, h)][L:], jnp.where(head_lanes[h], uu[g], 0.0)) for g, h in chains}
        out_h = {ch: akv[ch][L:] for ch in chains}
        upd = [_dot_tn(jnp.concatenate([uu[g], v[g]], axis=0), bk_end[g]) for g in gs]
        for g in gs:
            o_ref[rows, cols[g]] = xs[g][L:] + out_h[(g, 0)] + out_h[(g, 1)] + o_h[(g, 0)] + o_h[(g, 1)]
            st_sc[g] = st[g] * w_end[g] + bd * upd[g]
        return carry

    lax.fori_loop(0, tb // L, chunk, 0)

    @pl.when(t == pl.num_programs(2) - 1)
    def _():
        for g in range(groups):
            s = st_sc[g]
            s_ref[0, 2 * g] = s[:N, :N]
            s_ref[0, 2 * g + 1] = s[N:, N:]


def _rwkv_chunk(seqs, nb, seq):
    groups = RWKV_GROUPS
    width = groups * LANES
    tb = _row_tile(seq, SEQ_BLOCK)
    nt = seq // tb
    spec = pl.BlockSpec((tb, width), lambda b, h, t: (b * nt + t, h))
    return pl.pallas_call(
        functools.partial(_rwkv_chunk_body, tb=tb, groups=groups), grid=(nb, D_MODEL // width, nt),
        in_specs=[spec] * 6,
        out_specs=[spec, pl.BlockSpec((1, 2 * groups, RWKV_N, RWKV_N), lambda b, h, t: (b, h, 0, 0))],
        out_shape=[jax.ShapeDtypeStruct((nb * seq, D_MODEL), F32),
                   jax.ShapeDtypeStruct((nb, RWKV_H, RWKV_N, RWKV_N), F32)],
        scratch_shapes=[pltpu.VMEM((groups, LANES, LANES), F32)],
        name="rwkv_chunk",
        compiler_params=pltpu.CompilerParams(dimension_semantics=("parallel", "parallel", "arbitrary"),
                                             vmem_limit_bytes=VMEM_LIMIT),
    )(*seqs)


def _rwkv_step_body(r_ref, lw_ref, k_ref, v_ref, a_ref, b_ref, s0_ref, o_ref, s_ref, *, bb, steps):
    N = RWKV_N
    ii, jj = _iota2((N, N))
    eye = (ii == jj).astype(F32)

    def head(b_i, h):
        s = s0_ref[b_i, h]
        for t in range(steps):
            row = lambda ref: ref[b_i, t, pl.ds(h, 1), :]
            sa = jnp.sum(s * row(a_ref), axis=1, keepdims=True)
            v_col = jnp.sum(eye * row(v_ref), axis=1, keepdims=True)
            s = s * jnp.exp(row(lw_ref)) + sa * row(b_ref) + v_col * row(k_ref)
            o_col = jnp.sum(s * row(r_ref), axis=1, keepdims=True)
            o_ref[b_i, t, pl.ds(h, 1), :] = jnp.sum(eye * o_col, axis=0, keepdims=True)
        s_ref[b_i, h] = s

    per_row = RWKV_H // STEP_HEADS

    def one(i, carry):
        b_i = i // per_row
        h0 = (i % per_row) * STEP_HEADS
        for hh in range(STEP_HEADS):
            head(b_i, h0 + hh)
        return carry

    lax.fori_loop(0, bb * per_row, one, 0)


def _rwkv_step(seqs, s0, nb, steps):
    bb = _row_tile(nb, 8)
    spec = pl.BlockSpec((bb, steps, RWKV_H, RWKV_N), lambda b: (b, 0, 0, 0))
    sspec = pl.BlockSpec((bb, RWKV_H, RWKV_N, RWKV_N), lambda b: (b, 0, 0, 0))
    seqs = [t.reshape(nb, steps, RWKV_H, RWKV_N) for t in seqs]
    o, s = pl.pallas_call(
        functools.partial(_rwkv_step_body, bb=bb, steps=steps), grid=(nb // bb,),
        in_specs=[spec] * 6 + [sspec], out_specs=[spec, sspec],
        out_shape=[jax.ShapeDtypeStruct((nb, steps, RWKV_H, RWKV_N), F32),
                   jax.ShapeDtypeStruct((nb, RWKV_H, RWKV_N, RWKV_N), F32)],
        name="rwkv_step",
        compiler_params=pltpu.CompilerParams(dimension_semantics=("parallel",), vmem_limit_bytes=VMEM_LIMIT),
    )(*seqs, s0)
    return o.reshape(nb * steps, D_MODEL), s


def _rwkv_out_body(is_prompt, tile, x_ref, o_ref, g_ref, bon_ref, lw_ref, lb_ref, e_ref, et_ref, wo_ref, y_ref):
    del is_prompt, tile
    e, et = e_ref[...], et_ref[...]
    o = o_ref[...]
    inv_n = 1.0 / RWKV_N
    d = o - _dot_sel(_dot_sel(o, e), et) * inv_n
    var = _dot_sel(_dot_sel(d * d, e), et) * inv_n
    on = d * lax.rsqrt(var + RWKV_LNX_EPS) * lw_ref[...] + lb_ref[...]
    y_ref[...] = x_ref[...] + _dot((on + bon_ref[...]) * g_ref[...], wo_ref[...])


def _rwkv_layer(x, u, dims, shift_s, wkv_s, p, v_first, vres):
    nb_p, seq_p, nb_s, seq_s = dims
    u_p, u_s = u
    u3s = u_s.reshape(nb_s, seq_s, D_MODEL)
    prev_s = jnp.concatenate([shift_s[:, None, :], u3s[:, :-1]], axis=1).reshape(nb_s * seq_s, D_MODEL)
    e, et = _head_sel(D_MODEL, RWKV_H)
    has_vres = vres is not None
    tm = _row_tile(seq_p, 256)
    rows = [u, (u_p, prev_s)] + ([v_first] if has_vres else [])
    consts = [p['mu'], p['w_rkv'][0].astype(BF), p['w_rkv'][1].astype(BF), p['w_rkv'][2].astype(BF),
              p['w0'][None], p['w1'].astype(BF), p['w2'].astype(BF), p['a0'][None], p['a1'].astype(BF),
              p['a2'].astype(BF), p['g1'].astype(BF), p['g2'].astype(BF), p['k_k'][None], p['k_a'][None],
              p['r_k'].reshape(1, D_MODEL), e, et]
    if has_vres:
        consts += [vres[0][None], vres[1].astype(BF), vres[2].astype(BF)]
    body = functools.partial(_rwkv_proj_body, has_vres=has_vres, tiles_per_seq=seq_p // tm)
    r, lw, k, v, a, b, gate, bonus = _rows_call(body, rows, consts, [(D_MODEL, F32)] * 8, tm, "rwkv_proj", halo=(1,))
    seqs = (r, lw, k, v, a, b)
    o_p, s_p = _rwkv_chunk([t[0] for t in seqs], nb_p, seq_p)
    o_s, s_s = _rwkv_step([t[1] for t in seqs], wkv_s, nb_s, seq_s)
    x = _rows_call(_rwkv_out_body, [x, (o_p, o_s), gate, bonus],
                   [p['lnx_w'][None], p['lnx_b'][None], e, et, p['w_o'].astype(BF)], [(D_MODEL, F32)], 512,
                   "rwkv_out")[0]
    u3p = u_p.reshape(nb_p, seq_p, D_MODEL)
    return x, (v if not has_vres else v_first), (s_p, u3p[:, -1], s_s, u3s[:, -1])


def _t5_bias(dist, rel_bias):
    n = jnp.maximum(dist, 0)
    nf = jnp.maximum(n, 1).astype(F32)
    large = REL_MAX_EXACT + (jnp.log(nf / REL_MAX_EXACT) / math.log(REL_MAX_DIST / REL_MAX_EXACT)
                             * (REL_BUCKETS - REL_MAX_EXACT)).astype(jnp.int32)
    large = jnp.minimum(large, REL_BUCKETS - 1)
    bucket = jnp.where(n < REL_MAX_EXACT, n, large)
    onehot = (bucket[..., None] == jnp.arange(REL_BUCKETS)).astype(F32)
    return jnp.einsum('...b,bh->...h', onehot, rel_bias.astype(F32), precision=lax.Precision.HIGHEST)


def _lam_of(lam_ref, lam_init):
    lv = lam_ref[...]
    return (jnp.exp(jnp.sum(lv[0:1] * lv[1:2], axis=1, keepdims=True))
            - jnp.exp(jnp.sum(lv[2:3] * lv[3:4], axis=1, keepdims=True)) + lam_init)


ATTN_HEADS = 2


def _attn_prompt_body(q_ref, k_ref, v_ref, bias_ref, far_ref, lam_ref, sw_ref, o_ref, kb_sc, vt_sc, *, tq, lam_init):
    qi = pl.program_id(2)
    E, dh = DIFF_E, DIFF_DH
    nkv = kb_sc.shape[0]
    chains = [(hh, c) for hh in range(ATTN_HEADS) for c in range(2)]

    @pl.when(qi == 0)
    def _():
        for j in range(nkv):
            kb_sc[j] = k_ref[j * tq:(j + 1) * tq, :].astype(BF)
            vt_sc[j] = v_ref[j * tq:(j + 1) * tq, :].T.astype(BF)

    q_t = (q_ref[...] * (dh ** -0.5 * LOG2E)).T
    row = lax.broadcasted_iota(jnp.int32, q_t.shape, 0)
    q_c = {(hh, c): jnp.where((row >= hh * E + c * dh) & (row < hh * E + (c + 1) * dh), q_t, 0.0).astype(BF)
           for hh, c in chains}
    kk, qq = _iota2((tq, tq))

    def step(carry, kj, bias=None, shift=None):
        kb = kb_sc[kj]
        s = {ch: jnp.dot(kb, q_c[ch], preferred_element_type=F32) for ch in chains}
        if bias is not None:
            s = {ch: s[ch] + bias[ch[0]] for ch in chains}
        top = {ch: jnp.max(s[ch], axis=0, keepdims=True) for ch in chains}
        if shift is not None:
            top = {ch: top[ch] + shift[ch[0]] for ch in chains}
        m_new = {ch: jnp.maximum(carry[ch][0], top[ch]) for ch in chains}
        alpha = {ch: jnp.exp2(carry[ch][0] - m_new[ch]) for ch in chains}
        offs = m_new if shift is None else {ch: m_new[ch] - shift[ch[0]] for ch in chains}
        pr = {ch: jnp.exp2(s[ch] - offs[ch]) for ch in chains}
        l_new = {ch: carry[ch][1] * alpha[ch] + jnp.sum(pr[ch], axis=0, keepdims=True) for ch in chains}
        vt = vt_sc[kj]
        pv = {(hh, c): jnp.dot(vt[hh * E:(hh + 1) * E, :], pr[(hh, c)].astype(BF), preferred_element_type=F32)
              for hh, c in chains}
        return {ch: (m_new[ch], l_new[ch], carry[ch][2] * alpha[ch] + pv[ch]) for ch in chains}

    init = {ch: (jnp.full((1, tq), NEG, F32), jnp.zeros((1, tq), F32), jnp.zeros((E, tq), F32)) for ch in chains}
    carry = step(init, qi, bias=[jnp.where(kk <= qq, bias_ref[hh, 0], NEG) for hh in range(ATTN_HEADS)])
    carry = step(carry, jnp.maximum(qi - 1, 0),
                 bias=[jnp.where(qi >= 1, bias_ref[hh, 1], NEG) for hh in range(ATTN_HEADS)])
    far = [far_ref[hh, 0:1, :] for hh in range(ATTN_HEADS)]
    carry = lax.fori_loop(0, jnp.maximum(qi - 1, 0), lambda kj, c: step(c, kj, shift=far), carry)
    lam = _lam_of(lam_ref, lam_init)
    for hh in range(ATTN_HEADS):
        (_, l1, acc1), (_, l2, acc2) = carry[(hh, 0)], carry[(hh, 1)]
        o_t = acc1 / l1 - lam * (acc2 / l2)
        o_t = o_t * lax.rsqrt(jnp.mean(o_t * o_t, axis=0, keepdims=True) + SUBLN_EPS) * sw_ref[...]
        o_ref[:, hh * E:(hh + 1) * E] = o_t.T * (1.0 - lam_init)


def _attn_prompt(q, k, v, nb, seq, rel_bias, lam, subln_w, lam_init):
    tq = _row_tile(seq, 256)
    assert tq >= REL_MAX_DIST or tq == seq
    nq = seq // tq
    width = ATTN_HEADS * DIFF_E
    d0 = jnp.arange(tq)[None, :] - jnp.arange(tq)[:, None]
    bias = jnp.stack([_t5_bias(d0, rel_bias), _t5_bias(d0 + tq, rel_bias)])
    bias = jnp.transpose(bias, (3, 0, 1, 2)) * LOG2E
    far = jnp.broadcast_to(rel_bias[REL_BUCKETS - 1][:, None, None] * LOG2E, (DIFF_H, 8, tq)).astype(F32)
    qspec = pl.BlockSpec((tq, width), lambda h, b, i: (b * nq + i, h))
    kspec = pl.BlockSpec((seq, width), lambda h, b, i: (b, h))
    return pl.pallas_call(
        functools.partial(_attn_prompt_body, tq=tq, lam_init=lam_init), grid=(DIFF_H // ATTN_HEADS, nb, nq),
        in_specs=[qspec, kspec, kspec,
                  pl.BlockSpec((ATTN_HEADS, 2, tq, tq), lambda h, b, i: (h, 0, 0, 0)),
                  pl.BlockSpec((ATTN_HEADS, 8, tq), lambda h, b, i: (h, 0, 0)),
                  pl.BlockSpec((4, DIFF_DH), lambda h, b, i: (0, 0)),
                  pl.BlockSpec((DIFF_E, 1), lambda h, b, i: (0, 0))],
        out_specs=qspec, out_shape=jax.ShapeDtypeStruct((nb * seq, D_MODEL), F32),
        scratch_shapes=[pltpu.VMEM((nq, tq, width), BF), pltpu.VMEM((nq, width, tq), BF)], name="attn_prompt",
        compiler_params=pltpu.CompilerParams(dimension_semantics=("parallel", "parallel", "arbitrary"),
                                             vmem_limit_bytes=VMEM_LIMIT),
    )(q, k, v, bias, far, lam, subln_w[:, None])


def _attn_sample_body(pt_ref, q_ref, *refs, n_pages, lam_init):
    del pt_ref
    k_refs, v_refs = refs[:n_pages], refs[n_pages:2 * n_pages]
    kn_ref, vn_ref, bias_ref, lam_ref, sw_ref, o_ref = refs[2 * n_pages:]
    page = k_refs[0].shape[0] // DIFF_H
    steps = o_ref.shape[1]
    heads = range(DIFF_H)
    lam = _lam_of(lam_ref, lam_init)
    sw = sw_ref[...]
    zpad = jnp.zeros((page - kn_ref.shape[2], DIFF_E), BF)

    def head_rows(ref, h):
        return ref[pl.ds(h, page, stride=DIFF_H), :].astype(BF)

    def new_rows(ref, h):
        return jnp.concatenate([ref[0, h].astype(BF), zpad], axis=0)

    scores = [[_dot_nt(q_ref[0, h], head_rows(k_refs[i], h)) for i in range(n_pages)]
              + [_dot_nt(q_ref[0, h], new_rows(kn_ref, h))] for h in heads]
    probs, denom = [], []
    for h in heads:
        sc = jnp.concatenate(scores[h], axis=1) + bias_ref[h]
        pr = jnp.exp(sc - jnp.max(sc, axis=1, keepdims=True))
        denom.append(jnp.sum(pr, axis=1, keepdims=True))
        probs.append(pr.astype(BF))
    acc = []
    for h in heads:
        parts = [jnp.dot(probs[h][:, i * page:(i + 1) * page], head_rows(v_refs[i], h), preferred_element_type=F32)
                 for i in range(n_pages)]
        parts.append(jnp.dot(probs[h][:, n_pages * page:], new_rows(vn_ref, h), preferred_element_type=F32))
        acc.append(functools.reduce(jnp.add, parts))
    for h in heads:
        z = acc[h] / denom[h]
        o = z[:steps] - lam * z[steps:]
        o_ref[0, :, h * DIFF_E:(h + 1) * DIFF_E] = _rms(o, sw, SUBLN_EPS) * (1.0 - lam_init)


def _attn_sample(q, k_new, v_new, cache_k, cache_v, layer, page_table, rel_bias, lam, subln_w, lam_init):
    nb, n_pages = page_table.shape
    page = cache_k.shape[2]
    page_ids = page_table + layer * cache_k.shape[1]
    steps = q.shape[0] // nb
    past = n_pages * page
    width = DIFF_H * DIFF_E
    nrow = 2 * steps
    assert nrow % 8 == 0 and steps <= page
    q4 = jnp.transpose((q * (DIFF_DH ** -0.5)).reshape(nb, steps, DIFF_H, DIFF_E), (0, 2, 1, 3))
    first = jnp.arange(DIFF_E) < DIFF_DH
    q8 = jnp.concatenate([jnp.where(first, q4, 0.0), jnp.where(first, 0.0, q4)], axis=2).astype(BF)
    new_rows = lambda t: jnp.pad(jnp.transpose(t.reshape(nb, steps, DIFF_H, DIFF_E), (0, 2, 1, 3)),
                                 ((0, 0), (0, 0), (0, nrow - steps), (0, 0)))
    q_pos = past + jnp.arange(steps)
    key_pos = jnp.arange(past + steps)
    bias = _t5_bias(q_pos[None, :] - key_pos[:, None], rel_bias)
    bias = jnp.where((key_pos[:, None] <= q_pos[None, :])[..., None], bias, NEG)
    bias = jnp.pad(jnp.transpose(bias, (2, 1, 0)), ((0, 0), (0, 0), (0, page - steps)), constant_values=NEG)
    bias = jnp.concatenate([bias, bias], axis=1)
    rows = page * DIFF_H
    page_specs = [pl.BlockSpec((rows, DIFF_E), lambda b, pt, i=i: (pt[b, i], 0)) for i in range(n_pages)]
    head_spec = pl.BlockSpec((1, DIFF_H, nrow, DIFF_E), lambda b, pt: (b, 0, 0, 0))
    grid_spec = pltpu.PrefetchScalarGridSpec(
        num_scalar_prefetch=1, grid=(nb,),
        in_specs=[head_spec] + page_specs + page_specs + [
            head_spec, head_spec,
            pl.BlockSpec((DIFF_H, nrow, past + page), lambda b, pt: (0, 0, 0)),
            pl.BlockSpec((4, DIFF_DH), lambda b, pt: (0, 0)),
            pl.BlockSpec((1, DIFF_E), lambda b, pt: (0, 0))],
        out_specs=pl.BlockSpec((1, steps, width), lambda b, pt: (b, 0, 0)))
    ck = cache_k.reshape(-1, DIFF_E)
    cv = cache_v.reshape(-1, DIFF_E)
    o = pl.pallas_call(
        functools.partial(_attn_sample_body, n_pages=n_pages, lam_init=lam_init), grid_spec=grid_spec,
        out_shape=jax.ShapeDtypeStruct((nb, steps, width), F32), name="attn_sample",
        compiler_params=pltpu.CompilerParams(dimension_semantics=("parallel",), vmem_limit_bytes=VMEM_LIMIT),
    )(page_ids, q8, *([ck] * n_pages), *([cv] * n_pages), new_rows(k_new), new_rows(v_new), bias, lam,
      subln_w[None])
    return o.reshape(nb * steps, width)


def _attn_layer(x, u, dims, cache_k, cache_v, layer, page_table, p, lam_init, rel_bias):
    nb_p, seq_p, nb_s, seq_s = dims
    (q_p, q_s), (k_p, k_s), (v_p, v_s) = _linear(u, p['w_qkv'], 3, "attn_qkv")
    o_p = _attn_prompt(q_p, k_p, v_p, nb_p, seq_p, rel_bias, p['lam'], p['subln_w'], lam_init)
    o_s = _attn_sample(q_s, k_s, v_s, cache_k, cache_v, layer, page_table, rel_bias, p['lam'], p['subln_w'],
                       lam_init)
    x = _linear_res(x, (o_p, o_s), p['w_o'], "attn_out")
    shp_p = (nb_p, seq_p, DIFF_H, DIFF_E)
    shp_s = (nb_s, seq_s, DIFF_H, DIFF_E)
    return x, (k_p.reshape(shp_p), v_p.reshape(shp_p), k_s.reshape(shp_s), v_s.reshape(shp_s))


def _gdn_proj_body(is_prompt, tile, u_ref, wqkv_ref, wz_ref, wb_ref, wa_ref, al_ref, dt_ref, eb_ref,
                   qkv_o, z_o, beta_o, g_o):
    del is_prompt, tile
    u = u_ref[...].astype(BF)
    qkv_o[...] = jnp.dot(u, wqkv_ref[...], preferred_element_type=F32)
    z_o[...] = jnp.dot(u, wz_ref[...], preferred_element_type=F32)
    beta = _sigmoid(jnp.dot(u, wb_ref[...], preferred_element_type=F32))
    g = -jnp.exp(al_ref[...]) * _softplus(jnp.dot(u, wa_ref[...], preferred_element_type=F32) + dt_ref[...])
    eb = eb_ref[...]
    beta_o[...] = _dot_sel(beta, eb, 3)
    g_o[...] = _dot_sel(g, eb, 3)


def _gdn_post(conv, q_o, k_o, v_o):
    c = conv * _sigmoid(conv)
    hk = GDN_H * GDN_DK
    for h in range(GDN_H):
        qs = c[:, h * GDN_DK:(h + 1) * GDN_DK]
        ks = c[:, hk + h * GDN_DK:hk + (h + 1) * GDN_DK]
        q_o[:, h * GDN_DK:(h + 1) * GDN_DK] = qs * lax.rsqrt(jnp.sum(qs * qs, axis=1, keepdims=True) + 1e-6) * (GDN_DK ** -0.5)
        k_o[:, h * GDN_DK:(h + 1) * GDN_DK] = ks * lax.rsqrt(jnp.sum(ks * ks, axis=1, keepdims=True) + 1e-6)
    v_o[...] = c[:, 2 * hk:]


def _gdn_conv_prompt_body(x_ref, halo_ref, w_ref, q_o, k_o, v_o, *, tiles_per_seq):
    i = pl.program_id(0)
    x = x_ref[...]
    tm = x.shape[0]
    halo = jnp.where(i % tiles_per_seq == 0, 0.0, halo_ref[...])
    row8 = lax.broadcasted_iota(jnp.int32, (8, x.shape[1]), 0)
    w = w_ref[...]
    conv = w[GDN_CONV - 1:GDN_CONV] * x
    for j in range(GDN_CONV - 1):
        sh = GDN_CONV - 1 - j
        rolled = pltpu.roll(x, sh, 0)
        head = jnp.where(row8 < sh, pltpu.roll(halo, sh, 0), rolled[:8])
        tap = jnp.concatenate([head, rolled[8:]], axis=0) if tm > 8 else head
        conv = conv + w[j:j + 1] * tap
    _gdn_post(conv, q_o, k_o, v_o)


def _gdn_conv_taps_body(t0_ref, t1_ref, t2_ref, t3_ref, w_ref, q_o, k_o, v_o):
    w = w_ref[...]
    conv = w[0:1] * t0_ref[...] + w[1:2] * t1_ref[...] + w[2:3] * t2_ref[...] + w[3:4] * t3_ref[...]
    _gdn_post(conv, q_o, k_o, v_o)


def _gdn_chunk_body(q_ref, k_ref, v_ref, beta_ref, g_ref, o_ref, s_ref, st_sc, *, tb, groups):
    L, P = CHUNK, LANES
    t = pl.program_id(2)

    @pl.when(t == 0)
    def _():
        st_sc[...] = jnp.zeros(st_sc.shape, F32)

    ii, jj = _iota2((L, L))
    strict = ii > jj
    incl = ii >= jj

    gs = range(groups)

    def chunk(c, carry):
        rows = pl.ds(pl.multiple_of(c * L, L), L)
        cols = [slice(g * P, (g + 1) * P) for g in gs]
        st = [st_sc[g] for g in gs]
        gc = [_cumsum_rows(g_ref[rows, cols[g]]) for g in gs]
        q = [q_ref[rows, cols[g]] for g in gs]
        k = [k_ref[rows, cols[g]] for g in gs]
        kk = [_dot_nt(k[g], k[g]) for g in gs]
        qk = [_dot_nt(q[g], k[g]) for g in gs]
        n_mats, dec_i, e_gc, rhs = [], [], [], []
        for g in gs:
            diff = gc[g][:, :L] - gc[g].T[:L, :]
            beta = beta_ref[rows, cols[g]]
            n_mats.append(-(beta[:, :L] * kk[g] * jnp.exp(jnp.where(strict, diff, NEG))))
            dec_i.append(jnp.exp(jnp.where(incl, diff, NEG)))
            e_gc.append(jnp.exp(gc[g]))
            rhs.append(jnp.concatenate([beta * v_ref[rows, cols[g]], beta * e_gc[g] * k[g]], axis=1))
        t_inv = _tri_inv(n_mats, L)
        w = [_dot(t_inv[g], rhs[g]) for g in gs]
        uu = [w[g][:, :P] - _dot(w[g][:, P:], st[g]) for g in gs]
        qs = [_dot(q[g] * e_gc[g], st[g]) for g in gs]
        intra = [_dot(qk[g] * dec_i[g], uu[g]) for g in gs]
        upd = [_dot_tn(k[g] * jnp.exp(gc[g][L - 1:L, :] - gc[g]), uu[g]) for g in gs]
        for g in gs:
            o_ref[rows, cols[g]] = qs[g] + intra[g]
            st_sc[g] = jnp.exp(gc[g][L - 1:L, :]) * st[g] + upd[g]
        return carry

    lax.fori_loop(0, tb // L, chunk, 0)

    @pl.when(t == pl.num_programs(2) - 1)
    def _():
        for g in range(groups):
            s_ref[0, g] = st_sc[g]


def _gdn_step_body(q_ref, k_ref, v_ref, beta_ref, g_ref, s0_ref, o_ref, s_ref, *, bb, steps):
    ii, jj = _iota2((GDN_DK, GDN_DK))
    eye = (ii == jj).astype(F32)

    def one_row(b_i):
        s = s0_ref[b_i, 0]
        for t in range(steps):
            row = lambda ref: ref[b_i, pl.ds(t, 1), :]
            k_col = jnp.sum(eye * row(k_ref), axis=1, keepdims=True)
            q_col = jnp.sum(eye * row(q_ref), axis=1, keepdims=True)
            beta = row(beta_ref)
            ks = jnp.sum(k_col * s, axis=0, keepdims=True)
            s = jnp.exp(row(g_ref)) * (s - (beta * k_col) * ks) + (beta * k_col) * row(v_ref)
            o_ref[b_i, pl.ds(t, 1), :] = jnp.sum(q_col * s, axis=0, keepdims=True)
        s_ref[b_i, 0] = s

    together = math.gcd(bb, STEP_ROWS)

    def one(i, carry):
        for r in range(together):
            one_row(i * together + r)
        return carry

    lax.fori_loop(0, bb // together, one, 0)


def _gdn_out_body(is_prompt, tile, x_ref, o_ref, z_ref, nw_ref, wo_ref, y_ref):
    del is_prompt, tile
    z = z_ref[...]
    nw = nw_ref[...]
    o = o_ref[...]
    parts = []
    for h in range(GDN_H):
        sl = slice(h * GDN_DV, (h + 1) * GDN_DV)
        parts.append(_rms(o[:, sl], nw, NORM_EPS))
    on = jnp.concatenate(parts, axis=1) * (z * _sigmoid(z))
    y_ref[...] = x_ref[...] + _dot(on, wo_ref[...])


def _gdn_layer(x, u, dims, conv_s, state_s, p):
    nb_p, seq_p, nb_s, seq_s = dims
    n_p, n_s = nb_p * seq_p, nb_s * seq_s
    hv = GDN_H * GDN_DV
    w_in = p['w_in']
    pad = lambda w: jnp.pad(w, ((0, 0), (0, LANES - w.shape[-1])))
    eb = (jnp.arange(LANES)[:, None] == jnp.arange(hv)[None, :] // GDN_DV).astype(BF)
    (qkv_p, qkv_s), z, (beta_p, beta_s), (g_p, g_s) = _rows_call(
        _gdn_proj_body, [u],
        [w_in[:, :GDN_QKV].astype(BF), w_in[:, GDN_QKV:GDN_QKV + hv].astype(BF),
         pad(w_in[:, GDN_QKV + hv:GDN_QKV + hv + GDN_H]).astype(BF), pad(w_in[:, GDN_QKV + hv + GDN_H:]).astype(BF),
         pad(p['a_log'][None]), pad(p['dt_bias'][None]), eb],
        [(GDN_QKV, F32), (hv, F32), (hv, F32), (hv, F32)], 256, "gdn_proj")
    tm = _row_tile(seq_p, 256)
    cw = p['conv_w']
    outs3 = [jax.ShapeDtypeStruct((n_p, hv), F32)] * 3
    q_p, k_p, v_p = pl.pallas_call(
        functools.partial(_gdn_conv_prompt_body, tiles_per_seq=seq_p // tm), grid=(n_p // tm,),
        in_specs=[pl.BlockSpec((tm, GDN_QKV), lambda i: (i, 0)),
                  pl.BlockSpec((8, GDN_QKV), lambda i: (jnp.maximum(i * (tm // 8) - 1, 0), 0)),
                  pl.BlockSpec((GDN_CONV, GDN_QKV), lambda i: (0, 0))],
        out_specs=[pl.BlockSpec((tm, hv), lambda i: (i, 0))] * 3, out_shape=outs3, name="gdn_conv_prompt",
        compiler_params=pltpu.CompilerParams(dimension_semantics=("parallel",), vmem_limit_bytes=VMEM_LIMIT),
    )(qkv_p, qkv_p, cw)
    xp_s = jnp.concatenate([conv_s, qkv_s.reshape(nb_s, seq_s, GDN_QKV)], axis=1)
    taps = [xp_s[:, j:j + seq_s].reshape(n_s, GDN_QKV) for j in range(GDN_CONV)]
    ts = _row_tile(n_s, 256)
    q_s, k_s, v_s = pl.pallas_call(
        _gdn_conv_taps_body, grid=(n_s // ts,),
        in_specs=[pl.BlockSpec((ts, GDN_QKV), lambda i: (i, 0))] * GDN_CONV
        + [pl.BlockSpec((GDN_CONV, GDN_QKV), lambda i: (0, 0))],
        out_specs=[pl.BlockSpec((ts, hv), lambda i: (i, 0))] * 3,
        out_shape=[jax.ShapeDtypeStruct((n_s, hv), F32)] * 3, name="gdn_conv_sample",
        compiler_params=pltpu.CompilerParams(dimension_semantics=("parallel",), vmem_limit_bytes=VMEM_LIMIT),
    )(*taps, cw)
    groups = GDN_GROUPS
    tb = _row_tile(seq_p, SEQ_BLOCK)
    nt = seq_p // tb
    spec = pl.BlockSpec((tb, groups * LANES), lambda b, h, t: (b * nt + t, h))
    o_p, s_p = pl.pallas_call(
        functools.partial(_gdn_chunk_body, tb=tb, groups=groups), grid=(nb_p, GDN_H // groups, nt),
        in_specs=[spec] * 5,
        out_specs=[spec, pl.BlockSpec((1, groups, GDN_DK, GDN_DV), lambda b, h, t: (b, h, 0, 0))],
        out_shape=[jax.ShapeDtypeStruct((n_p, hv), F32), jax.ShapeDtypeStruct((nb_p, GDN_H, GDN_DK, GDN_DV), F32)],
        scratch_shapes=[pltpu.VMEM((groups, GDN_DK, GDN_DV), F32)],
        name="gdn_chunk",
        compiler_params=pltpu.CompilerParams(dimension_semantics=("parallel", "parallel", "arbitrary"),
                                             vmem_limit_bytes=VMEM_LIMIT),
    )(q_p, k_p, v_p, beta_p, g_p)
    bb = _row_tile(nb_s, 8)
    sspec = pl.BlockSpec((bb, seq_s, LANES), lambda b, h: (b, 0, h))
    stspec = pl.BlockSpec((bb, 1, GDN_DK, GDN_DV), lambda b, h: (b, h, 0, 0))
    r3 = lambda t: t.reshape(nb_s, seq_s, hv)
    o_s, s_s = pl.pallas_call(
        functools.partial(_gdn_step_body, bb=bb, steps=seq_s), grid=(nb_s // bb, GDN_H),
        in_specs=[sspec] * 5 + [stspec], out_specs=[sspec, stspec],
        out_shape=[jax.ShapeDtypeStruct((nb_s, seq_s, hv), F32),
                   jax.ShapeDtypeStruct((nb_s, GDN_H, GDN_DK, GDN_DV), F32)],
        name="gdn_step",
        compiler_params=pltpu.CompilerParams(dimension_semantics=("parallel", "parallel"),
                                             vmem_limit_bytes=VMEM_LIMIT),
    )(r3(q_s), r3(k_s), r3(v_s), r3(beta_s), r3(g_s), state_s)
    x = _rows_call(_gdn_out_body, [x, (o_p, o_s.reshape(n_s, hv)), z], [p['norm_w'][None], p['w_o'].astype(BF)],
                   [(D_MODEL, F32)], 512, "gdn_out")[0]
    keep = GDN_CONV - 1
    tail_p = qkv_p.reshape(nb_p, seq_p, GDN_QKV)[:, -keep:]
    conv_p = jnp.pad(tail_p, ((0, 0), (max(keep - seq_p, 0), 0), (0, 0)))
    return x, (s_p, conv_p, s_s, xp_s[:, -keep:])


def kernel(x_prompt, x_sample, state_rwkv_wkv, state_rwkv_shift, cache_attn_k, cache_attn_v, state_gdn, state_gdn_conv, page_table, norm_w, final_norm_w, ffn_w_gate, ffn_w_up, ffn_w_down, rwkv_mu, rwkv_w_rkv, rwkv_w_o, rwkv_w0, rwkv_w1, rwkv_w2, rwkv_a0, rwkv_a1, rwkv_a2, rwkv_g1, rwkv_g2, rwkv_k_k, rwkv_k_a, rwkv_r_k, rwkv_lnx_w, rwkv_lnx_b, rwkv_v0, rwkv_v1, rwkv_v2, attn_w_qkv, attn_w_o, attn_lambda, attn_subln_w, rel_bias, gdn_w_in, gdn_conv_w, gdn_a_log, gdn_dt_bias, gdn_norm_w, gdn_w_o):
    nb_p, seq_p, _ = x_prompt.shape
    nb_s, seq_s, _ = x_sample.shape
    dims = (nb_p, seq_p, nb_s, seq_s)
    depth = norm_w.shape[0]
    x = (x_prompt.reshape(nb_p * seq_p, D_MODEL), x_sample.reshape(nb_s * seq_s, D_MODEL))
    v_first = None
    rw, at, gd = [], [], []
    for i in range(depth):
        kind, j = i % 3, i // 3
        x, u = _ffn(x, norm_w[i, 0], ffn_w_gate[i, 0], ffn_w_up[i, 0], ffn_w_down[i, 0], norm_w[i, 1])
        if kind == 0:
            p = dict(mu=rwkv_mu[j], w_rkv=rwkv_w_rkv[j], w_o=rwkv_w_o[j], w0=rwkv_w0[j], w1=rwkv_w1[j],
                     w2=rwkv_w2[j], a0=rwkv_a0[j], a1=rwkv_a1[j], a2=rwkv_a2[j], g1=rwkv_g1[j], g2=rwkv_g2[j],
                     k_k=rwkv_k_k[j], k_a=rwkv_k_a[j], r_k=rwkv_r_k[j], lnx_w=rwkv_lnx_w[j], lnx_b=rwkv_lnx_b[j])
            vres = None if j == 0 else (rwkv_v0[j - 1], rwkv_v1[j - 1], rwkv_v2[j - 1])
            x, v_first, st = _rwkv_layer(x, u, dims, state_rwkv_shift[j], state_rwkv_wkv[j], p, v_first, vres)
            rw.append(st)
        elif kind == 1:
            p = dict(w_qkv=attn_w_qkv[j], w_o=attn_w_o[j], lam=attn_lambda[j], subln_w=attn_subln_w[j])
            lam_init = 0.8 - 0.6 * math.exp(-0.3 * i)
            x, st = _attn_layer(x, u, dims, cache_attn_k, cache_attn_v, j, page_table, p, lam_init, rel_bias)
            at.append(st)
        else:
            p = dict(w_in=gdn_w_in[j], conv_w=gdn_conv_w[j], a_log=gdn_a_log[j], dt_bias=gdn_dt_bias[j],
                     norm_w=gdn_norm_w[j], w_o=gdn_w_o[j])
            x, st = _gdn_layer(x, u, dims, state_gdn_conv[j], state_gdn[j], p)
            gd.append(st)
        g_next = final_norm_w if i == depth - 1 else jnp.ones((D_MODEL,), F32)
        x, y = _ffn(x, norm_w[i, 2], ffn_w_gate[i, 1], ffn_w_up[i, 1], ffn_w_down[i, 1], g_next)
    stack = lambda lst, k: jnp.stack([t[k] for t in lst])
    return (y[0].reshape(x_prompt.shape), y[1].reshape(x_sample.shape),
            stack(rw, 0), stack(rw, 1), stack(at, 0), stack(at, 1), stack(gd, 0), stack(gd, 1),
            stack(rw, 2), stack(rw, 3), stack(at, 2), stack(at, 3), stack(gd, 2), stack(gd, 3))
```

```python
import functools
import math

import jax
import jax.numpy as jnp
from jax import lax
from jax.experimental import pallas as pl
from jax.experimental.pallas import tpu as pltpu

F32 = jnp.float32
BF = jnp.bfloat16

D_MODEL = 1024
NORM_EPS = 1e-6
RWKV_N = 64
RWKV_H = D_MODEL // RWKV_N
RWKV_LNX_EPS = 64e-5
DIFF_H = 8
DIFF_DH = 64
DIFF_E = 128
SUBLN_EPS = 1e-5
REL_BUCKETS = 32
REL_MAX_EXACT = 16
REL_MAX_DIST = 128
GDN_H = 8
GDN_DK = 128
GDN_DV = 128
GDN_CONV = 4
GDN_QKV = GDN_H * (2 * GDN_DK + GDN_DV)
FFN_TILE = 1024
FFN_CHUNK_BIG_TILE = 256
FFN_CHUNK_SMALL_TILE = 1408
CHUNK = 64
SEQ_BLOCK = 512
RWKV_GROUPS = 8
GDN_GROUPS = 8
LANES = 128
NEG = -1e30
LOG2E = math.log2(math.e)
VMEM_LIMIT = 56 * 1024 * 1024


def _dot(a, b):
    return jnp.dot(a.astype(BF), b.astype(BF), preferred_element_type=F32)


def _dot_nt(a, b):
    return lax.dot_general(a.astype(BF), b.astype(BF), (((1,), (1,)), ((), ())), preferred_element_type=F32)


def _dot_tn(a, b):
    return lax.dot_general(a.astype(BF), b.astype(BF), (((0,), (0,)), ((), ())), preferred_element_type=F32)


def _split(x, n):
    out = []
    for _ in range(n):
        h = x.astype(BF)
        out.append(h)
        x = x - h.astype(F32)
    return out


def _dot_sel(x, sel, n=2):
    return sum(jnp.dot(t, sel, preferred_element_type=F32) for t in _split(x, n))


def _rms(x, g, eps):
    return x * lax.rsqrt(jnp.mean(x * x, axis=-1, keepdims=True) + eps) * g


def _sigmoid(x):
    return 1.0 / (1.0 + jnp.exp(-x))


def _softplus(x):
    return jnp.maximum(x, 0.0) + jnp.log(1.0 + jnp.exp(-jnp.abs(x)))


def _iota2(shape):
    return lax.broadcasted_iota(jnp.int32, shape, 0), lax.broadcasted_iota(jnp.int32, shape, 1)


def _cumsum_rows(x):
    n = x.shape[0]
    row = lax.broadcasted_iota(jnp.int32, x.shape, 0)
    shift = 1
    while shift < n:
        x = x + jnp.where(row >= shift, pltpu.roll(x, shift, 0), 0.0)
        shift *= 2
    return x


def _tri_inv(n_mats, size):
    ii, jj = _iota2((size, size))
    eye = (ii == jj).astype(F32)
    xs = [eye + jnp.where((ii >> 1) == (jj >> 1), n, 0.0) for n in n_mats]
    lvl = 1
    while (2 << lvl) <= size:
        sel = ((ii >> (lvl + 1)) == (jj >> (lvl + 1))) & (((ii >> lvl) & 1) == 1) & (((jj >> lvl) & 1) == 0)
        ys = [_dot(x, jnp.where(sel, n, 0.0)) for x, n in zip(xs, n_mats)]
        xs = [x + _dot(y, x) for x, y in zip(xs, ys)]
        lvl += 1
    return xs


def _row_tile(n, want):
    tm = min(want, n)
    while n % tm:
        tm //= 2
    return tm


HALO = 8
SAMPLE_TILE = 128


def _rows_call(body, rows, consts, outs, tm, name, halo=()):
    n_p, n_s = rows[0][0].shape[0], rows[0][1].shape[0]
    tm_p, tm_s = _row_tile(n_p, tm), _row_tile(n_s, min(tm, SAMPLE_TILE))
    tp, ts = n_p // tm_p, n_s // tm_s
    p_idx = lambda i: jnp.minimum(i, tp - 1)
    s_idx = lambda i: jnp.maximum(i - tp, 0)
    in_specs, args = [], []
    for k, (a_p, a_s) in enumerate(rows):
        if k in halo:
            in_specs.append(pl.BlockSpec((HALO, a_p.shape[1]),
                                         lambda i: (jnp.maximum(p_idx(i) * (tm_p // HALO) - 1, 0), 0)))
        else:
            in_specs.append(pl.BlockSpec((tm_p, a_p.shape[1]), lambda i: (p_idx(i), 0)))
        in_specs.append(pl.BlockSpec((tm_s, a_s.shape[1]), lambda i: (s_idx(i), 0)))
        args += [a_p, a_s]
    for a in consts:
        in_specs.append(pl.BlockSpec(a.shape, lambda i, nd=a.ndim: (0,) * nd, pipeline_mode=pl.Buffered(1)))
    out_specs, out_shape = [], []
    for c, dt in outs:
        out_specs += [pl.BlockSpec((tm_p, c), lambda i: (p_idx(i), 0)), pl.BlockSpec((tm_s, c), lambda i: (s_idx(i), 0))]
        out_shape += [jax.ShapeDtypeStruct((n_p, c), dt), jax.ShapeDtypeStruct((n_s, c), dt)]
    nr, nc = len(rows), len(consts)

    def kern(*refs):
        i = pl.program_id(0)
        cs = refs[2 * nr:2 * nr + nc]
        o = refs[2 * nr + nc:]

        @pl.when(i < tp)
        def _():
            body(True, i, *refs[0:2 * nr:2], *cs, *o[0::2])

        @pl.when(i >= tp)
        def _():
            body(False, i - tp, *refs[1:2 * nr:2], *cs, *o[1::2])

    res = pl.pallas_call(
        kern, grid=(tp + ts,), in_specs=in_specs, out_specs=out_specs, out_shape=out_shape, name=name,
        compiler_params=pltpu.CompilerParams(dimension_semantics=("arbitrary",), vmem_limit_bytes=VMEM_LIMIT),
    )(*args, *consts)
    return [(res[2 * k], res[2 * k + 1]) for k in range(len(outs))]


def _ffn_body(is_prompt, tile, x_ref, g_ref, wg_ref, wu_ref, wd_ref, g2_ref, o_ref, u_ref):
    del is_prompt, tile
    x = x_ref[...]
    h = _rms(x, g_ref[...], NORM_EPS).astype(BF)
    hidden = wg_ref.shape[1]
    want = FFN_CHUNK_BIG_TILE if x.shape[0] >= FFN_TILE else FFN_CHUNK_SMALL_TILE
    nchunk = max(n for n in range(1, hidden // LANES + 1) if hidden % (n * LANES) == 0 and hidden // n >= want)
    fc = hidden // nchunk
    acc = jnp.zeros_like(x)
    for c in range(nchunk):
        sl = slice(c * fc, (c + 1) * fc)
        gate = jnp.dot(h, wg_ref[:, sl], preferred_element_type=F32)
        up = jnp.dot(h, wu_ref[:, sl], preferred_element_type=F32)
        act = (gate * _sigmoid(gate) * up).astype(BF)
        acc = acc + jnp.dot(act, wd_ref[sl, :], preferred_element_type=F32)
    y = x + 0.5 * acc
    o_ref[...] = y
    u_ref[...] = _rms(y, g2_ref[...], NORM_EPS)


def _ffn(x, g, wg, wu, wd, g_next):
    return _rows_call(_ffn_body, [x], [g[None], wg.astype(BF), wu.astype(BF), wd.astype(BF), g_next[None]],
                      [(D_MODEL, F32), (D_MODEL, F32)], FFN_TILE, "ffn_half")


def _linear_body(is_prompt, tile, *refs, n_out, residual):
    del is_prompt, tile
    if residual:
        x_ref, u_ref, w_ref = refs[:3]
        outs = refs[3:]
    else:
        u_ref, w_ref = refs[:2]
        outs = refs[2:]
    y = jnp.dot(u_ref[...].astype(BF), w_ref[...], preferred_element_type=F32)
    if residual:
        outs[0][...] = x_ref[...] + y
    else:
        c = y.shape[1] // n_out
        for k, o in enumerate(outs):
            o[...] = y[:, k * c:(k + 1) * c]


def _linear(u, w, n_out, name):
    body = functools.partial(_linear_body, n_out=n_out, residual=False)
    c = w.shape[1] // n_out
    return _rows_call(body, [u], [w.astype(BF)], [(c, F32)] * n_out, 512, name)


def _linear_res(x, u, w, name):
    body = functools.partial(_linear_body, n_out=1, residual=True)
    return _rows_call(body, [x, u], [w.astype(BF)], [(w.shape[1], F32)], 512, name)[0]


def _head_sel(width, heads):
    e = (jnp.arange(width)[:, None] // (width // heads) == jnp.arange(heads)[None, :]).astype(BF)
    return e, e.T


def _rwkv_proj_body(is_prompt, tile, *refs, has_vres, tiles_per_seq):
    (u_ref, prev_ref) = refs[:2]
    k0 = 2
    if has_vres:
        vf_ref = refs[2]
        k0 = 3
    (mu_ref, wr_ref, wk_ref, wv_ref, w0_ref, w1_ref, w2_ref, a0_ref, a1_ref, a2_ref, g1_ref, g2_ref,
     kk_ref, ka_ref, rk_ref, e_ref, et_ref) = refs[k0:k0 + 17]
    k1 = k0 + 17
    if has_vres:
        v0_ref, v1_ref, v2_ref = refs[k1:k1 + 3]
        k1 += 3
    r_o, lw_o, k_o, v_o, a_o, b_o, g_o, bon_o = refs[k1:]
    u = u_ref[...]
    if is_prompt:
        before = jnp.where(tile % tiles_per_seq == 0, 0.0, prev_ref[HALO - 1:HALO, :])
        row = lax.broadcasted_iota(jnp.int32, u.shape, 0)
        prev = jnp.where(row == 0, before, pltpu.roll(u, 1, 0))
    else:
        prev = prev_ref[...]
    xx = prev - u
    mu = mu_ref[...]
    xr, xw, xk = u + xx * mu[0:1], u + xx * mu[1:2], u + xx * mu[2:3]
    xv, xa, xg = u + xx * mu[3:4], u + xx * mu[4:5], u + xx * mu[5:6]
    r = _dot(xr, wr_ref[...])
    k = _dot(xk, wk_ref[...])
    v = _dot(xv, wv_ref[...])
    wlog = -_softplus(-(w0_ref[...] + _dot(jnp.tanh(_dot(xw, w1_ref[...])), w2_ref[...]))) - 0.5
    lw_o[...] = -jnp.exp(wlog)
    if has_vres:
        v = v + (vf_ref[...] - v) * _sigmoid(v0_ref[...] + _dot(_dot(xv, v1_ref[...]), v2_ref[...]))
    a = _sigmoid(a0_ref[...] + _dot(_dot(xa, a1_ref[...]), a2_ref[...]))
    g_o[...] = _dot(_sigmoid(_dot(xg, g1_ref[...])), g2_ref[...])
    e, et = e_ref[...], et_ref[...]
    kk = k * kk_ref[...]
    kk = kk * lax.rsqrt(_dot_sel(_dot_sel(kk * kk, e), et) + 1e-24)
    k_mod = k * (1.0 + (a - 1.0) * ka_ref[...])
    r_o[...] = r
    k_o[...] = k_mod
    v_o[...] = v
    a_o[...] = -kk
    b_o[...] = kk * a
    bon_o[...] = _dot_sel(_dot_sel(r * k_mod * rk_ref[...], e), et) * v


def _rwkv_advance(L, refs, rows, cols, st):
    r_ref, lw_ref, k_ref, v_ref, a_ref, b_ref = refs
    P, N = LANES, RWKV_N
    ii, jj = _iota2((L, L))
    mask2 = jnp.concatenate([(ii > jj).astype(F32), (ii >= jj).astype(F32)], axis=0)
    lane = lax.broadcasted_iota(jnp.int32, (L, P), 1)
    lane2 = lax.broadcasted_iota(jnp.int32, (2 * L, P), 1)
    bi, bj = _iota2((P, P))
    bd = ((bi >= N) == (bj >= N)).astype(F32)
    us = range(len(st))
    chains = [(u, h) for u in us for h in range(2)]
    head_lanes = [(lane >= N) == bool(h) for h in range(2)]
    head_lanes2 = [(lane2 >= N) == bool(h) for h in range(2)]
    x, bk, v, bk_end, w_end = [], [], [], [], []
    for u in us:
        at = lambda ref: ref[rows[u], cols[u]]
        lw = at(lw_ref)
        gc = _cumsum_rows(lw)
        gl = gc[L - 1:L, :]
        r, k, a, b = at(r_ref), at(k_ref), at(a_ref), at(b_ref)
        e_neg = jnp.exp(-gc)
        e_end = jnp.exp(gl - gc)
        x.append(jnp.concatenate([a * jnp.exp(gc - lw), r * jnp.exp(gc)], axis=0).astype(BF))
        bk.append(jnp.concatenate([b * e_neg, k * e_neg], axis=0).astype(BF))
        v.append(at(v_ref))
        bk_end.append(jnp.concatenate([b * e_end, k * e_end], axis=0))
        w_end.append(jnp.exp(gl))
    xs = [_dot_nt(x[u], st[u]) for u in us]
    xh = {(u, h): jnp.where(head_lanes2[h], x[u], jnp.zeros_like(x[u])) for u, h in chains}
    abk = {ch: _dot_nt(xh[ch], bk[ch[0]]) for ch in chains}
    ab = {ch: abk[ch][:, :L] * mask2 for ch in chains}
    ak = {ch: abk[ch][:, L:] * mask2 for ch in chains}
    t_inv = dict(zip(chains, _tri_inv([ab[ch][:L] for ch in chains], L)))
    vh = {(u, h): jnp.where(head_lanes[h], v[u], 0.0) for u, h in chains}
    akv = {ch: _dot(ak[ch], vh[ch]) for ch in chains}
    rhs = [xs[u][:L] + akv[(u, 0)][:L] + akv[(u, 1)][:L] for u in us]
    uu_h = {(u, h): _dot(t_inv[(u, h)], jnp.where(head_lanes[h], rhs[u], 0.0)) for u, h in chains}
    uu = [uu_h[(u, 0)] + uu_h[(u, 1)] for u in us]
    o_h = {(u, h): _dot(ab[(u, h)][L:], jnp.where(head_lanes[h], uu[u], 0.0)) for u, h in chains}
    upd = [_dot_tn(jnp.concatenate([uu[u], v[u]], axis=0), bk_end[u]) for u in us]
    outs = [xs[u][L:] + akv[(u, 0)][L:] + akv[(u, 1)][L:] + o_h[(u, 0)] + o_h[(u, 1)] for u in us]
    return outs, [st[u] * w_end[u] + bd * upd[u] for u in us]


def _rwkv_chunk_body(r_ref, lw_ref, k_ref, v_ref, a_ref, b_ref, o_ref, s_ref, st_sc, *, tb, groups):
    L, P, N = CHUNK, LANES, RWKV_N
    t = pl.program_id(2)

    @pl.when(t == 0)
    def _():
        st_sc[...] = jnp.zeros(st_sc.shape, F32)

    refs = (r_ref, lw_ref, k_ref, v_ref, a_ref, b_ref)
    cols = [slice(g * P, (g + 1) * P) for g in range(groups)]

    def chunk(c, carry):
        rows = pl.ds(pl.multiple_of(c * L, L), L)
        outs, new = _rwkv_advance(L, refs, [rows] * groups, cols, [st_sc[g] for g in range(groups)])
        for g in range(groups):
            o_ref[rows, cols[g]] = outs[g]
            st_sc[g] = new[g]
        return carry

    lax.fori_loop(0, tb // L, chunk, 0, unroll=2)

    @pl.when(t == pl.num_programs(2) - 1)
    def _():
        for g in range(groups):
            s = st_sc[g]
            s_ref[0, 2 * g] = s[:N, :N]
            s_ref[0, 2 * g + 1] = s[N:, N:]


def _rwkv_chunk(seqs, nb, seq):
    groups = RWKV_GROUPS
    width = groups * LANES
    tb = _row_tile(seq, SEQ_BLOCK)
    nt = seq // tb
    spec = pl.BlockSpec((tb, width), lambda b, h, t: (b * nt + t, h))
    return pl.pallas_call(
        functools.partial(_rwkv_chunk_body, tb=tb, groups=groups), grid=(nb, D_MODEL // width, nt),
        in_specs=[spec] * 6,
        out_specs=[spec, pl.BlockSpec((1, 2 * groups, RWKV_N, RWKV_N), lambda b, h, t: (b, h, 0, 0))],
        out_shape=[jax.ShapeDtypeStruct((nb * seq, D_MODEL), F32),
                   jax.ShapeDtypeStruct((nb, RWKV_H, RWKV_N, RWKV_N), F32)],
        scratch_shapes=[pltpu.VMEM((groups, LANES, LANES), F32)],
        name="rwkv_chunk",
        compiler_params=pltpu.CompilerParams(dimension_semantics=("parallel", "parallel", "arbitrary"),
                                             vmem_limit_bytes=VMEM_LIMIT),
    )(*seqs)


def _rwkv_short_body(r_ref, lw_ref, k_ref, v_ref, a_ref, b_ref, s0_ref, o_ref, s_ref, *, bb, length):
    P, N = LANES, RWKV_N
    groups = D_MODEL // P
    units = [(b_i, g) for b_i in range(bb) for g in range(groups)]
    zero = jnp.zeros((N, N), F32)
    st = [jnp.concatenate([jnp.concatenate([s0_ref[b_i, 2 * g], zero], axis=1),
                           jnp.concatenate([zero, s0_ref[b_i, 2 * g + 1]], axis=1)], axis=0) for b_i, g in units]
    rows = [slice(b_i * length, (b_i + 1) * length) for b_i, _ in units]
    cols = [slice(g * P, (g + 1) * P) for _, g in units]
    outs, new = _rwkv_advance(length, (r_ref, lw_ref, k_ref, v_ref, a_ref, b_ref), rows, cols, st)
    for u, (b_i, g) in enumerate(units):
        o_ref[rows[u], cols[u]] = outs[u]
        s_ref[b_i, 2 * g] = new[u][:N, :N]
        s_ref[b_i, 2 * g + 1] = new[u][N:, N:]


SHORT_LEN = 8
SHORT_ROWS = 4


def _rwkv_short(seqs, s0, nb, steps):
    length = SHORT_LEN * pl.cdiv(steps, SHORT_LEN)
    pad = lambda t: jnp.pad(t.reshape(nb, steps, D_MODEL), ((0, 0), (0, length - steps), (0, 0))).reshape(nb * length, D_MODEL)
    bb = _row_tile(nb, SHORT_ROWS)
    spec = pl.BlockSpec((bb * length, D_MODEL), lambda b: (b, 0))
    sspec = pl.BlockSpec((bb, RWKV_H, RWKV_N, RWKV_N), lambda b: (b, 0, 0, 0))
    o, s = pl.pallas_call(
        functools.partial(_rwkv_short_body, bb=bb, length=length), grid=(nb // bb,),
        in_specs=[spec] * 6 + [sspec], out_specs=[spec, sspec],
        out_shape=[jax.ShapeDtypeStruct((nb * length, D_MODEL), F32),
                   jax.ShapeDtypeStruct((nb, RWKV_H, RWKV_N, RWKV_N), F32)],
        name="rwkv_short",
        compiler_params=pltpu.CompilerParams(dimension_semantics=("parallel",), vmem_limit_bytes=VMEM_LIMIT),
    )(*[pad(t) for t in seqs], s0)
    return o.reshape(nb, length, D_MODEL)[:, :steps].reshape(nb * steps, D_MODEL), s


def _rwkv_out_body(is_prompt, tile, x_ref, o_ref, g_ref, bon_ref, lw_ref, lb_ref, e_ref, et_ref, wo_ref, y_ref):
    del is_prompt, tile
    e, et = e_ref[...], et_ref[...]
    o = o_ref[...]
    inv_n = 1.0 / RWKV_N
    d = o - _dot_sel(_dot_sel(o, e), et) * inv_n
    var = _dot_sel(_dot_sel(d * d, e), et) * inv_n
    on = d * lax.rsqrt(var + RWKV_LNX_EPS) * lw_ref[...] + lb_ref[...]
    y_ref[...] = x_ref[...] + _dot((on + bon_ref[...]) * g_ref[...], wo_ref[...])


def _rwkv_layer(x, u, dims, shift_s, wkv_s, p, v_first, vres):
    nb_p, seq_p, nb_s, seq_s = dims
    u_p, u_s = u
    u3s = u_s.reshape(nb_s, seq_s, D_MODEL)
    prev_s = jnp.concatenate([shift_s[:, None, :], u3s[:, :-1]], axis=1).reshape(nb_s * seq_s, D_MODEL)
    e, et = _head_sel(D_MODEL, RWKV_H)
    has_vres = vres is not None
    tm = _row_tile(seq_p, 256)
    rows = [u, (u_p, prev_s)] + ([v_first] if has_vres else [])
    consts = [p['mu'], p['w_rkv'][0].astype(BF), p['w_rkv'][1].astype(BF), p['w_rkv'][2].astype(BF),
              p['w0'][None], p['w1'].astype(BF), p['w2'].astype(BF), p['a0'][None], p['a1'].astype(BF),
              p['a2'].astype(BF), p['g1'].astype(BF), p['g2'].astype(BF), p['k_k'][None], p['k_a'][None],
              p['r_k'].reshape(1, D_MODEL), e, et]
    if has_vres:
        consts += [vres[0][None], vres[1].astype(BF), vres[2].astype(BF)]
    body = functools.partial(_rwkv_proj_body, has_vres=has_vres, tiles_per_seq=seq_p // tm)
    r, lw, k, v, a, b, gate, bonus = _rows_call(body, rows, consts, [(D_MODEL, F32)] * 8, tm, "rwkv_proj", halo=(1,))
    seqs = (r, lw, k, v, a, b)
    o_p, s_p = _rwkv_chunk([t[0] for t in seqs], nb_p, seq_p)
    o_s, s_s = _rwkv_short([t[1] for t in seqs], wkv_s, nb_s, seq_s)
    x = _rows_call(_rwkv_out_body, [x, (o_p, o_s), gate, bonus],
                   [p['lnx_w'][None], p['lnx_b'][None], e, et, p['w_o'].astype(BF)], [(D_MODEL, F32)], 512,
                   "rwkv_out")[0]
    u3p = u_p.reshape(nb_p, seq_p, D_MODEL)
    return x, (v if not has_vres else v_first), (s_p, u3p[:, -1], s_s, u3s[:, -1])


def _t5_bias(dist, rel_bias):
    n = jnp.maximum(dist, 0)
    nf = jnp.maximum(n, 1).astype(F32)
    large = REL_MAX_EXACT + (jnp.log(nf / REL_MAX_EXACT) / math.log(REL_MAX_DIST / REL_MAX_EXACT)
                             * (REL_BUCKETS - REL_MAX_EXACT)).astype(jnp.int32)
    large = jnp.minimum(large, REL_BUCKETS - 1)
    bucket = jnp.where(n < REL_MAX_EXACT, n, large)
    onehot = (bucket[..., None] == jnp.arange(REL_BUCKETS)).astype(F32)
    return jnp.einsum('...b,bh->...h', onehot, rel_bias.astype(F32), precision=lax.Precision.HIGHEST)


def _lam_of(lam_ref, lam_init):
    lv = lam_ref[...]
    return (jnp.exp(jnp.sum(lv[0:1] * lv[1:2], axis=1, keepdims=True))
            - jnp.exp(jnp.sum(lv[2:3] * lv[3:4], axis=1, keepdims=True)) + lam_init)


ATTN_HEADS = 2


def _attn_prompt_body(q_ref, k_ref, v_ref, bias_ref, far_ref, lam_ref, sw_ref, o_ref, kb_sc, vt_sc, *, tq, lam_init):
    qi = pl.program_id(2)
    E, dh = DIFF_E, DIFF_DH
    nkv = kb_sc.shape[0]
    chains = [(hh, c) for hh in range(ATTN_HEADS) for c in range(2)]

    @pl.when(qi == 0)
    def _():
        for j in range(nkv):
            kb_sc[j] = k_ref[j * tq:(j + 1) * tq, :].astype(BF)
            vt_sc[j] = v_ref[j * tq:(j + 1) * tq, :].T.astype(BF)

    q_t = (q_ref[...] * (dh ** -0.5 * LOG2E)).T
    row = lax.broadcasted_iota(jnp.int32, q_t.shape, 0)
    q_c = {(hh, c): jnp.where((row >= hh * E + c * dh) & (row < hh * E + (c + 1) * dh), q_t, 0.0).astype(BF)
           for hh, c in chains}
    kk, qq = _iota2((tq, tq))

    def step(carry, kj, bias=None, shift=None):
        kb = kb_sc[kj]
        s = {ch: jnp.dot(kb, q_c[ch], preferred_element_type=F32) for ch in chains}
        if bias is not None:
            s = {ch: s[ch] + bias[ch[0]] for ch in chains}
        top = {ch: jnp.max(s[ch], axis=0, keepdims=True) for ch in chains}
        if shift is not None:
            top = {ch: top[ch] + shift[ch[0]] for ch in chains}
        m_new = {ch: jnp.maximum(carry[ch][0], top[ch]) for ch in chains}
        alpha = {ch: jnp.exp2(carry[ch][0] - m_new[ch]) for ch in chains}
        offs = m_new if shift is None else {ch: m_new[ch] - shift[ch[0]] for ch in chains}
        pr = {ch: jnp.exp2(s[ch] - offs[ch]) for ch in chains}
        l_new = {ch: carry[ch][1] * alpha[ch] + jnp.sum(pr[ch], axis=0, keepdims=True) for ch in chains}
        vt = vt_sc[kj]
        pv = {(hh, c): jnp.dot(vt[hh * E:(hh + 1) * E, :], pr[(hh, c)].astype(BF), preferred_element_type=F32)
              for hh, c in chains}
        return {ch: (m_new[ch], l_new[ch], carry[ch][2] * alpha[ch] + pv[ch]) for ch in chains}

    init = {ch: (jnp.full((1, tq), NEG, F32), jnp.zeros((1, tq), F32), jnp.zeros((E, tq), F32)) for ch in chains}
    carry = step(init, qi, bias=[jnp.where(kk <= qq, bias_ref[hh, 0], NEG) for hh in range(ATTN_HEADS)])
    carry = step(carry, jnp.maximum(qi - 1, 0),
                 bias=[jnp.where(qi >= 1, bias_ref[hh, 1], NEG) for hh in range(ATTN_HEADS)])
    far = [far_ref[hh, 0:1, :] for hh in range(ATTN_HEADS)]
    carry = lax.fori_loop(0, jnp.maximum(qi - 1, 0), lambda kj, c: step(c, kj, shift=far), carry)
    lam = _lam_of(lam_ref, lam_init)
    for hh in range(ATTN_HEADS):
        (_, l1, acc1), (_, l2, acc2) = carry[(hh, 0)], carry[(hh, 1)]
        o_t = acc1 / l1 - lam * (acc2 / l2)
        o_t = o_t * lax.rsqrt(jnp.mean(o_t * o_t, axis=0, keepdims=True) + SUBLN_EPS) * sw_ref[...]
        o_ref[:, hh * E:(hh + 1) * E] = o_t.T * (1.0 - lam_init)


def _attn_prompt(q, k, v, nb, seq, rel_bias, lam, subln_w, lam_init):
    tq = _row_tile(seq, 256)
    assert tq >= REL_MAX_DIST or tq == seq
    nq = seq // tq
    width = ATTN_HEADS * DIFF_E
    d0 = jnp.arange(tq)[None, :] - jnp.arange(tq)[:, None]
    bias = jnp.stack([_t5_bias(d0, rel_bias), _t5_bias(d0 + tq, rel_bias)])
    bias = jnp.transpose(bias, (3, 0, 1, 2)) * LOG2E
    far = jnp.broadcast_to(rel_bias[REL_BUCKETS - 1][:, None, None] * LOG2E, (DIFF_H, 8, tq)).astype(F32)
    qspec = pl.BlockSpec((tq, width), lambda h, b, i: (b * nq + i, h))
    kspec = pl.BlockSpec((seq, width), lambda h, b, i: (b, h))
    return pl.pallas_call(
        functools.partial(_attn_prompt_body, tq=tq, lam_init=lam_init), grid=(DIFF_H // ATTN_HEADS, nb, nq),
        in_specs=[qspec, kspec, kspec,
                  pl.BlockSpec((ATTN_HEADS, 2, tq, tq), lambda h, b, i: (h, 0, 0, 0)),
                  pl.BlockSpec((ATTN_HEADS, 8, tq), lambda h, b, i: (h, 0, 0)),
                  pl.BlockSpec((4, DIFF_DH), lambda h, b, i: (0, 0)),
                  pl.BlockSpec((DIFF_E, 1), lambda h, b, i: (0, 0))],
        out_specs=qspec, out_shape=jax.ShapeDtypeStruct((nb * seq, D_MODEL), F32),
        scratch_shapes=[pltpu.VMEM((nq, tq, width), BF), pltpu.VMEM((nq, width, tq), BF)], name="attn_prompt",
        compiler_params=pltpu.CompilerParams(dimension_semantics=("parallel", "parallel", "arbitrary"),
                                             vmem_limit_bytes=VMEM_LIMIT),
    )(q, k, v, bias, far, lam, subln_w[:, None])


def _attn_sample_body(pt_ref, q_ref, *refs, n_pages, lam_init):
    del pt_ref
    k_refs, v_refs = refs[:n_pages], refs[n_pages:2 * n_pages]
    kn_ref, vn_ref, bias_ref, lam_ref, sw_ref, o_ref = refs[2 * n_pages:]
    page = k_refs[0].shape[0] // DIFF_H
    steps = o_ref.shape[1]
    heads = range(DIFF_H)
    lam = _lam_of(lam_ref, lam_init)
    sw = sw_ref[...]
    zpad = jnp.zeros((page - kn_ref.shape[2], DIFF_E), BF)

    def head_rows(ref, h):
        return ref[pl.ds(h, page, stride=DIFF_H), :].astype(BF)

    def new_rows(ref, h):
        return jnp.concatenate([ref[0, h].astype(BF), zpad], axis=0)

    scores = [[_dot_nt(q_ref[0, h], head_rows(k_refs[i], h)) for i in range(n_pages)]
              + [_dot_nt(q_ref[0, h], new_rows(kn_ref, h))] for h in heads]
    probs, denom = [], []
    for h in heads:
        sc = jnp.concatenate(scores[h], axis=1) + bias_ref[h]
        pr = jnp.exp(sc - jnp.max(sc, axis=1, keepdims=True))
        denom.append(jnp.sum(pr, axis=1, keepdims=True))
        probs.append(pr.astype(BF))
    acc = []
    for h in heads:
        parts = [jnp.dot(probs[h][:, i * page:(i + 1) * page], head_rows(v_refs[i], h), preferred_element_type=F32)
                 for i in range(n_pages)]
        parts.append(jnp.dot(probs[h][:, n_pages * page:], new_rows(vn_ref, h), preferred_element_type=F32))
        acc.append(functools.reduce(jnp.add, parts))
    for h in heads:
        z = acc[h] / denom[h]
        o = z[:steps] - lam * z[steps:]
        o_ref[0, :, h * DIFF_E:(h + 1) * DIFF_E] = _rms(o, sw, SUBLN_EPS) * (1.0 - lam_init)


def _attn_sample(q, k_new, v_new, cache_k, cache_v, layer, page_table, rel_bias, lam, subln_w, lam_init):
    nb, n_pages = page_table.shape
    page = cache_k.shape[2]
    page_ids = page_table + layer * cache_k.shape[1]
    steps = q.shape[0] // nb
    past = n_pages * page
    width = DIFF_H * DIFF_E
    nrow = 2 * steps
    assert nrow % 8 == 0 and steps <= page
    q4 = jnp.transpose((q * (DIFF_DH ** -0.5)).reshape(nb, steps, DIFF_H, DIFF_E), (0, 2, 1, 3))
    first = jnp.arange(DIFF_E) < DIFF_DH
    q8 = jnp.concatenate([jnp.where(first, q4, 0.0), jnp.where(first, 0.0, q4)], axis=2).astype(BF)
    new_rows = lambda t: jnp.pad(jnp.transpose(t.reshape(nb, steps, DIFF_H, DIFF_E), (0, 2, 1, 3)),
                                 ((0, 0), (0, 0), (0, nrow - steps), (0, 0)))
    q_pos = past + jnp.arange(steps)
    key_pos = jnp.arange(past + steps)
    bias = _t5_bias(q_pos[None, :] - key_pos[:, None], rel_bias)
    bias = jnp.where((key_pos[:, None] <= q_pos[None, :])[..., None], bias, NEG)
    bias = jnp.pad(jnp.transpose(bias, (2, 1, 0)), ((0, 0), (0, 0), (0, page - steps)), constant_values=NEG)
    bias = jnp.concatenate([bias, bias], axis=1)
    rows = page * DIFF_H
    page_specs = [pl.BlockSpec((rows, DIFF_E), lambda b, pt, i=i: (pt[b, i], 0)) for i in range(n_pages)]
    head_spec = pl.BlockSpec((1, DIFF_H, nrow, DIFF_E), lambda b, pt: (b, 0, 0, 0))
    grid_spec = pltpu.PrefetchScalarGridSpec(
        num_scalar_prefetch=1, grid=(nb,),
        in_specs=[head_spec] + page_specs + page_specs + [
            head_spec, head_spec,
            pl.BlockSpec((DIFF_H, nrow, past + page), lambda b, pt: (0, 0, 0)),
            pl.BlockSpec((4, DIFF_DH), lambda b, pt: (0, 0)),
            pl.BlockSpec((1, DIFF_E), lambda b, pt: (0, 0))],
        out_specs=pl.BlockSpec((1, steps, width), lambda b, pt: (b, 0, 0)))
    ck = cache_k.reshape(-1, DIFF_E)
    cv = cache_v.reshape(-1, DIFF_E)
    o = pl.pallas_call(
        functools.partial(_attn_sample_body, n_pages=n_pages, lam_init=lam_init), grid_spec=grid_spec,
        out_shape=jax.ShapeDtypeStruct((nb, steps, width), F32), name="attn_sample",
        compiler_params=pltpu.CompilerParams(dimension_semantics=("parallel",), vmem_limit_bytes=VMEM_LIMIT),
    )(page_ids, q8, *([ck] * n_pages), *([cv] * n_pages), new_rows(k_new), new_rows(v_new), bias, lam,
      subln_w[None])
    return o.reshape(nb * steps, width)


def _attn_layer(x, u, dims, cache_k, cache_v, layer, page_table, p, lam_init, rel_bias):
    nb_p, seq_p, nb_s, seq_s = dims
    (q_p, q_s), (k_p, k_s), (v_p, v_s) = _linear(u, p['w_qkv'], 3, "attn_qkv")
    o_p = _attn_prompt(q_p, k_p, v_p, nb_p, seq_p, rel_bias, p['lam'], p['subln_w'], lam_init)
    o_s = _attn_sample(q_s, k_s, v_s, cache_k, cache_v, layer, page_table, rel_bias, p['lam'], p['subln_w'],
                       lam_init)
    x = _linear_res(x, (o_p, o_s), p['w_o'], "attn_out")
    shp_p = (nb_p, seq_p, DIFF_H, DIFF_E)
    shp_s = (nb_s, seq_s, DIFF_H, DIFF_E)
    return x, (k_p.reshape(shp_p), v_p.reshape(shp_p), k_s.reshape(shp_s), v_s.reshape(shp_s))


def _gdn_proj_body(is_prompt, tile, u_ref, wqkv_ref, wz_ref, wb_ref, wa_ref, al_ref, dt_ref, eb_ref,
                   qkv_o, z_o, beta_o, g_o):
    del is_prompt, tile
    u = u_ref[...].astype(BF)
    qkv_o[...] = jnp.dot(u, wqkv_ref[...], preferred_element_type=F32)
    z_o[...] = jnp.dot(u, wz_ref[...], preferred_element_type=F32)
    beta = _sigmoid(jnp.dot(u, wb_ref[...], preferred_element_type=F32))
    g = -jnp.exp(al_ref[...]) * _softplus(jnp.dot(u, wa_ref[...], preferred_element_type=F32) + dt_ref[...])
    eb = eb_ref[...]
    beta_o[...] = _dot_sel(beta, eb, 3)
    g_o[...] = _dot_sel(g, eb, 3)


def _gdn_post(conv, q_o, k_o, v_o):
    c = conv * _sigmoid(conv)
    hk = GDN_H * GDN_DK
    for h in range(GDN_H):
        qs = c[:, h * GDN_DK:(h + 1) * GDN_DK]
        ks = c[:, hk + h * GDN_DK:hk + (h + 1) * GDN_DK]
        q_o[:, h * GDN_DK:(h + 1) * GDN_DK] = qs * lax.rsqrt(jnp.sum(qs * qs, axis=1, keepdims=True) + 1e-6) * (GDN_DK ** -0.5)
        k_o[:, h * GDN_DK:(h + 1) * GDN_DK] = ks * lax.rsqrt(jnp.sum(ks * ks, axis=1, keepdims=True) + 1e-6)
    v_o[...] = c[:, 2 * hk:]


def _gdn_conv_prompt_body(x_ref, halo_ref, w_ref, q_o, k_o, v_o, *, tiles_per_seq):
    i = pl.program_id(0)
    x = x_ref[...]
    tm = x.shape[0]
    halo = jnp.where(i % tiles_per_seq == 0, 0.0, halo_ref[...])
    row8 = lax.broadcasted_iota(jnp.int32, (8, x.shape[1]), 0)
    w = w_ref[...]
    conv = w[GDN_CONV - 1:GDN_CONV] * x
    for j in range(GDN_CONV - 1):
        sh = GDN_CONV - 1 - j
        rolled = pltpu.roll(x, sh, 0)
        head = jnp.where(row8 < sh, pltpu.roll(halo, sh, 0), rolled[:8])
        tap = jnp.concatenate([head, rolled[8:]], axis=0) if tm > 8 else head
        conv = conv + w[j:j + 1] * tap
    _gdn_post(conv, q_o, k_o, v_o)


def _gdn_conv_taps_body(t0_ref, t1_ref, t2_ref, t3_ref, w_ref, q_o, k_o, v_o):
    w = w_ref[...]
    conv = w[0:1] * t0_ref[...] + w[1:2] * t1_ref[...] + w[2:3] * t2_ref[...] + w[3:4] * t3_ref[...]
    _gdn_post(conv, q_o, k_o, v_o)


def _gdn_advance(L, refs, rows, cols, st):
    q_ref, k_ref, v_ref, beta_ref, g_ref = refs
    P = LANES
    ii, jj = _iota2((L, L))
    strict = ii > jj
    incl = ii >= jj
    us = range(len(st))
    at = lambda ref, u: ref[rows[u], cols[u]]
    gc = [_cumsum_rows(at(g_ref, u)) for u in us]
    q = [at(q_ref, u) for u in us]
    k = [at(k_ref, u) for u in us]
    kk = [_dot_nt(k[u], k[u]) for u in us]
    qk = [_dot_nt(q[u], k[u]) for u in us]
    n_mats, dec_i, e_gc, rhs = [], [], [], []
    for u in us:
        diff = gc[u][:, :L] - gc[u].T[:L, :]
        beta = at(beta_ref, u)
        n_mats.append(-(beta[:, :L] * kk[u] * jnp.exp(jnp.where(strict, diff, NEG))))
        dec_i.append(jnp.exp(jnp.where(incl, diff, NEG)))
        e_gc.append(jnp.exp(gc[u]))
        rhs.append(jnp.concatenate([beta * at(v_ref, u), beta * e_gc[u] * k[u]], axis=1))
    t_inv = _tri_inv(n_mats, L)
    w = [_dot(t_inv[u], rhs[u]) for u in us]
    uu = [w[u][:, :P] - _dot(w[u][:, P:], st[u]) for u in us]
    qs = [_dot(q[u] * e_gc[u], st[u]) for u in us]
    intra = [_dot(qk[u] * dec_i[u], uu[u]) for u in us]
    upd = [_dot_tn(k[u] * jnp.exp(gc[u][L - 1:L, :] - gc[u]), uu[u]) for u in us]
    return ([qs[u] + intra[u] for u in us], [jnp.exp(gc[u][L - 1:L, :]) * st[u] + upd[u] for u in us])


def _gdn_chunk_body(q_ref, k_ref, v_ref, beta_ref, g_ref, o_ref, s_ref, st_sc, *, tb, groups):
    L, P = CHUNK, LANES
    t = pl.program_id(2)

    @pl.when(t == 0)
    def _():
        st_sc[...] = jnp.zeros(st_sc.shape, F32)

    refs = (q_ref, k_ref, v_ref, beta_ref, g_ref)
    cols = [slice(g * P, (g + 1) * P) for g in range(groups)]

    def chunk(c, carry):
        rows = pl.ds(pl.multiple_of(c * L, L), L)
        outs, new = _gdn_advance(L, refs, [rows] * groups, cols, [st_sc[g] for g in range(groups)])
        for g in range(groups):
            o_ref[rows, cols[g]] = outs[g]
            st_sc[g] = new[g]
        return carry

    lax.fori_loop(0, tb // L, chunk, 0, unroll=2)

    @pl.when(t == pl.num_programs(2) - 1)
    def _():
        for g in range(groups):
            s_ref[0, g] = st_sc[g]


def _gdn_short_body(q_ref, k_ref, v_ref, beta_ref, g_ref, s0_ref, o_ref, s_ref, *, bb, length):
    P = LANES
    units = [(b_i, h) for b_i in range(bb) for h in range(GDN_H)]
    rows = [slice(b_i * length, (b_i + 1) * length) for b_i, _ in units]
    cols = [slice(h * P, (h + 1) * P) for _, h in units]
    outs, new = _gdn_advance(length, (q_ref, k_ref, v_ref, beta_ref, g_ref), rows, cols,
                             [s0_ref[b_i, h] for b_i, h in units])
    for u, (b_i, h) in enumerate(units):
        o_ref[rows[u], cols[u]] = outs[u]
        s_ref[b_i, h] = new[u]


def _gdn_out_body(is_prompt, tile, x_ref, o_ref, z_ref, nw_ref, wo_ref, y_ref):
    del is_prompt, tile
    z = z_ref[...]
    nw = nw_ref[...]
    o = o_ref[...]
    parts = []
    for h in range(GDN_H):
        sl = slice(h * GDN_DV, (h + 1) * GDN_DV)
        parts.append(_rms(o[:, sl], nw, NORM_EPS))
    on = jnp.concatenate(parts, axis=1) * (z * _sigmoid(z))
    y_ref[...] = x_ref[...] + _dot(on, wo_ref[...])


def _gdn_layer(x, u, dims, conv_s, state_s, p):
    nb_p, seq_p, nb_s, seq_s = dims
    n_p, n_s = nb_p * seq_p, nb_s * seq_s
    hv = GDN_H * GDN_DV
    w_in = p['w_in']
    pad = lambda w: jnp.pad(w, ((0, 0), (0, LANES - w.shape[-1])))
    eb = (jnp.arange(LANES)[:, None] == jnp.arange(hv)[None, :] // GDN_DV).astype(BF)
    (qkv_p, qkv_s), z, (beta_p, beta_s), (g_p, g_s) = _rows_call(
        _gdn_proj_body, [u],
        [w_in[:, :GDN_QKV].astype(BF), w_in[:, GDN_QKV:GDN_QKV + hv].astype(BF),
         pad(w_in[:, GDN_QKV + hv:GDN_QKV + hv + GDN_H]).astype(BF), pad(w_in[:, GDN_QKV + hv + GDN_H:]).astype(BF),
         pad(p['a_log'][None]), pad(p['dt_bias'][None]), eb],
        [(GDN_QKV, F32), (hv, F32), (hv, F32), (hv, F32)], 256, "gdn_proj")
    tm = _row_tile(seq_p, 256)
    cw = p['conv_w']
    outs3 = [jax.ShapeDtypeStruct((n_p, hv), F32)] * 3
    q_p, k_p, v_p = pl.pallas_call(
        functools.partial(_gdn_conv_prompt_body, tiles_per_seq=seq_p // tm), grid=(n_p // tm,),
        in_specs=[pl.BlockSpec((tm, GDN_QKV), lambda i: (i, 0)),
                  pl.BlockSpec((8, GDN_QKV), lambda i: (jnp.maximum(i * (tm // 8) - 1, 0), 0)),
                  pl.BlockSpec((GDN_CONV, GDN_QKV), lambda i: (0, 0))],
        out_specs=[pl.BlockSpec((tm, hv), lambda i: (i, 0))] * 3, out_shape=outs3, name="gdn_conv_prompt",
        compiler_params=pltpu.CompilerParams(dimension_semantics=("parallel",), vmem_limit_bytes=VMEM_LIMIT),
    )(qkv_p, qkv_p, cw)
    xp_s = jnp.concatenate([conv_s, qkv_s.reshape(nb_s, seq_s, GDN_QKV)], axis=1)
    taps = [xp_s[:, j:j + seq_s].reshape(n_s, GDN_QKV) for j in range(GDN_CONV)]
    ts = _row_tile(n_s, 256)
    q_s, k_s, v_s = pl.pallas_call(
        _gdn_conv_taps_body, grid=(n_s // ts,),
        in_specs=[pl.BlockSpec((ts, GDN_QKV), lambda i: (i, 0))] * GDN_CONV
        + [pl.BlockSpec((GDN_CONV, GDN_QKV), lambda i: (0, 0))],
        out_specs=[pl.BlockSpec((ts, hv), lambda i: (i, 0))] * 3,
        out_shape=[jax.ShapeDtypeStruct((n_s, hv), F32)] * 3, name="gdn_conv_sample",
        compiler_params=pltpu.CompilerParams(dimension_semantics=("parallel",), vmem_limit_bytes=VMEM_LIMIT),
    )(*taps, cw)
    groups = GDN_GROUPS
    tb = _row_tile(seq_p, SEQ_BLOCK)
    nt = seq_p // tb
    spec = pl.BlockSpec((tb, groups * LANES), lambda b, h, t: (b * nt + t, h))
    o_p, s_p = pl.pallas_call(
        functools.partial(_gdn_chunk_body, tb=tb, groups=groups), grid=(nb_p, GDN_H // groups, nt),
        in_specs=[spec] * 5,
        out_specs=[spec, pl.BlockSpec((1, groups, GDN_DK, GDN_DV), lambda b, h, t: (b, h, 0, 0))],
        out_shape=[jax.ShapeDtypeStruct((n_p, hv), F32), jax.ShapeDtypeStruct((nb_p, GDN_H, GDN_DK, GDN_DV), F32)],
        scratch_shapes=[pltpu.VMEM((groups, GDN_DK, GDN_DV), F32)],
        name="gdn_chunk",
        compiler_params=pltpu.CompilerParams(dimension_semantics=("parallel", "parallel", "arbitrary"),
                                             vmem_limit_bytes=VMEM_LIMIT),
    )(q_p, k_p, v_p, beta_p, g_p)
    length = SHORT_LEN * pl.cdiv(seq_s, SHORT_LEN)
    pad_rows = lambda t: jnp.pad(t.reshape(nb_s, seq_s, hv), ((0, 0), (0, length - seq_s), (0, 0))).reshape(nb_s * length, hv)
    bb = _row_tile(nb_s, SHORT_ROWS)
    sspec = pl.BlockSpec((bb * length, hv), lambda b: (b, 0))
    stspec = pl.BlockSpec((bb, GDN_H, GDN_DK, GDN_DV), lambda b: (b, 0, 0, 0))
    o_s, s_s = pl.pallas_call(
        functools.partial(_gdn_short_body, bb=bb, length=length), grid=(nb_s // bb,),
        in_specs=[sspec] * 5 + [stspec], out_specs=[sspec, stspec],
        out_shape=[jax.ShapeDtypeStruct((nb_s * length, hv), F32),
                   jax.ShapeDtypeStruct((nb_s, GDN_H, GDN_DK, GDN_DV), F32)],
        name="gdn_short",
        compiler_params=pltpu.CompilerParams(dimension_semantics=("parallel",), vmem_limit_bytes=VMEM_LIMIT),
    )(pad_rows(q_s), pad_rows(k_s), pad_rows(v_s), pad_rows(beta_s), pad_rows(g_s), state_s)
    o_s = o_s.reshape(nb_s, length, hv)[:, :seq_s].reshape(n_s, hv)
    x = _rows_call(_gdn_out_body, [x, (o_p, o_s), z], [p['norm_w'][None], p['w_o'].astype(BF)],
                   [(D_MODEL, F32)], 512, "gdn_out")[0]
    keep = GDN_CONV - 1
    tail_p = qkv_p.reshape(nb_p, seq_p, GDN_QKV)[:, -keep:]
    conv_p = jnp.pad(tail_p, ((0, 0), (max(keep - seq_p, 0), 0), (0, 0)))
    return x, (s_p, conv_p, s_s, xp_s[:, -keep:])


def kernel(x_prompt, x_sample, state_rwkv_wkv, state_rwkv_shift, cache_attn_k, cache_attn_v, state_gdn, state_gdn_conv, page_table, norm_w, final_norm_w, ffn_w_gate, ffn_w_up, ffn_w_down, rwkv_mu, rwkv_w_rkv, rwkv_w_o, rwkv_w0, rwkv_w1, rwkv_w2, rwkv_a0, rwkv_a1, rwkv_a2, rwkv_g1, rwkv_g2, rwkv_k_k, rwkv_k_a, rwkv_r_k, rwkv_lnx_w, rwkv_lnx_b, rwkv_v0, rwkv_v1, rwkv_v2, attn_w_qkv, attn_w_o, attn_lambda, attn_subln_w, rel_bias, gdn_w_in, gdn_conv_w, gdn_a_log, gdn_dt_bias, gdn_norm_w, gdn_w_o):
    nb_p, seq_p, _ = x_prompt.shape
    nb_s, seq_s, _ = x_sample.shape
    dims = (nb_p, seq_p, nb_s, seq_s)
    depth = norm_w.shape[0]
    x = (x_prompt.reshape(nb_p * seq_p, D_MODEL), x_sample.reshape(nb_s * seq_s, D_MODEL))
    v_first = None
    rw, at, gd = [], [], []
    for i in range(depth):
        kind, j = i % 3, i // 3
        x, u = _ffn(x, norm_w[i, 0], ffn_w_gate[i, 0], ffn_w_up[i, 0], ffn_w_down[i, 0], norm_w[i, 1])
        if kind == 0:
            p = dict(mu=rwkv_mu[j], w_rkv=rwkv_w_rkv[j], w_o=rwkv_w_o[j], w0=rwkv_w0[j], w1=rwkv_w1[j],
                     w2=rwkv_w2[j], a0=rwkv_a0[j], a1=rwkv_a1[j], a2=rwkv_a2[j], g1=rwkv_g1[j], g2=rwkv_g2[j],
                     k_k=rwkv_k_k[j], k_a=rwkv_k_a[j], r_k=rwkv_r_k[j], lnx_w=rwkv_lnx_w[j], lnx_b=rwkv_lnx_b[j])
            vres = None if j == 0 else (rwkv_v0[j - 1], rwkv_v1[j - 1], rwkv_v2[j - 1])
            x, v_first, st = _rwkv_layer(x, u, dims, state_rwkv_shift[j], state_rwkv_wkv[j], p, v_first, vres)
            rw.append(st)
        elif kind == 1:
            p = dict(w_qkv=attn_w_qkv[j], w_o=attn_w_o[j], lam=attn_lambda[j], subln_w=attn_subln_w[j])
            lam_init = 0.8 - 0.6 * math.exp(-0.3 * i)
            x, st = _attn_layer(x, u, dims, cache_attn_k, cache_attn_v, j, page_table, p, lam_init, rel_bias)
            at.append(st)
        else:
            p = dict(w_in=gdn_w_in[j], conv_w=gdn_conv_w[j], a_log=gdn_a_log[j], dt_bias=gdn_dt_bias[j],
                     norm_w=gdn_norm_w[j], w_o=gdn_w_o[j])
            x, st = _gdn_layer(x, u, dims, state_gdn_conv[j], state_gdn[j], p)
            gd.append(st)
        g_next = final_norm_w if i == depth - 1 else jnp.ones((D_MODEL,), F32)
        x, y = _ffn(x, norm_w[i, 2], ffn_w_gate[i, 1], ffn_w_up[i, 1], ffn_w_down[i, 1], g_next)
    stack = lambda lst, k: jnp.stack([t[k] for t in lst])
    return (y[0].reshape(x_prompt.shape), y[1].reshape(x_sample.shape),
            stack(rw, 0), stack(rw, 1), stack(at, 0), stack(at, 1), stack(gd, 0), stack(gd, 1),
            stack(rw, 2), stack(rw, 3), stack(at, 2), stack(at, 3), stack(gd, 2), stack(gd, 3))
```

```python
import functools
import math

import jax
import jax.numpy as jnp
from jax import lax
from jax.experimental import pallas as pl
from jax.experimental.pallas import tpu as pltpu

F32 = jnp.float32
BF = jnp.bfloat16

D_MODEL = 1024
NORM_EPS = 1e-6
RWKV_N = 64
RWKV_H = D_MODEL // RWKV_N
RWKV_LNX_EPS = 64e-5
DIFF_H = 8
DIFF_DH = 64
DIFF_E = 128
SUBLN_EPS = 1e-5
REL_BUCKETS = 32
REL_MAX_EXACT = 16
REL_MAX_DIST = 128
GDN_H = 8
GDN_DK = 128
GDN_DV = 128
GDN_CONV = 4
GDN_QKV = GDN_H * (2 * GDN_DK + GDN_DV)
FFN_TILE = 1024
FFN_CHUNK_BIG_TILE = 256
FFN_CHUNK_SMALL_TILE = 1408
CHUNK = 64
SEQ_BLOCK = 512
RWKV_GROUPS = 8
GDN_GROUPS = 8
GDN_INV_SIDE = 1
RWKV_INV_UNITS = 1
LANES = 128
NEG = -1e30
LOG2E = math.log2(math.e)
VMEM_LIMIT = 56 * 1024 * 1024


def _dot(a, b):
    return jnp.dot(a.astype(BF), b.astype(BF), preferred_element_type=F32)


def _dot_nt(a, b):
    return lax.dot_general(a.astype(BF), b.astype(BF), (((1,), (1,)), ((), ())), preferred_element_type=F32)


def _dot_tn(a, b):
    return lax.dot_general(a.astype(BF), b.astype(BF), (((0,), (0,)), ((), ())), preferred_element_type=F32)


def _rms(x, g, eps):
    return x * lax.rsqrt(jnp.mean(x * x, axis=-1, keepdims=True) + eps) * g


def _sigmoid(x):
    return 1.0 / (1.0 + jnp.exp(-x))


def _softplus(x):
    return jnp.maximum(x, 0.0) + jnp.log(1.0 + jnp.exp(-jnp.abs(x)))


def _iota2(shape):
    return lax.broadcasted_iota(jnp.int32, shape, 0), lax.broadcasted_iota(jnp.int32, shape, 1)


def _split(x, n):
    out = []
    for _ in range(n):
        h = x.astype(BF)
        out.append(h)
        x = x - h.astype(F32)
    return out


def _dot_sel(x, sel, n=2):
    return sum(jnp.dot(t, sel, preferred_element_type=F32) for t in _split(x, n))


def _segment_allsum(x, e, et):
    return _dot_sel(_dot_sel(x, e), et)


def _cumsum_rows(x):
    n = x.shape[0]
    row = lax.broadcasted_iota(jnp.int32, x.shape, 0)
    shift = 1
    while shift < n:
        x = x + jnp.where(row >= shift, pltpu.roll(x, shift, 0), 0.0)
        shift *= 2
    return x


def _block_diag(m_row, size):
    mb = m_row.astype(BF)
    blk = lax.broadcasted_iota(jnp.int32, mb.shape, 1) >> (size.bit_length() - 1)
    return jnp.concatenate([jnp.where(blk == h, mb, jnp.zeros_like(mb)) for h in range(mb.shape[1] // size)], axis=0)


def _tri_inv(n_rows, size):
    ii, jj = _iota2(n_rows[0].shape)
    jj = jj & (size - 1)
    eye = (ii == jj).astype(F32)
    xs = [eye + jnp.where((ii >> 1) == (jj >> 1), n, 0.0) for n in n_rows]
    lvl = 1
    while (2 << lvl) <= size:
        sel = ((ii >> (lvl + 1)) == (jj >> (lvl + 1))) & (((ii >> lvl) & 1) == 1) & (((jj >> lvl) & 1) == 0)
        ys = [_dot(x, _block_diag(jnp.where(sel, n, 0.0), size)) for x, n in zip(xs, n_rows)]
        xs = [x + _dot(y, _block_diag(x, size)) for x, y in zip(xs, ys)]
        lvl += 1
    return xs


def _row_tile(n, want):
    tm = min(want, n)
    while n % tm:
        tm //= 2
    return tm


HALO = 8
SAMPLE_TILE = 128


def _rows_call(body, rows, consts, outs, tm, name, halo=()):
    n_p, n_s = rows[0][0].shape[0], rows[0][1].shape[0]
    tm_p, tm_s = _row_tile(n_p, tm), _row_tile(n_s, min(tm, SAMPLE_TILE))
    tp, ts = n_p // tm_p, n_s // tm_s
    p_idx = lambda i: jnp.minimum(i, tp - 1)
    s_idx = lambda i: jnp.maximum(i - tp, 0)
    in_specs, args = [], []
    for k, (a_p, a_s) in enumerate(rows):
        if k in halo:
            in_specs.append(pl.BlockSpec((HALO, a_p.shape[1]),
                                         lambda i: (jnp.maximum(p_idx(i) * (tm_p // HALO) - 1, 0), 0)))
        else:
            in_specs.append(pl.BlockSpec((tm_p, a_p.shape[1]), lambda i: (p_idx(i), 0)))
        in_specs.append(pl.BlockSpec((tm_s, a_s.shape[1]), lambda i: (s_idx(i), 0)))
        args += [a_p, a_s]
    for a in consts:
        in_specs.append(pl.BlockSpec(a.shape, lambda i, nd=a.ndim: (0,) * nd, pipeline_mode=pl.Buffered(1)))
    out_specs, out_shape = [], []
    for c, dt in outs:
        out_specs += [pl.BlockSpec((tm_p, c), lambda i: (p_idx(i), 0)), pl.BlockSpec((tm_s, c), lambda i: (s_idx(i), 0))]
        out_shape += [jax.ShapeDtypeStruct((n_p, c), dt), jax.ShapeDtypeStruct((n_s, c), dt)]
    nr, nc = len(rows), len(consts)

    def kern(*refs):
        i = pl.program_id(0)
        cs = refs[2 * nr:2 * nr + nc]
        o = refs[2 * nr + nc:]

        @pl.when(i < tp)
        def _():
            body(True, i, *refs[0:2 * nr:2], *cs, *o[0::2])

        @pl.when(i >= tp)
        def _():
            body(False, i - tp, *refs[1:2 * nr:2], *cs, *o[1::2])

    res = pl.pallas_call(
        kern, grid=(tp + ts,), in_specs=in_specs, out_specs=out_specs, out_shape=out_shape, name=name,
        compiler_params=pltpu.CompilerParams(dimension_semantics=("arbitrary",), vmem_limit_bytes=VMEM_LIMIT),
    )(*args, *consts)
    return [(res[2 * k], res[2 * k + 1]) for k in range(len(outs))]


def _ffn_body(is_prompt, tile, x_ref, g_ref, wg_ref, wu_ref, wd_ref, *rest, normed):
    del is_prompt, tile
    (g2_ref, o_ref, u_ref) = rest if normed else (None, rest[0], None)
    x = x_ref[...]
    h = _rms(x, g_ref[...], NORM_EPS).astype(BF)
    hidden = wg_ref.shape[1]
    want = FFN_CHUNK_BIG_TILE if x.shape[0] >= FFN_TILE else FFN_CHUNK_SMALL_TILE
    nchunk = max(n for n in range(1, hidden // LANES + 1) if hidden % (n * LANES) == 0 and hidden // n >= want)
    fc = hidden // nchunk
    acc = jnp.zeros_like(x)
    for c in range(nchunk):
        sl = slice(c * fc, (c + 1) * fc)
        gate = jnp.dot(h, wg_ref[:, sl], preferred_element_type=F32)
        up = jnp.dot(h, wu_ref[:, sl], preferred_element_type=F32)
        act = (gate * _sigmoid(gate) * up).astype(BF)
        acc = acc + jnp.dot(act, wd_ref[sl, :], preferred_element_type=F32)
    y = x + 0.5 * acc
    o_ref[...] = y
    if normed:
        u_ref[...] = _rms(y, g2_ref[...], NORM_EPS)


def _ffn(x, g, wg, wu, wd, g_next=None):
    normed = g_next is not None
    consts = [g[None], wg.astype(BF), wu.astype(BF), wd.astype(BF)] + ([g_next[None]] if normed else [])
    outs = _rows_call(functools.partial(_ffn_body, normed=normed), [x], consts,
                      [(D_MODEL, F32)] * (2 if normed else 1), FFN_TILE, "ffn_half")
    return outs if normed else outs[0]


def _linear_body(is_prompt, tile, *refs, n_out, residual):
    del is_prompt, tile
    if residual:
        x_ref, u_ref, w_ref = refs[:3]
        outs = refs[3:]
    else:
        u_ref, w_ref = refs[:2]
        outs = refs[2:]
    y = jnp.dot(u_ref[...].astype(BF), w_ref[...], preferred_element_type=F32)
    if residual:
        outs[0][...] = x_ref[...] + y
    else:
        c = y.shape[1] // n_out
        for k, o in enumerate(outs):
            o[...] = y[:, k * c:(k + 1) * c]


def _linear(u, w, n_out, name):
    body = functools.partial(_linear_body, n_out=n_out, residual=False)
    c = w.shape[1] // n_out
    return _rows_call(body, [u], [w.astype(BF)], [(c, F32)] * n_out, 512, name)


def _linear_res(x, u, w, name):
    body = functools.partial(_linear_body, n_out=1, residual=True)
    return _rows_call(body, [x, u], [w.astype(BF)], [(w.shape[1], F32)], 512, name)[0]


def _rwkv_proj_body(is_prompt, tile, *refs, has_vres, tiles_per_seq):
    (u_ref, prev_ref) = refs[:2]
    k0 = 2
    if has_vres:
        vf_ref = refs[2]
        k0 = 3
    (mu_ref, wr_ref, wk_ref, wv_ref, w0_ref, w1_ref, w2_ref, a0_ref, a1_ref, a2_ref, g1_ref, g2_ref,
     kk_ref, ka_ref, rk_ref, e_ref, et_ref) = refs[k0:k0 + 17]
    k1 = k0 + 17
    if has_vres:
        v0_ref, v1_ref, v2_ref = refs[k1:k1 + 3]
        k1 += 3
    r_o, lw_o, k_o, v_o, a_o, b_o, g_o, bon_o = refs[k1:]
    u = u_ref[...]
    if is_prompt:
        before = jnp.where(tile % tiles_per_seq == 0, 0.0, prev_ref[HALO - 1:HALO, :])
        row = lax.broadcasted_iota(jnp.int32, u.shape, 0)
        prev = jnp.where(row == 0, before, pltpu.roll(u, 1, 0))
    else:
        prev = prev_ref[...]
    xx = prev - u
    mu = mu_ref[...]
    xr, xw, xk = u + xx * mu[0:1], u + xx * mu[1:2], u + xx * mu[2:3]
    xv, xa, xg = u + xx * mu[3:4], u + xx * mu[4:5], u + xx * mu[5:6]
    r = _dot(xr, wr_ref[...])
    k = _dot(xk, wk_ref[...])
    v = _dot(xv, wv_ref[...])
    wlog = -_softplus(-(w0_ref[...] + _dot(jnp.tanh(_dot(xw, w1_ref[...])), w2_ref[...]))) - 0.5
    lw_o[...] = -jnp.exp(wlog)
    if has_vres:
        v = v + (vf_ref[...] - v) * _sigmoid(v0_ref[...] + _dot(_dot(xv, v1_ref[...]), v2_ref[...]))
    a = _sigmoid(a0_ref[...] + _dot(_dot(xa, a1_ref[...]), a2_ref[...]))
    g_o[...] = _dot(_sigmoid(_dot(xg, g1_ref[...])), g2_ref[...])
    kk = k * kk_ref[...]
    e, et = e_ref[...], et_ref[...]
    kk = kk * lax.rsqrt(_segment_allsum(kk * kk, e, et) + 1e-24)
    k_mod = k * (1.0 + (a - 1.0) * ka_ref[...])
    r_o[...] = r
    k_o[...] = k_mod
    v_o[...] = v
    a_o[...] = -kk
    b_o[...] = kk * a
    bon_o[...] = _segment_allsum(r * k_mod * rk_ref[...], e, et) * v


def _rwkv_advance(L, refs, rows, cols, st):
    r_ref, lw_ref, k_ref, v_ref, a_ref, b_ref = refs
    P, N = LANES, RWKV_N
    assert len(st) % RWKV_INV_UNITS == 0
    ii, jj = _iota2((2 * L, 4 * L))
    jj = jj & (L - 1)
    mask4 = (((ii < L) & (ii > jj)) | ((ii >= L) & (ii - L >= jj))).astype(F32)
    lane = lax.broadcasted_iota(jnp.int32, (L, P), 1)
    bi, bj = _iota2((P, P))
    bd = ((bi >= N) == (bj >= N)).astype(F32)
    us = range(len(st))
    head_lanes = [(lane >= N) == bool(h) for h in range(2)]

    def by_head(t):
        return jnp.concatenate([jnp.where(m, t, jnp.zeros_like(t)) for m in head_lanes], axis=0)

    x, wt, v, bk_end, w_end = [], [], [], [], []
    for u in us:
        at = lambda ref: ref[rows[u], cols[u]]
        lw = at(lw_ref)
        gc = _cumsum_rows(lw)
        gl = gc[L - 1:L, :]
        r, k, a, b = at(r_ref), at(k_ref), at(a_ref), at(b_ref)
        e_neg = jnp.exp(-gc)
        e_end = jnp.exp(gl - gc)
        x.append(jnp.concatenate([a * jnp.exp(gc - lw), r * jnp.exp(gc)], axis=0).astype(BF))
        wt.append(jnp.concatenate([by_head((b * e_neg).astype(BF)), by_head((k * e_neg).astype(BF))], axis=0))
        v.append(at(v_ref))
        bk_end.append(jnp.concatenate([b * e_end, k * e_end], axis=0))
        w_end.append(jnp.exp(gl))
    xs = [_dot_nt(x[u], st[u]) for u in us]
    abk = [_dot_nt(x[u], wt[u]) * mask4 for u in us]
    side = RWKV_INV_UNITS
    t_rows = _tri_inv([jnp.concatenate([abk[u + d][:L, :2 * L] for d in range(side)], axis=1) for u in us[::side]], L)
    t_inv = [t_rows[u // side][:, (u % side) * 2 * L:(u % side + 1) * 2 * L] for u in us]
    akv = [_dot(abk[u][:, 2 * L:], by_head(v[u])) for u in us]
    rhs = [xs[u][:L] + akv[u][:L] for u in us]
    uu = [_dot(t_inv[u], by_head(rhs[u])) for u in us]
    o_u = [_dot(abk[u][L:, :2 * L], by_head(uu[u])) for u in us]
    upd = [_dot_tn(jnp.concatenate([uu[u], v[u]], axis=0), bk_end[u]) for u in us]
    outs = [xs[u][L:] + akv[u][L:] + o_u[u] for u in us]
    return outs, [st[u] * w_end[u] + bd * upd[u] for u in us]


def _rwkv_chunk_body(r_ref, lw_ref, k_ref, v_ref, a_ref, b_ref, o_ref, s_ref, st_sc, *, tb, groups):
    L, P, N = CHUNK, LANES, RWKV_N
    t = pl.program_id(2)

    @pl.when(t == 0)
    def _():
        st_sc[...] = jnp.zeros(st_sc.shape, F32)

    refs = (r_ref, lw_ref, k_ref, v_ref, a_ref, b_ref)
    cols = [slice(g * P, (g + 1) * P) for g in range(groups)]

    def chunk(c, carry):
        rows = pl.ds(pl.multiple_of(c * L, L), L)
        outs, new = _rwkv_advance(L, refs, [rows] * groups, cols, [st_sc[g] for g in range(groups)])
        for g in range(groups):
            o_ref[rows, cols[g]] = outs[g]
            st_sc[g] = new[g]
        return carry

    lax.fori_loop(0, tb // L, chunk, 0, unroll=2)

    @pl.when(t == pl.num_programs(2) - 1)
    def _():
        for g in range(groups):
            s = st_sc[g]
            s_ref[0, 2 * g] = s[:N, :N]
            s_ref[0, 2 * g + 1] = s[N:, N:]


def _rwkv_chunk(seqs, nb, seq):
    groups = RWKV_GROUPS
    width = groups * LANES
    tb = _row_tile(seq, SEQ_BLOCK)
    nt = seq // tb
    spec = pl.BlockSpec((tb, width), lambda b, h, t: (b * nt + t, h))
    return pl.pallas_call(
        functools.partial(_rwkv_chunk_body, tb=tb, groups=groups), grid=(nb, D_MODEL // width, nt),
        in_specs=[spec] * 6,
        out_specs=[spec, pl.BlockSpec((1, 2 * groups, RWKV_N, RWKV_N), lambda b, h, t: (b, h, 0, 0))],
        out_shape=[jax.ShapeDtypeStruct((nb * seq, D_MODEL), F32),
                   jax.ShapeDtypeStruct((nb, RWKV_H, RWKV_N, RWKV_N), F32)],
        scratch_shapes=[pltpu.VMEM((groups, LANES, LANES), F32)],
        name="rwkv_chunk",
        compiler_params=pltpu.CompilerParams(dimension_semantics=("parallel", "parallel", "arbitrary"),
                                             vmem_limit_bytes=VMEM_LIMIT),
    )(*seqs)


def _rwkv_short_body(r_ref, lw_ref, k_ref, v_ref, a_ref, b_ref, s0_ref, o_ref, s_ref, *, bb, length):
    P, N = LANES, RWKV_N
    groups = D_MODEL // P
    units = [(b_i, g) for b_i in range(bb) for g in range(groups)]
    zero = jnp.zeros((N, N), F32)
    st = [jnp.concatenate([jnp.concatenate([s0_ref[b_i, 2 * g], zero], axis=1),
                           jnp.concatenate([zero, s0_ref[b_i, 2 * g + 1]], axis=1)], axis=0) for b_i, g in units]
    rows = [slice(b_i * length, (b_i + 1) * length) for b_i, _ in units]
    cols = [slice(g * P, (g + 1) * P) for _, g in units]
    outs, new = _rwkv_advance(length, (r_ref, lw_ref, k_ref, v_ref, a_ref, b_ref), rows, cols, st)
    for u, (b_i, g) in enumerate(units):
        o_ref[rows[u], cols[u]] = outs[u]
        s_ref[b_i, 2 * g] = new[u][:N, :N]
        s_ref[b_i, 2 * g + 1] = new[u][N:, N:]


SHORT_LEN = 8
SHORT_ROWS = 4


def _rwkv_short(seqs, s0, nb, steps):
    length = SHORT_LEN * pl.cdiv(steps, SHORT_LEN)
    pad = lambda t: jnp.pad(t.reshape(nb, steps, D_MODEL), ((0, 0), (0, length - steps), (0, 0))).reshape(nb * length, D_MODEL)
    bb = _row_tile(nb, SHORT_ROWS)
    spec = pl.BlockSpec((bb * length, D_MODEL), lambda b: (b, 0))
    sspec = pl.BlockSpec((bb, RWKV_H, RWKV_N, RWKV_N), lambda b: (b, 0, 0, 0))
    o, s = pl.pallas_call(
        functools.partial(_rwkv_short_body, bb=bb, length=length), grid=(nb // bb,),
        in_specs=[spec] * 6 + [sspec], out_specs=[spec, sspec],
        out_shape=[jax.ShapeDtypeStruct((nb * length, D_MODEL), F32),
                   jax.ShapeDtypeStruct((nb, RWKV_H, RWKV_N, RWKV_N), F32)],
        name="rwkv_short",
        compiler_params=pltpu.CompilerParams(dimension_semantics=("parallel",), vmem_limit_bytes=VMEM_LIMIT),
    )(*[pad(t) for t in seqs], s0)
    return o.reshape(nb, length, D_MODEL)[:, :steps].reshape(nb * steps, D_MODEL), s


def _rwkv_out_body(is_prompt, tile, x_ref, o_ref, g_ref, bon_ref, lw_ref, lb_ref, e_ref, et_ref, wo_ref, y_ref):
    del is_prompt, tile
    e, et = e_ref[...], et_ref[...]
    o = o_ref[...]
    inv_n = 1.0 / RWKV_N
    d = o - _segment_allsum(o, e, et) * inv_n
    var = _segment_allsum(d * d, e, et) * inv_n
    on = d * lax.rsqrt(var + RWKV_LNX_EPS) * lw_ref[...] + lb_ref[...]
    y_ref[...] = x_ref[...] + _dot((on + bon_ref[...]) * g_ref[...], wo_ref[...])


def _rwkv_layer(x, u, dims, shift_s, wkv_s, p, v_first, vres):
    nb_p, seq_p, nb_s, seq_s = dims
    u_p, u_s = u
    u3s = u_s.reshape(nb_s, seq_s, D_MODEL)
    prev_s = jnp.concatenate([shift_s[:, None, :], u3s[:, :-1]], axis=1).reshape(nb_s * seq_s, D_MODEL)
    has_vres = vres is not None
    e = (jnp.arange(D_MODEL)[:, None] // RWKV_N == jnp.arange(RWKV_H)[None, :]).astype(BF)
    et = e.T
    tm = _row_tile(seq_p, 256)
    rows = [u, (u_p, prev_s)] + ([v_first] if has_vres else [])
    consts = [p['mu'], p['w_rkv'][0].astype(BF), p['w_rkv'][1].astype(BF), p['w_rkv'][2].astype(BF),
              p['w0'][None], p['w1'].astype(BF), p['w2'].astype(BF), p['a0'][None], p['a1'].astype(BF),
              p['a2'].astype(BF), p['g1'].astype(BF), p['g2'].astype(BF), p['k_k'][None], p['k_a'][None],
              p['r_k'].reshape(1, D_MODEL), e, et]
    if has_vres:
        consts += [vres[0][None], vres[1].astype(BF), vres[2].astype(BF)]
    body = functools.partial(_rwkv_proj_body, has_vres=has_vres, tiles_per_seq=seq_p // tm)
    r, lw, k, v, a, b, gate, bonus = _rows_call(body, rows, consts, [(D_MODEL, F32)] * 8, tm, "rwkv_proj", halo=(1,))
    seqs = (r, lw, k, v, a, b)
    o_p, s_p = _rwkv_chunk([t[0] for t in seqs], nb_p, seq_p)
    o_s, s_s = _rwkv_short([t[1] for t in seqs], wkv_s, nb_s, seq_s)
    x = _rows_call(_rwkv_out_body, [x, (o_p, o_s), gate, bonus],
                   [p['lnx_w'][None], p['lnx_b'][None], e, et, p['w_o'].astype(BF)], [(D_MODEL, F32)], 512,
                   "rwkv_out")[0]
    u3p = u_p.reshape(nb_p, seq_p, D_MODEL)
    return x, (v if not has_vres else v_first), (s_p, u3p[:, -1], s_s, u3s[:, -1])


def _t5_bias(dist, rel_bias):
    n = jnp.maximum(dist, 0)
    nf = jnp.maximum(n, 1).astype(F32)
    large = REL_MAX_EXACT + (jnp.log(nf / REL_MAX_EXACT) / math.log(REL_MAX_DIST / REL_MAX_EXACT)
                             * (REL_BUCKETS - REL_MAX_EXACT)).astype(jnp.int32)
    large = jnp.minimum(large, REL_BUCKETS - 1)
    bucket = jnp.where(n < REL_MAX_EXACT, n, large)
    onehot = (bucket[..., None] == jnp.arange(REL_BUCKETS)).astype(F32)
    return jnp.einsum('...b,bh->...h', onehot, rel_bias.astype(F32), precision=lax.Precision.HIGHEST)


def _lam_of(lam_ref, lam_init):
    lv = lam_ref[...]
    return (jnp.exp(jnp.sum(lv[0:1] * lv[1:2], axis=1, keepdims=True))
            - jnp.exp(jnp.sum(lv[2:3] * lv[3:4], axis=1, keepdims=True)) + lam_init)


ATTN_HEADS = 2


def _attn_prompt_body(q_ref, k_ref, v_ref, bias_ref, far_ref, lam_ref, sw_ref, o_ref, kb_sc, vt_sc, *, tq, lam_init):
    qi = pl.program_id(2)
    E, dh = DIFF_E, DIFF_DH
    nkv = kb_sc.shape[0]
    chains = [(hh, c) for hh in range(ATTN_HEADS) for c in range(2)]

    @pl.when(qi == 0)
    def _():
        for j in range(nkv):
            kb_sc[j] = k_ref[j * tq:(j + 1) * tq, :].astype(BF)
            vt_sc[j] = v_ref[j * tq:(j + 1) * tq, :].T.astype(BF)

    q_t = (q_ref[...] * (dh ** -0.5 * LOG2E)).T
    row = lax.broadcasted_iota(jnp.int32, q_t.shape, 0)
    q_c = {(hh, c): jnp.where((row >= hh * E + c * dh) & (row < hh * E + (c + 1) * dh), q_t, 0.0).astype(BF)
           for hh, c in chains}
    kk, qq = _iota2((tq, tq))

    def step(carry, kj, bias=None, shift=None):
        kb = kb_sc[kj]
        s = {ch: jnp.dot(kb, q_c[ch], preferred_element_type=F32) for ch in chains}
        if bias is not None:
            s = {ch: s[ch] + bias[ch[0]] for ch in chains}
        top = {ch: jnp.max(s[ch], axis=0, keepdims=True) for ch in chains}
        if shift is not None:
            top = {ch: top[ch] + shift[ch[0]] for ch in chains}
        m_new = {ch: jnp.maximum(carry[ch][0], top[ch]) for ch in chains}
        alpha = {ch: jnp.exp2(carry[ch][0] - m_new[ch]) for ch in chains}
        offs = m_new if shift is None else {ch: m_new[ch] - shift[ch[0]] for ch in chains}
        pr = {ch: jnp.exp2(s[ch] - offs[ch]) for ch in chains}
        l_new = {ch: carry[ch][1] * alpha[ch] + jnp.sum(pr[ch], axis=0, keepdims=True) for ch in chains}
        vt = vt_sc[kj]
        pv = {(hh, c): jnp.dot(vt[hh * E:(hh + 1) * E, :], pr[(hh, c)].astype(BF), preferred_element_type=F32)
              for hh, c in chains}
        return {ch: (m_new[ch], l_new[ch], carry[ch][2] * alpha[ch] + pv[ch]) for ch in chains}

    init = {ch: (jnp.full((1, tq), NEG, F32), jnp.zeros((1, tq), F32), jnp.zeros((E, tq), F32)) for ch in chains}
    carry = step(init, qi, bias=[jnp.where(kk <= qq, bias_ref[hh, 0], NEG) for hh in range(ATTN_HEADS)])
    carry = step(carry, jnp.maximum(qi - 1, 0),
                 bias=[jnp.where(qi >= 1, bias_ref[hh, 1], NEG) for hh in range(ATTN_HEADS)])
    far = [far_ref[hh, 0:1, :] for hh in range(ATTN_HEADS)]
    carry = lax.fori_loop(0, jnp.maximum(qi - 1, 0), lambda kj, c: step(c, kj, shift=far), carry)
    lam = _lam_of(lam_ref, lam_init)
    for hh in range(ATTN_HEADS):
        (_, l1, acc1), (_, l2, acc2) = carry[(hh, 0)], carry[(hh, 1)]
        o_t = acc1 / l1 - lam * (acc2 / l2)
        o_t = o_t * lax.rsqrt(jnp.mean(o_t * o_t, axis=0, keepdims=True) + SUBLN_EPS) * sw_ref[...]
        o_ref[:, hh * E:(hh + 1) * E] = o_t.T * (1.0 - lam_init)


def _attn_prompt(q, k, v, nb, seq, rel_bias, lam, subln_w, lam_init):
    tq = _row_tile(seq, 256)
    assert tq >= REL_MAX_DIST or tq == seq
    nq = seq // tq
    width = ATTN_HEADS * DIFF_E
    d0 = jnp.arange(tq)[None, :] - jnp.arange(tq)[:, None]
    bias = jnp.stack([_t5_bias(d0, rel_bias), _t5_bias(d0 + tq, rel_bias)])
    bias = jnp.transpose(bias, (3, 0, 1, 2)) * LOG2E
    far = jnp.broadcast_to(rel_bias[REL_BUCKETS - 1][:, None, None] * LOG2E, (DIFF_H, 8, tq)).astype(F32)
    qspec = pl.BlockSpec((tq, width), lambda h, b, i: (b * nq + i, h))
    kspec = pl.BlockSpec((seq, width), lambda h, b, i: (b, h))
    return pl.pallas_call(
        functools.partial(_attn_prompt_body, tq=tq, lam_init=lam_init), grid=(DIFF_H // ATTN_HEADS, nb, nq),
        in_specs=[qspec, kspec, kspec,
                  pl.BlockSpec((ATTN_HEADS, 2, tq, tq), lambda h, b, i: (h, 0, 0, 0)),
                  pl.BlockSpec((ATTN_HEADS, 8, tq), lambda h, b, i: (h, 0, 0)),
                  pl.BlockSpec((4, DIFF_DH), lambda h, b, i: (0, 0)),
                  pl.BlockSpec((DIFF_E, 1), lambda h, b, i: (0, 0))],
        out_specs=qspec, out_shape=jax.ShapeDtypeStruct((nb * seq, D_MODEL), F32),
        scratch_shapes=[pltpu.VMEM((nq, tq, width), BF), pltpu.VMEM((nq, width, tq), BF)], name="attn_prompt",
        compiler_params=pltpu.CompilerParams(dimension_semantics=("parallel", "parallel", "arbitrary"),
                                             vmem_limit_bytes=VMEM_LIMIT),
    )(q, k, v, bias, far, lam, subln_w[:, None])


def _attn_sample_body(pt_ref, q_ref, *refs, n_pages, lam_init):
    del pt_ref
    k_refs, v_refs = refs[:n_pages], refs[n_pages:2 * n_pages]
    kn_ref, vn_ref, bias_ref, lam_ref, sw_ref, o_ref = refs[2 * n_pages:]
    page = k_refs[0].shape[0] // DIFF_H
    steps = o_ref.shape[1]
    heads = range(DIFF_H)
    lam = _lam_of(lam_ref, lam_init)
    sw = sw_ref[...]
    zpad = jnp.zeros((page - kn_ref.shape[2], DIFF_E), BF)

    def head_rows(ref, h):
        return ref[pl.ds(h, page, stride=DIFF_H), :].astype(BF)

    def new_rows(ref, h):
        return jnp.concatenate([ref[0, h].astype(BF), zpad], axis=0)

    scores = [[_dot_nt(q_ref[0, h], head_rows(k_refs[i], h)) for i in range(n_pages)]
              + [_dot_nt(q_ref[0, h], new_rows(kn_ref, h))] for h in heads]
    probs, denom = [], []
    for h in heads:
        sc = jnp.concatenate(scores[h], axis=1) + bias_ref[h]
        pr = jnp.exp(sc - jnp.max(sc, axis=1, keepdims=True))
        denom.append(jnp.sum(pr, axis=1, keepdims=True))
        probs.append(pr.astype(BF))
    acc = []
    for h in heads:
        parts = [jnp.dot(probs[h][:, i * page:(i + 1) * page], head_rows(v_refs[i], h), preferred_element_type=F32)
                 for i in range(n_pages)]
        parts.append(jnp.dot(probs[h][:, n_pages * page:], new_rows(vn_ref, h), preferred_element_type=F32))
        acc.append(functools.reduce(jnp.add, parts))
    for h in heads:
        z = acc[h] / denom[h]
        o = z[:steps] - lam * z[steps:]
        o_ref[0, :, h * DIFF_E:(h + 1) * DIFF_E] = _rms(o, sw, SUBLN_EPS) * (1.0 - lam_init)


def _attn_sample(q, k_new, v_new, cache_k, cache_v, layer, page_table, rel_bias, lam, subln_w, lam_init):
    nb, n_pages = page_table.shape
    page = cache_k.shape[2]
    page_ids = page_table + layer * cache_k.shape[1]
    steps = q.shape[0] // nb
    past = n_pages * page
    width = DIFF_H * DIFF_E
    nrow = 2 * steps
    assert nrow % 8 == 0 and steps <= page
    q4 = jnp.transpose((q * (DIFF_DH ** -0.5)).reshape(nb, steps, DIFF_H, DIFF_E), (0, 2, 1, 3))
    first = jnp.arange(DIFF_E) < DIFF_DH
    q8 = jnp.concatenate([jnp.where(first, q4, 0.0), jnp.where(first, 0.0, q4)], axis=2).astype(BF)
    new_rows = lambda t: jnp.pad(jnp.transpose(t.reshape(nb, steps, DIFF_H, DIFF_E), (0, 2, 1, 3)),
                                 ((0, 0), (0, 0), (0, nrow - steps), (0, 0)))
    q_pos = past + jnp.arange(steps)
    key_pos = jnp.arange(past + steps)
    bias = _t5_bias(q_pos[None, :] - key_pos[:, None], rel_bias)
    bias = jnp.where((key_pos[:, None] <= q_pos[None, :])[..., None], bias, NEG)
    bias = jnp.pad(jnp.transpose(bias, (2, 1, 0)), ((0, 0), (0, 0), (0, page - steps)), constant_values=NEG)
    bias = jnp.concatenate([bias, bias], axis=1)
    rows = page * DIFF_H
    page_specs = [pl.BlockSpec((rows, DIFF_E), lambda b, pt, i=i: (pt[b, i], 0)) for i in range(n_pages)]
    head_spec = pl.BlockSpec((1, DIFF_H, nrow, DIFF_E), lambda b, pt: (b, 0, 0, 0))
    grid_spec = pltpu.PrefetchScalarGridSpec(
        num_scalar_prefetch=1, grid=(nb,),
        in_specs=[head_spec] + page_specs + page_specs + [
            head_spec, head_spec,
            pl.BlockSpec((DIFF_H, nrow, past + page), lambda b, pt: (0, 0, 0)),
            pl.BlockSpec((4, DIFF_DH), lambda b, pt: (0, 0)),
            pl.BlockSpec((1, DIFF_E), lambda b, pt: (0, 0))],
        out_specs=pl.BlockSpec((1, steps, width), lambda b, pt: (b, 0, 0)))
    ck = cache_k.reshape(-1, DIFF_E)
    cv = cache_v.reshape(-1, DIFF_E)
    o = pl.pallas_call(
        functools.partial(_attn_sample_body, n_pages=n_pages, lam_init=lam_init), grid_spec=grid_spec,
        out_shape=jax.ShapeDtypeStruct((nb, steps, width), F32), name="attn_sample",
        compiler_params=pltpu.CompilerParams(dimension_semantics=("parallel",), vmem_limit_bytes=VMEM_LIMIT),
    )(page_ids, q8, *([ck] * n_pages), *([cv] * n_pages), new_rows(k_new), new_rows(v_new), bias, lam,
      subln_w[None])
    return o.reshape(nb * steps, width)


def _attn_layer(x, u, dims, cache_k, cache_v, layer, page_table, p, lam_init, rel_bias):
    nb_p, seq_p, nb_s, seq_s = dims
    (q_p, q_s), (k_p, k_s), (v_p, v_s) = _linear(u, p['w_qkv'], 3, "attn_qkv")
    o_p = _attn_prompt(q_p, k_p, v_p, nb_p, seq_p, rel_bias, p['lam'], p['subln_w'], lam_init)
    o_s = _attn_sample(q_s, k_s, v_s, cache_k, cache_v, layer, page_table, rel_bias, p['lam'], p['subln_w'],
                       lam_init)
    x = _linear_res(x, (o_p, o_s), p['w_o'], "attn_out")
    shp_p = (nb_p, seq_p, DIFF_H, DIFF_E)
    shp_s = (nb_s, seq_s, DIFF_H, DIFF_E)
    return x, (k_p.reshape(shp_p), v_p.reshape(shp_p), k_s.reshape(shp_s), v_s.reshape(shp_s))


def _gdn_proj_body(is_prompt, tile, u_ref, wqkv_ref, wz_ref, wb_ref, wa_ref, al_ref, dt_ref,
                   qkv_o, z_o, beta_o, g_o):
    del is_prompt, tile
    u = u_ref[...].astype(BF)
    qkv_o[...] = jnp.dot(u, wqkv_ref[...], preferred_element_type=F32)
    z_o[...] = jnp.dot(u, wz_ref[...], preferred_element_type=F32)
    beta = _sigmoid(jnp.dot(u, wb_ref[...], preferred_element_type=F32))
    g = -jnp.exp(al_ref[...]) * _softplus(jnp.dot(u, wa_ref[...], preferred_element_type=F32) + dt_ref[...])
    for h in range(GDN_H):
        cols = slice(h * GDN_DV, (h + 1) * GDN_DV)
        beta_o[:, cols] = jnp.broadcast_to(beta[:, h:h + 1], (beta.shape[0], GDN_DV))
        g_o[:, cols] = jnp.broadcast_to(g[:, h:h + 1], (g.shape[0], GDN_DV))


def _gdn_post(conv, q_o, k_o, v_o):
    c = conv * _sigmoid(conv)
    hk = GDN_H * GDN_DK
    for h in range(GDN_H):
        qs = c[:, h * GDN_DK:(h + 1) * GDN_DK]
        ks = c[:, hk + h * GDN_DK:hk + (h + 1) * GDN_DK]
        q_o[:, h * GDN_DK:(h + 1) * GDN_DK] = qs * lax.rsqrt(jnp.sum(qs * qs, axis=1, keepdims=True) + 1e-6) * (GDN_DK ** -0.5)
        k_o[:, h * GDN_DK:(h + 1) * GDN_DK] = ks * lax.rsqrt(jnp.sum(ks * ks, axis=1, keepdims=True) + 1e-6)
    v_o[...] = c[:, 2 * hk:]


def _gdn_conv_prompt_body(x_ref, halo_ref, w_ref, q_o, k_o, v_o, *, tiles_per_seq):
    i = pl.program_id(0)
    x = x_ref[...]
    tm = x.shape[0]
    halo = jnp.where(i % tiles_per_seq == 0, 0.0, halo_ref[...])
    row8 = lax.broadcasted_iota(jnp.int32, (8, x.shape[1]), 0)
    w = w_ref[...]
    conv = w[GDN_CONV - 1:GDN_CONV] * x
    for j in range(GDN_CONV - 1):
        sh = GDN_CONV - 1 - j
        rolled = pltpu.roll(x, sh, 0)
        head = jnp.where(row8 < sh, pltpu.roll(halo, sh, 0), rolled[:8])
        tap = jnp.concatenate([head, rolled[8:]], axis=0) if tm > 8 else head
        conv = conv + w[j:j + 1] * tap
    _gdn_post(conv, q_o, k_o, v_o)


def _gdn_conv_taps_body(t0_ref, t1_ref, t2_ref, t3_ref, w_ref, q_o, k_o, v_o):
    w = w_ref[...]
    conv = w[0:1] * t0_ref[...] + w[1:2] * t1_ref[...] + w[2:3] * t2_ref[...] + w[3:4] * t3_ref[...]
    _gdn_post(conv, q_o, k_o, v_o)


def _gdn_advance(L, refs, rows, cols, st):
    q_ref, k_ref, v_ref, beta_ref, g_ref = refs
    P = LANES
    ii, jj = _iota2((L, L))
    strict = ii > jj
    incl = ii >= jj
    us = range(len(st))
    at = lambda ref, u: ref[rows[u], cols[u]]
    gc = [_cumsum_rows(at(g_ref, u)) for u in us]
    q = [at(q_ref, u) for u in us]
    k = [at(k_ref, u) for u in us]
    kk = [_dot_nt(k[u], k[u]) for u in us]
    qk = [_dot_nt(q[u], k[u]) for u in us]
    n_mats, dec_i, e_gc, rhs = [], [], [], []
    for u in us:
        diff = gc[u][:, :L] - gc[u].T[:L, :]
        beta = at(beta_ref, u)
        n_mats.append(-(beta[:, :L] * kk[u] * jnp.exp(jnp.where(strict, diff, NEG))))
        dec_i.append(jnp.exp(jnp.where(incl, diff, NEG)))
        e_gc.append(jnp.exp(gc[u]))
        rhs.append(jnp.concatenate([beta * at(v_ref, u), beta * e_gc[u] * k[u]], axis=1))
    side = GDN_INV_SIDE
    assert len(st) % side == 0
    t_rows = _tri_inv([jnp.concatenate(n_mats[c:c + side], axis=1) for c in range(0, len(st), side)], L)
    t_inv = [t_rows[u // side][:, (u % side) * L:(u % side + 1) * L] for u in us]
    w = [_dot(t_inv[u], rhs[u]) for u in us]
    uu = [w[u][:, :P] - _dot(w[u][:, P:], st[u]) for u in us]
    qs = [_dot(q[u] * e_gc[u], st[u]) for u in us]
    intra = [_dot(qk[u] * dec_i[u], uu[u]) for u in us]
    upd = [_dot_tn(k[u] * jnp.exp(gc[u][L - 1:L, :] - gc[u]), uu[u]) for u in us]
    return ([qs[u] + intra[u] for u in us], [jnp.exp(gc[u][L - 1:L, :]) * st[u] + upd[u] for u in us])


def _gdn_chunk_body(q_ref, k_ref, v_ref, beta_ref, g_ref, o_ref, s_ref, st_sc, *, tb, groups):
    L, P = CHUNK, LANES
    t = pl.program_id(2)

    @pl.when(t == 0)
    def _():
        st_sc[...] = jnp.zeros(st_sc.shape, F32)

    refs = (q_ref, k_ref, v_ref, beta_ref, g_ref)
    cols = [slice(g * P, (g + 1) * P) for g in range(groups)]

    def chunk(c, carry):
        rows = pl.ds(pl.multiple_of(c * L, L), L)
        outs, new = _gdn_advance(L, refs, [rows] * groups, cols, [st_sc[g] for g in range(groups)])
        for g in range(groups):
            o_ref[rows, cols[g]] = outs[g]
            st_sc[g] = new[g]
        return carry

    lax.fori_loop(0, tb // L, chunk, 0, unroll=2)

    @pl.when(t == pl.num_programs(2) - 1)
    def _():
        for g in range(groups):
            s_ref[0, g] = st_sc[g]


def _gdn_short_body(q_ref, k_ref, v_ref, beta_ref, g_ref, s0_ref, o_ref, s_ref, *, bb, length):
    P = LANES
    units = [(b_i, h) for b_i in range(bb) for h in range(GDN_H)]
    rows = [slice(b_i * length, (b_i + 1) * length) for b_i, _ in units]
    cols = [slice(h * P, (h + 1) * P) for _, h in units]
    outs, new = _gdn_advance(length, (q_ref, k_ref, v_ref, beta_ref, g_ref), rows, cols,
                             [s0_ref[b_i, h] for b_i, h in units])
    for u, (b_i, h) in enumerate(units):
        o_ref[rows[u], cols[u]] = outs[u]
        s_ref[b_i, h] = new[u]


def _gdn_out_body(is_prompt, tile, x_ref, o_ref, z_ref, nw_ref, wo_ref, y_ref):
    del is_prompt, tile
    z = z_ref[...]
    nw = nw_ref[...]
    o = o_ref[...]
    parts = []
    for h in range(GDN_H):
        sl = slice(h * GDN_DV, (h + 1) * GDN_DV)
        parts.append(_rms(o[:, sl], nw, NORM_EPS))
    on = jnp.concatenate(parts, axis=1) * (z * _sigmoid(z))
    y_ref[...] = x_ref[...] + _dot(on, wo_ref[...])


def _gdn_layer(x, u, dims, conv_s, state_s, p):
    nb_p, seq_p, nb_s, seq_s = dims
    n_p, n_s = nb_p * seq_p, nb_s * seq_s
    hv = GDN_H * GDN_DV
    w_in = p['w_in']
    pad = lambda w: jnp.pad(w, ((0, 0), (0, LANES - w.shape[-1])))
    (qkv_p, qkv_s), z, (beta_p, beta_s), (g_p, g_s) = _rows_call(
        _gdn_proj_body, [u],
        [w_in[:, :GDN_QKV].astype(BF), w_in[:, GDN_QKV:GDN_QKV + hv].astype(BF),
         pad(w_in[:, GDN_QKV + hv:GDN_QKV + hv + GDN_H]).astype(BF), pad(w_in[:, GDN_QKV + hv + GDN_H:]).astype(BF),
         pad(p['a_log'][None]), pad(p['dt_bias'][None])],
        [(GDN_QKV, F32), (hv, F32), (hv, F32), (hv, F32)], 256, "gdn_proj")
    tm = _row_tile(seq_p, 256)
    cw = p['conv_w']
    outs3 = [jax.ShapeDtypeStruct((n_p, hv), F32)] * 3
    q_p, k_p, v_p = pl.pallas_call(
        functools.partial(_gdn_conv_prompt_body, tiles_per_seq=seq_p // tm), grid=(n_p // tm,),
        in_specs=[pl.BlockSpec((tm, GDN_QKV), lambda i: (i, 0)),
                  pl.BlockSpec((8, GDN_QKV), lambda i: (jnp.maximum(i * (tm // 8) - 1, 0), 0)),
                  pl.BlockSpec((GDN_CONV, GDN_QKV), lambda i: (0, 0))],
        out_specs=[pl.BlockSpec((tm, hv), lambda i: (i, 0))] * 3, out_shape=outs3, name="gdn_conv_prompt",
        compiler_params=pltpu.CompilerParams(dimension_semantics=("parallel",), vmem_limit_bytes=VMEM_LIMIT),
    )(qkv_p, qkv_p, cw)
    xp_s = jnp.concatenate([conv_s, qkv_s.reshape(nb_s, seq_s, GDN_QKV)], axis=1)
    taps = [xp_s[:, j:j + seq_s].reshape(n_s, GDN_QKV) for j in range(GDN_CONV)]
    ts = _row_tile(n_s, 256)
    q_s, k_s, v_s = pl.pallas_call(
        _gdn_conv_taps_body, grid=(n_s // ts,),
        in_specs=[pl.BlockSpec((ts, GDN_QKV), lambda i: (i, 0))] * GDN_CONV
        + [pl.BlockSpec((GDN_CONV, GDN_QKV), lambda i: (0, 0))],
        out_specs=[pl.BlockSpec((ts, hv), lambda i: (i, 0))] * 3,
        out_shape=[jax.ShapeDtypeStruct((n_s, hv), F32)] * 3, name="gdn_conv_sample",
        compiler_params=pltpu.CompilerParams(dimension_semantics=("parallel",), vmem_limit_bytes=VMEM_LIMIT),
    )(*taps, cw)
    groups = GDN_GROUPS
    tb = _row_tile(seq_p, SEQ_BLOCK)
    nt = seq_p // tb
    spec = pl.BlockSpec((tb, groups * LANES), lambda b, h, t: (b * nt + t, h))
    o_p, s_p = pl.pallas_call(
        functools.partial(_gdn_chunk_body, tb=tb, groups=groups), grid=(nb_p, GDN_H // groups, nt),
        in_specs=[spec] * 5,
        out_specs=[spec, pl.BlockSpec((1, groups, GDN_DK, GDN_DV), lambda b, h, t: (b, h, 0, 0))],
        out_shape=[jax.ShapeDtypeStruct((n_p, hv), F32), jax.ShapeDtypeStruct((nb_p, GDN_H, GDN_DK, GDN_DV), F32)],
        scratch_shapes=[pltpu.VMEM((groups, GDN_DK, GDN_DV), F32)],
        name="gdn_chunk",
        compiler_params=pltpu.CompilerParams(dimension_semantics=("parallel", "parallel", "arbitrary"),
                                             vmem_limit_bytes=VMEM_LIMIT),
    )(q_p, k_p, v_p, beta_p, g_p)
    length = SHORT_LEN * pl.cdiv(seq_s, SHORT_LEN)
    pad_rows = lambda t: jnp.pad(t.reshape(nb_s, seq_s, hv), ((0, 0), (0, length - seq_s), (0, 0))).reshape(nb_s * length, hv)
    bb = _row_tile(nb_s, SHORT_ROWS)
    sspec = pl.BlockSpec((bb * length, hv), lambda b: (b, 0))
    stspec = pl.BlockSpec((bb, GDN_H, GDN_DK, GDN_DV), lambda b: (b, 0, 0, 0))
    o_s, s_s = pl.pallas_call(
        functools.partial(_gdn_short_body, bb=bb, length=length), grid=(nb_s // bb,),
        in_specs=[sspec] * 5 + [stspec], out_specs=[sspec, stspec],
        out_shape=[jax.ShapeDtypeStruct((nb_s * length, hv), F32),
                   jax.ShapeDtypeStruct((nb_s, GDN_H, GDN_DK, GDN_DV), F32)],
        name="gdn_short",
        compiler_params=pltpu.CompilerParams(dimension_semantics=("parallel",), vmem_limit_bytes=VMEM_LIMIT),
    )(pad_rows(q_s), pad_rows(k_s), pad_rows(v_s), pad_rows(beta_s), pad_rows(g_s), state_s)
    o_s = o_s.reshape(nb_s, length, hv)[:, :seq_s].reshape(n_s, hv)
    x = _rows_call(_gdn_out_body, [x, (o_p, o_s), z], [p['norm_w'][None], p['w_o'].astype(BF)],
                   [(D_MODEL, F32)], 512, "gdn_out")[0]
    keep = GDN_CONV - 1
    tail_p = qkv_p.reshape(nb_p, seq_p, GDN_QKV)[:, -keep:]
    conv_p = jnp.pad(tail_p, ((0, 0), (max(keep - seq_p, 0), 0), (0, 0)))
    return x, (s_p, conv_p, s_s, xp_s[:, -keep:])


def kernel(x_prompt, x_sample, state_rwkv_wkv, state_rwkv_shift, cache_attn_k, cache_attn_v, state_gdn, state_gdn_conv, page_table, norm_w, final_norm_w, ffn_w_gate, ffn_w_up, ffn_w_down, rwkv_mu, rwkv_w_rkv, rwkv_w_o, rwkv_w0, rwkv_w1, rwkv_w2, rwkv_a0, rwkv_a1, rwkv_a2, rwkv_g1, rwkv_g2, rwkv_k_k, rwkv_k_a, rwkv_r_k, rwkv_lnx_w, rwkv_lnx_b, rwkv_v0, rwkv_v1, rwkv_v2, attn_w_qkv, attn_w_o, attn_lambda, attn_subln_w, rel_bias, gdn_w_in, gdn_conv_w, gdn_a_log, gdn_dt_bias, gdn_norm_w, gdn_w_o):
    nb_p, seq_p, _ = x_prompt.shape
    nb_s, seq_s, _ = x_sample.shape
    dims = (nb_p, seq_p, nb_s, seq_s)
    depth = norm_w.shape[0]
    x = (x_prompt.reshape(nb_p * seq_p, D_MODEL), x_sample.reshape(nb_s * seq_s, D_MODEL))
    v_first = None
    rw, at, gd = [], [], []
    for i in range(depth):
        kind, j = i % 3, i // 3
        x, u = _ffn(x, norm_w[i, 0], ffn_w_gate[i, 0], ffn_w_up[i, 0], ffn_w_down[i, 0], norm_w[i, 1])
        if kind == 0:
            p = dict(mu=rwkv_mu[j], w_rkv=rwkv_w_rkv[j], w_o=rwkv_w_o[j], w0=rwkv_w0[j], w1=rwkv_w1[j],
                     w2=rwkv_w2[j], a0=rwkv_a0[j], a1=rwkv_a1[j], a2=rwkv_a2[j], g1=rwkv_g1[j], g2=rwkv_g2[j],
                     k_k=rwkv_k_k[j], k_a=rwkv_k_a[j], r_k=rwkv_r_k[j], lnx_w=rwkv_lnx_w[j], lnx_b=rwkv_lnx_b[j])
            vres = None if j == 0 else (rwkv_v0[j - 1], rwkv_v1[j - 1], rwkv_v2[j - 1])
            x, v_first, st = _rwkv_layer(x, u, dims, state_rwkv_shift[j], state_rwkv_wkv[j], p, v_first, vres)
            rw.append(st)
        elif kind == 1:
            p = dict(w_qkv=attn_w_qkv[j], w_o=attn_w_o[j], lam=attn_lambda[j], subln_w=attn_subln_w[j])
            lam_init = 0.8 - 0.6 * math.exp(-0.3 * i)
            x, st = _attn_layer(x, u, dims, cache_attn_k, cache_attn_v, j, page_table, p, lam_init, rel_bias)
            at.append(st)
        else:
            p = dict(w_in=gdn_w_in[j], conv_w=gdn_conv_w[j], a_log=gdn_a_log[j], dt_bias=gdn_dt_bias[j],
                     norm_w=gdn_norm_w[j], w_o=gdn_w_o[j])
            x, st = _gdn_layer(x, u, dims, state_gdn_conv[j], state_gdn[j], p)
            gd.append(st)
        if i == depth - 1:
            x, y = _ffn(x, norm_w[i, 2], ffn_w_gate[i, 1], ffn_w_up[i, 1], ffn_w_down[i, 1], final_norm_w)
        else:
            x = _ffn(x, norm_w[i, 2], ffn_w_gate[i, 1], ffn_w_up[i, 1], ffn_w_down[i, 1])
    stack = lambda lst, k: jnp.stack([t[k] for t in lst])
    return (y[0].reshape(x_prompt.shape), y[1].reshape(x_sample.shape),
            stack(rw, 0), stack(rw, 1), stack(at, 0), stack(at, 1), stack(gd, 0), stack(gd, 1),
            stack(rw, 2), stack(rw, 3), stack(at, 2), stack(at, 3), stack(gd, 2), stack(gd, 3))
```

```python
import functools
import math

import jax
import jax.numpy as jnp
from jax import lax
from jax.experimental import pallas as pl
from jax.experimental.pallas import tpu as pltpu

F32 = jnp.float32
BF = jnp.bfloat16

D_MODEL = 1024
NORM_EPS = 1e-6
RWKV_N = 64
RWKV_H = D_MODEL // RWKV_N
RWKV_LNX_EPS = 64e-5
DIFF_H = 8
DIFF_DH = 64
DIFF_E = 128
SUBLN_EPS = 1e-5
REL_BUCKETS = 32
REL_MAX_EXACT = 16
REL_MAX_DIST = 128
GDN_H = 8
GDN_DK = 128
GDN_DV = 128
GDN_CONV = 4
GDN_QKV = GDN_H * (2 * GDN_DK + GDN_DV)
FFN_TILE = 1024
FFN_CHUNK_BIG_TILE = 256
FFN_CHUNK_SMALL_TILE = 1408
CHUNK = 64
SEQ_BLOCK = 512
RWKV_GROUPS = 8
GDN_GROUPS = 8
GDN_INV_SIDE = 1
RWKV_INV_UNITS = 1
LANES = 128
NEG = -1e30
LOG2E = math.log2(math.e)
VMEM_LIMIT = 56 * 1024 * 1024


def _dot(a, b):
    return jnp.dot(a.astype(BF), b.astype(BF), preferred_element_type=F32)


def _dot_nt(a, b):
    return lax.dot_general(a.astype(BF), b.astype(BF), (((1,), (1,)), ((), ())), preferred_element_type=F32)


def _dot_tn(a, b):
    return lax.dot_general(a.astype(BF), b.astype(BF), (((0,), (0,)), ((), ())), preferred_element_type=F32)


def _rms(x, g, eps):
    return x * lax.rsqrt(jnp.mean(x * x, axis=-1, keepdims=True) + eps) * g


def _sigmoid(x):
    return 1.0 / (1.0 + jnp.exp(-x))


def _softplus(x):
    return jnp.maximum(x, 0.0) + jnp.log(1.0 + jnp.exp(-jnp.abs(x)))


def _iota2(shape):
    return lax.broadcasted_iota(jnp.int32, shape, 0), lax.broadcasted_iota(jnp.int32, shape, 1)


def _split(x, n):
    out = []
    for _ in range(n):
        h = x.astype(BF)
        out.append(h)
        x = x - h.astype(F32)
    return out


def _dot_sel(x, sel, n=2):
    return sum(jnp.dot(t, sel, preferred_element_type=F32) for t in _split(x, n))


def _segment_allsum(x, e, et):
    return _dot_sel(_dot_sel(x, e), et)


def _cumsum_rows(x):
    n = x.shape[0]
    row = lax.broadcasted_iota(jnp.int32, x.shape, 0)
    shift = 1
    while shift < n:
        x = x + jnp.where(row >= shift, pltpu.roll(x, shift, 0), 0.0)
        shift *= 2
    return x


def _block_diag(m_row, size):
    mb = m_row.astype(BF)
    blk = lax.broadcasted_iota(jnp.int32, mb.shape, 1) >> (size.bit_length() - 1)
    return jnp.concatenate([jnp.where(blk == h, mb, jnp.zeros_like(mb)) for h in range(mb.shape[1] // size)], axis=0)


def _tri_inv(n_rows, size):
    ii, jj = _iota2(n_rows[0].shape)
    jj = jj & (size - 1)
    eye = (ii == jj).astype(F32)
    xs = [eye + jnp.where((ii >> 1) == (jj >> 1), n, 0.0) for n in n_rows]
    lvl = 1
    while (2 << lvl) <= size:
        sel = ((ii >> (lvl + 1)) == (jj >> (lvl + 1))) & (((ii >> lvl) & 1) == 1) & (((jj >> lvl) & 1) == 0)
        ys = [_dot(x, _block_diag(jnp.where(sel, n, 0.0), size)) for x, n in zip(xs, n_rows)]
        xs = [x + _dot(y, _block_diag(x, size)) for x, y in zip(xs, ys)]
        lvl += 1
    return xs


def _row_tile(n, want):
    tm = min(want, n)
    while n % tm:
        tm //= 2
    return tm


HALO = 8
SAMPLE_TILE = 128


def _rows_call(body, rows, consts, outs, tm, name, halo=()):
    n_p, n_s = rows[0][0].shape[0], rows[0][1].shape[0]
    tm_p, tm_s = _row_tile(n_p, tm), _row_tile(n_s, min(tm, SAMPLE_TILE))
    tp, ts = n_p // tm_p, n_s // tm_s
    p_idx = lambda i: jnp.minimum(i, tp - 1)
    s_idx = lambda i: jnp.maximum(i - tp, 0)
    in_specs, args = [], []
    for k, (a_p, a_s) in enumerate(rows):
        if k in halo:
            in_specs.append(pl.BlockSpec((HALO, a_p.shape[1]),
                                         lambda i: (jnp.maximum(p_idx(i) * (tm_p // HALO) - 1, 0), 0)))
        else:
            in_specs.append(pl.BlockSpec((tm_p, a_p.shape[1]), lambda i: (p_idx(i), 0)))
        in_specs.append(pl.BlockSpec((tm_s, a_s.shape[1]), lambda i: (s_idx(i), 0)))
        args += [a_p, a_s]
    for a in consts:
        in_specs.append(pl.BlockSpec(a.shape, lambda i, nd=a.ndim: (0,) * nd, pipeline_mode=pl.Buffered(1)))
    out_specs, out_shape = [], []
    for c, dt in outs:
        out_specs += [pl.BlockSpec((tm_p, c), lambda i: (p_idx(i), 0)), pl.BlockSpec((tm_s, c), lambda i: (s_idx(i), 0))]
        out_shape += [jax.ShapeDtypeStruct((n_p, c), dt), jax.ShapeDtypeStruct((n_s, c), dt)]
    nr, nc = len(rows), len(consts)

    def kern(*refs):
        i = pl.program_id(0)
        cs = refs[2 * nr:2 * nr + nc]
        o = refs[2 * nr + nc:]

        @pl.when(i < tp)
        def _():
            body(True, i, *refs[0:2 * nr:2], *cs, *o[0::2])

        @pl.when(i >= tp)
        def _():
            body(False, i - tp, *refs[1:2 * nr:2], *cs, *o[1::2])

    res = pl.pallas_call(
        kern, grid=(tp + ts,), in_specs=in_specs, out_specs=out_specs, out_shape=out_shape, name=name,
        compiler_params=pltpu.CompilerParams(dimension_semantics=("arbitrary",), vmem_limit_bytes=VMEM_LIMIT),
    )(*args, *consts)
    return [(res[2 * k], res[2 * k + 1]) for k in range(len(outs))]


def _ffn_body(is_prompt, tile, x_ref, g_ref, wg_ref, wu_ref, wd_ref, *rest, normed):
    del is_prompt, tile
    (g2_ref, o_ref, u_ref) = rest if normed else (None, rest[0], None)
    x = x_ref[...]
    h = _rms(x, g_ref[...], NORM_EPS).astype(BF)
    hidden = wg_ref.shape[1]
    want = FFN_CHUNK_BIG_TILE if x.shape[0] >= FFN_TILE else FFN_CHUNK_SMALL_TILE
    nchunk = max(n for n in range(1, hidden // LANES + 1) if hidden % (n * LANES) == 0 and hidden // n >= want)
    fc = hidden // nchunk
    acc = jnp.zeros_like(x)
    for c in range(nchunk):
        sl = slice(c * fc, (c + 1) * fc)
        gate = jnp.dot(h, wg_ref[:, sl], preferred_element_type=F32)
        up = jnp.dot(h, wu_ref[:, sl], preferred_element_type=F32)
        act = (gate * _sigmoid(gate) * up).astype(BF)
        acc = acc + jnp.dot(act, wd_ref[sl, :], preferred_element_type=F32)
    y = x + 0.5 * acc
    o_ref[...] = y
    if normed:
        u_ref[...] = _rms(y, g2_ref[...], NORM_EPS)


def _ffn(x, g, wg, wu, wd, g_next=None):
    normed = g_next is not None
    consts = [g[None], wg.astype(BF), wu.astype(BF), wd.astype(BF)] + ([g_next[None]] if normed else [])
    outs = _rows_call(functools.partial(_ffn_body, normed=normed), [x], consts,
                      [(D_MODEL, F32)] * (2 if normed else 1), FFN_TILE, "ffn_half")
    return outs if normed else outs[0]


def _linear_body(is_prompt, tile, *refs, n_out, residual):
    del is_prompt, tile
    if residual:
        x_ref, u_ref, w_ref = refs[:3]
        outs = refs[3:]
    else:
        u_ref, w_ref = refs[:2]
        outs = refs[2:]
    y = jnp.dot(u_ref[...].astype(BF), w_ref[...], preferred_element_type=F32)
    if residual:
        outs[0][...] = x_ref[...] + y
    else:
        c = y.shape[1] // n_out
        for k, o in enumerate(outs):
            o[...] = y[:, k * c:(k + 1) * c]


def _linear(u, w, n_out, name):
    body = functools.partial(_linear_body, n_out=n_out, residual=False)
    c = w.shape[1] // n_out
    return _rows_call(body, [u], [w.astype(BF)], [(c, F32)] * n_out, 512, name)


def _linear_res(x, u, w, name):
    body = functools.partial(_linear_body, n_out=1, residual=True)
    return _rows_call(body, [x, u], [w.astype(BF)], [(w.shape[1], F32)], 512, name)[0]


def _rwkv_proj_body(is_prompt, tile, *refs, has_vres, tiles_per_seq):
    (u_ref, prev_ref) = refs[:2]
    k0 = 2
    if has_vres:
        vf_ref = refs[2]
        k0 = 3
    (mu_ref, wr_ref, wk_ref, wv_ref, w0_ref, w1_ref, w2_ref, a0_ref, a1_ref, a2_ref, g1_ref, g2_ref,
     kk_ref, ka_ref, rk_ref, e_ref, et_ref) = refs[k0:k0 + 17]
    k1 = k0 + 17
    if has_vres:
        v0_ref, v1_ref, v2_ref = refs[k1:k1 + 3]
        k1 += 3
    r_o, lw_o, k_o, v_o, a_o, b_o, g_o, bon_o = refs[k1:]
    u = u_ref[...]
    if is_prompt:
        before = jnp.where(tile % tiles_per_seq == 0, 0.0, prev_ref[HALO - 1:HALO, :])
        row = lax.broadcasted_iota(jnp.int32, u.shape, 0)
        prev = jnp.where(row == 0, before, pltpu.roll(u, 1, 0))
    else:
        prev = prev_ref[...]
    xx = prev - u
    mu = mu_ref[...]
    xr, xw, xk = u + xx * mu[0:1], u + xx * mu[1:2], u + xx * mu[2:3]
    xv, xa, xg = u + xx * mu[3:4], u + xx * mu[4:5], u + xx * mu[5:6]
    r = _dot(xr, wr_ref[...])
    k = _dot(xk, wk_ref[...])
    v = _dot(xv, wv_ref[...])
    wlog = -_softplus(-(w0_ref[...] + _dot(jnp.tanh(_dot(xw, w1_ref[...])), w2_ref[...]))) - 0.5
    lw_o[...] = -jnp.exp(wlog)
    if has_vres:
        v = v + (vf_ref[...] - v) * _sigmoid(v0_ref[...] + _dot(_dot(xv, v1_ref[...]), v2_ref[...]))
    a = _sigmoid(a0_ref[...] + _dot(_dot(xa, a1_ref[...]), a2_ref[...]))
    g_o[...] = _dot(_sigmoid(_dot(xg, g1_ref[...])), g2_ref[...])
    kk = k * kk_ref[...]
    e, et = e_ref[...], et_ref[...]
    kk = kk * lax.rsqrt(_segment_allsum(kk * kk, e, et) + 1e-24)
    k_mod = k * (1.0 + (a - 1.0) * ka_ref[...])
    r_o[...] = r
    k_o[...] = k_mod
    v_o[...] = v
    a_o[...] = -kk
    b_o[...] = kk * a
    bon_o[...] = _segment_allsum(r * k_mod * rk_ref[...], e, et) * v


def _rwkv_advance(L, refs, rows, cols, st):
    r_ref, lw_ref, k_ref, v_ref, a_ref, b_ref = refs
    P, N = LANES, RWKV_N
    assert len(st) % RWKV_INV_UNITS == 0
    ii, jj = _iota2((2 * L, 4 * L))
    jj = jj & (L - 1)
    mask4 = (((ii < L) & (ii > jj)) | ((ii >= L) & (ii - L >= jj))).astype(F32)
    lane = lax.broadcasted_iota(jnp.int32, (L, P), 1)
    bi, bj = _iota2((P, P))
    bd = ((bi >= N) == (bj >= N)).astype(F32)
    us = range(len(st))
    head_lanes = [(lane >= N) == bool(h) for h in range(2)]

    def by_head(t):
        return jnp.concatenate([jnp.where(m, t, jnp.zeros_like(t)) for m in head_lanes], axis=0)

    x, wt, v, bk_end, w_end = [], [], [], [], []
    for u in us:
        at = lambda ref: ref[rows[u], cols[u]]
        lw = at(lw_ref)
        gc = _cumsum_rows(lw)
        gl = gc[L - 1:L, :]
        r, k, a, b = at(r_ref), at(k_ref), at(a_ref), at(b_ref)
        e_neg = jnp.exp(-gc)
        e_end = jnp.exp(gl - gc)
        x.append(jnp.concatenate([a * jnp.exp(gc - lw), r * jnp.exp(gc)], axis=0).astype(BF))
        wt.append(jnp.concatenate([by_head((b * e_neg).astype(BF)), by_head((k * e_neg).astype(BF))], axis=0))
        v.append(at(v_ref))
        bk_end.append(jnp.concatenate([b * e_end, k * e_end], axis=0))
        w_end.append(jnp.exp(gl))
    xs = [_dot_nt(x[u], st[u]) for u in us]
    abk = [_dot_nt(x[u], wt[u]) * mask4 for u in us]
    side = RWKV_INV_UNITS
    t_rows = _tri_inv([jnp.concatenate([abk[u + d][:L, :2 * L] for d in range(side)], axis=1) for u in us[::side]], L)
    t_inv = [t_rows[u // side][:, (u % side) * 2 * L:(u % side + 1) * 2 * L] for u in us]
    akv = [_dot(abk[u][:, 2 * L:], by_head(v[u])) for u in us]
    rhs = [xs[u][:L] + akv[u][:L] for u in us]
    uu = [_dot(t_inv[u], by_head(rhs[u])) for u in us]
    o_u = [_dot(abk[u][L:, :2 * L], by_head(uu[u])) for u in us]
    upd = [_dot_tn(jnp.concatenate([uu[u], v[u]], axis=0), bk_end[u]) for u in us]
    outs = [xs[u][L:] + akv[u][L:] + o_u[u] for u in us]
    return outs, [st[u] * w_end[u] + bd * upd[u] for u in us]


def _rwkv_chunk_body(r_ref, lw_ref, k_ref, v_ref, a_ref, b_ref, o_ref, s_ref, st_sc, *, tb, groups):
    L, P, N = CHUNK, LANES, RWKV_N
    t = pl.program_id(2)

    @pl.when(t == 0)
    def _():
        st_sc[...] = jnp.zeros(st_sc.shape, F32)

    refs = (r_ref, lw_ref, k_ref, v_ref, a_ref, b_ref)
    cols = [slice(g * P, (g + 1) * P) for g in range(groups)]

    def chunk(c, carry):
        rows = pl.ds(pl.multiple_of(c * L, L), L)
        outs, new = _rwkv_advance(L, refs, [rows] * groups, cols, [st_sc[g] for g in range(groups)])
        for g in range(groups):
            o_ref[rows, cols[g]] = outs[g]
            st_sc[g] = new[g]
        return carry

    lax.fori_loop(0, tb // L, chunk, 0, unroll=2)

    @pl.when(t == pl.num_programs(2) - 1)
    def _():
        for g in range(groups):
            s = st_sc[g]
            s_ref[0, 2 * g] = s[:N, :N]
            s_ref[0, 2 * g + 1] = s[N:, N:]


def _rwkv_chunk(seqs, nb, seq):
    groups = RWKV_GROUPS
    width = groups * LANES
    tb = _row_tile(seq, SEQ_BLOCK)
    nt = seq // tb
    spec = pl.BlockSpec((tb, width), lambda b, h, t: (b * nt + t, h))
    return pl.pallas_call(
        functools.partial(_rwkv_chunk_body, tb=tb, groups=groups), grid=(nb, D_MODEL // width, nt),
        in_specs=[spec] * 6,
        out_specs=[spec, pl.BlockSpec((1, 2 * groups, RWKV_N, RWKV_N), lambda b, h, t: (b, h, 0, 0))],
        out_shape=[jax.ShapeDtypeStruct((nb * seq, D_MODEL), F32),
                   jax.ShapeDtypeStruct((nb, RWKV_H, RWKV_N, RWKV_N), F32)],
        scratch_shapes=[pltpu.VMEM((groups, LANES, LANES), F32)],
        name="rwkv_chunk",
        compiler_params=pltpu.CompilerParams(dimension_semantics=("parallel", "parallel", "arbitrary"),
                                             vmem_limit_bytes=VMEM_LIMIT),
    )(*seqs)


def _rwkv_short_body(r_ref, lw_ref, k_ref, v_ref, a_ref, b_ref, s0_ref, o_ref, s_ref, *, bb, length):
    P, N = LANES, RWKV_N
    groups = D_MODEL // P
    units = [(b_i, g) for b_i in range(bb) for g in range(groups)]
    zero = jnp.zeros((N, N), F32)
    st = [jnp.concatenate([jnp.concatenate([s0_ref[b_i, 2 * g], zero], axis=1),
                           jnp.concatenate([zero, s0_ref[b_i, 2 * g + 1]], axis=1)], axis=0) for b_i, g in units]
    rows = [slice(b_i * length, (b_i + 1) * length) for b_i, _ in units]
    cols = [slice(g * P, (g + 1) * P) for _, g in units]
    outs, new = _rwkv_advance(length, (r_ref, lw_ref, k_ref, v_ref, a_ref, b_ref), rows, cols, st)
    for u, (b_i, g) in enumerate(units):
        o_ref[rows[u], cols[u]] = outs[u]
        s_ref[b_i, 2 * g] = new[u][:N, :N]
        s_ref[b_i, 2 * g + 1] = new[u][N:, N:]


SHORT_LEN = 8
SHORT_ROWS = 4


def _rwkv_short(seqs, states, layer, nb, steps):
    length = SHORT_LEN * pl.cdiv(steps, SHORT_LEN)
    pad = lambda t: jnp.pad(t.reshape(nb, steps, D_MODEL), ((0, 0), (0, length - steps), (0, 0))).reshape(nb * length, D_MODEL)
    bb = _row_tile(nb, SHORT_ROWS)
    first = layer * (nb // bb)
    spec = pl.BlockSpec((bb * length, D_MODEL), lambda b: (b, 0))
    sspec = pl.BlockSpec((bb, RWKV_H, RWKV_N, RWKV_N), lambda b: (b, 0, 0, 0))
    o, s = pl.pallas_call(
        functools.partial(_rwkv_short_body, bb=bb, length=length), grid=(nb // bb,),
        in_specs=[spec] * 6 + [pl.BlockSpec((bb, RWKV_H, RWKV_N, RWKV_N), lambda b: (first + b, 0, 0, 0))],
        out_specs=[spec, sspec],
        out_shape=[jax.ShapeDtypeStruct((nb * length, D_MODEL), F32),
                   jax.ShapeDtypeStruct((nb, RWKV_H, RWKV_N, RWKV_N), F32)],
        name="rwkv_short",
        compiler_params=pltpu.CompilerParams(dimension_semantics=("parallel",), vmem_limit_bytes=VMEM_LIMIT),
    )(*[pad(t) for t in seqs], states.reshape(-1, RWKV_H, RWKV_N, RWKV_N))
    return o.reshape(nb, length, D_MODEL)[:, :steps].reshape(nb * steps, D_MODEL), s


def _rwkv_out_body(is_prompt, tile, x_ref, o_ref, g_ref, bon_ref, lw_ref, lb_ref, e_ref, et_ref, wo_ref, y_ref):
    del is_prompt, tile
    e, et = e_ref[...], et_ref[...]
    o = o_ref[...]
    inv_n = 1.0 / RWKV_N
    d = o - _segment_allsum(o, e, et) * inv_n
    var = _segment_allsum(d * d, e, et) * inv_n
    on = d * lax.rsqrt(var + RWKV_LNX_EPS) * lw_ref[...] + lb_ref[...]
    y_ref[...] = x_ref[...] + _dot((on + bon_ref[...]) * g_ref[...], wo_ref[...])


def _rwkv_layer(x, u, dims, shift_s, wkv_states, layer, p, v_first, vres):
    nb_p, seq_p, nb_s, seq_s = dims
    u_p, u_s = u
    u3s = u_s.reshape(nb_s, seq_s, D_MODEL)
    prev_s = jnp.concatenate([shift_s[:, None, :], u3s[:, :-1]], axis=1).reshape(nb_s * seq_s, D_MODEL)
    has_vres = vres is not None
    e = (jnp.arange(D_MODEL)[:, None] // RWKV_N == jnp.arange(RWKV_H)[None, :]).astype(BF)
    et = e.T
    tm = _row_tile(seq_p, 256)
    rows = [u, (u_p, prev_s)] + ([v_first] if has_vres else [])
    consts = [p['mu'], p['w_rkv'][0].astype(BF), p['w_rkv'][1].astype(BF), p['w_rkv'][2].astype(BF),
              p['w0'][None], p['w1'].astype(BF), p['w2'].astype(BF), p['a0'][None], p['a1'].astype(BF),
              p['a2'].astype(BF), p['g1'].astype(BF), p['g2'].astype(BF), p['k_k'][None], p['k_a'][None],
              p['r_k'].reshape(1, D_MODEL), e, et]
    if has_vres:
        consts += [vres[0][None], vres[1].astype(BF), vres[2].astype(BF)]
    body = functools.partial(_rwkv_proj_body, has_vres=has_vres, tiles_per_seq=seq_p // tm)
    r, lw, k, v, a, b, gate, bonus = _rows_call(body, rows, consts, [(D_MODEL, F32)] * 8, tm, "rwkv_proj", halo=(1,))
    seqs = (r, lw, k, v, a, b)
    o_p, s_p = _rwkv_chunk([t[0] for t in seqs], nb_p, seq_p)
    o_s, s_s = _rwkv_short([t[1] for t in seqs], wkv_states, layer, nb_s, seq_s)
    x = _rows_call(_rwkv_out_body, [x, (o_p, o_s), gate, bonus],
                   [p['lnx_w'][None], p['lnx_b'][None], e, et, p['w_o'].astype(BF)], [(D_MODEL, F32)], 512,
                   "rwkv_out")[0]
    u3p = u_p.reshape(nb_p, seq_p, D_MODEL)
    return x, (v if not has_vres else v_first), (s_p, u3p[:, -1], s_s, u3s[:, -1])


def _t5_bias(dist, rel_bias):
    n = jnp.maximum(dist, 0)
    nf = jnp.maximum(n, 1).astype(F32)
    large = REL_MAX_EXACT + (jnp.log(nf / REL_MAX_EXACT) / math.log(REL_MAX_DIST / REL_MAX_EXACT)
                             * (REL_BUCKETS - REL_MAX_EXACT)).astype(jnp.int32)
    large = jnp.minimum(large, REL_BUCKETS - 1)
    bucket = jnp.where(n < REL_MAX_EXACT, n, large)
    onehot = (bucket[..., None] == jnp.arange(REL_BUCKETS)).astype(F32)
    return jnp.einsum('...b,bh->...h', onehot, rel_bias.astype(F32), precision=lax.Precision.HIGHEST)


def _lam_of(lam_ref, lam_init):
    lv = lam_ref[...]
    return (jnp.exp(jnp.sum(lv[0:1] * lv[1:2], axis=1, keepdims=True))
            - jnp.exp(jnp.sum(lv[2:3] * lv[3:4], axis=1, keepdims=True)) + lam_init)


ATTN_HEADS = 2


def _attn_prompt_body(q_ref, k_ref, v_ref, bias_ref, far_ref, lam_ref, sw_ref, o_ref, kb_sc, vt_sc, *, tq, lam_init):
    qi = pl.program_id(2)
    E, dh = DIFF_E, DIFF_DH
    nkv = kb_sc.shape[0]
    chains = [(hh, c) for hh in range(ATTN_HEADS) for c in range(2)]

    @pl.when(qi == 0)
    def _():
        for j in range(nkv):
            kb_sc[j] = k_ref[j * tq:(j + 1) * tq, :].astype(BF)
            vt_sc[j] = v_ref[j * tq:(j + 1) * tq, :].T.astype(BF)

    q_t = (q_ref[...] * (dh ** -0.5 * LOG2E)).T
    row = lax.broadcasted_iota(jnp.int32, q_t.shape, 0)
    q_c = {(hh, c): jnp.where((row >= hh * E + c * dh) & (row < hh * E + (c + 1) * dh), q_t, 0.0).astype(BF)
           for hh, c in chains}
    kk, qq = _iota2((tq, tq))

    def step(carry, kj, bias=None, shift=None):
        kb = kb_sc[kj]
        s = {ch: jnp.dot(kb, q_c[ch], preferred_element_type=F32) for ch in chains}
        if bias is not None:
            s = {ch: s[ch] + bias[ch[0]] for ch in chains}
        top = {ch: jnp.max(s[ch], axis=0, keepdims=True) for ch in chains}
        if shift is not None:
            top = {ch: top[ch] + shift[ch[0]] for ch in chains}
        m_new = {ch: jnp.maximum(carry[ch][0], top[ch]) for ch in chains}
        alpha = {ch: jnp.exp2(carry[ch][0] - m_new[ch]) for ch in chains}
        offs = m_new if shift is None else {ch: m_new[ch] - shift[ch[0]] for ch in chains}
        pr = {ch: jnp.exp2(s[ch] - offs[ch]) for ch in chains}
        l_new = {ch: carry[ch][1] * alpha[ch] + jnp.sum(pr[ch], axis=0, keepdims=True) for ch in chains}
        vt = vt_sc[kj]
        pv = {(hh, c): jnp.dot(vt[hh * E:(hh + 1) * E, :], pr[(hh, c)].astype(BF), preferred_element_type=F32)
              for hh, c in chains}
        return {ch: (m_new[ch], l_new[ch], carry[ch][2] * alpha[ch] + pv[ch]) for ch in chains}

    init = {ch: (jnp.full((1, tq), NEG, F32), jnp.zeros((1, tq), F32), jnp.zeros((E, tq), F32)) for ch in chains}
    carry = step(init, qi, bias=[jnp.where(kk <= qq, bias_ref[hh, 0], NEG) for hh in range(ATTN_HEADS)])
    carry = step(carry, jnp.maximum(qi - 1, 0),
                 bias=[jnp.where(qi >= 1, bias_ref[hh, 1], NEG) for hh in range(ATTN_HEADS)])
    far = [far_ref[hh, 0:1, :] for hh in range(ATTN_HEADS)]
    carry = lax.fori_loop(0, jnp.maximum(qi - 1, 0), lambda kj, c: step(c, kj, shift=far), carry)
    lam = _lam_of(lam_ref, lam_init)
    for hh in range(ATTN_HEADS):
        (_, l1, acc1), (_, l2, acc2) = carry[(hh, 0)], carry[(hh, 1)]
        o_t = acc1 / l1 - lam * (acc2 / l2)
        o_t = o_t * lax.rsqrt(jnp.mean(o_t * o_t, axis=0, keepdims=True) + SUBLN_EPS) * sw_ref[...]
        o_ref[:, hh * E:(hh + 1) * E] = o_t.T * (1.0 - lam_init)


def _attn_prompt(q, k, v, nb, seq, rel_bias, lam, subln_w, lam_init):
    tq = _row_tile(seq, 256)
    assert tq >= REL_MAX_DIST or tq == seq
    nq = seq // tq
    width = ATTN_HEADS * DIFF_E
    d0 = jnp.arange(tq)[None, :] - jnp.arange(tq)[:, None]
    bias = jnp.stack([_t5_bias(d0, rel_bias), _t5_bias(d0 + tq, rel_bias)])
    bias = jnp.transpose(bias, (3, 0, 1, 2)) * LOG2E
    far = jnp.broadcast_to(rel_bias[REL_BUCKETS - 1][:, None, None] * LOG2E, (DIFF_H, 8, tq)).astype(F32)
    qspec = pl.BlockSpec((tq, width), lambda h, b, i: (b * nq + i, h))
    kspec = pl.BlockSpec((seq, width), lambda h, b, i: (b, h))
    return pl.pallas_call(
        functools.partial(_attn_prompt_body, tq=tq, lam_init=lam_init), grid=(DIFF_H // ATTN_HEADS, nb, nq),
        in_specs=[qspec, kspec, kspec,
                  pl.BlockSpec((ATTN_HEADS, 2, tq, tq), lambda h, b, i: (h, 0, 0, 0)),
                  pl.BlockSpec((ATTN_HEADS, 8, tq), lambda h, b, i: (h, 0, 0)),
                  pl.BlockSpec((4, DIFF_DH), lambda h, b, i: (0, 0)),
                  pl.BlockSpec((DIFF_E, 1), lambda h, b, i: (0, 0))],
        out_specs=qspec, out_shape=jax.ShapeDtypeStruct((nb * seq, D_MODEL), F32),
        scratch_shapes=[pltpu.VMEM((nq, tq, width), BF), pltpu.VMEM((nq, width, tq), BF)], name="attn_prompt",
        compiler_params=pltpu.CompilerParams(dimension_semantics=("parallel", "parallel", "arbitrary"),
                                             vmem_limit_bytes=VMEM_LIMIT),
    )(q, k, v, bias, far, lam, subln_w[:, None])


def _attn_sample_body(pt_ref, q_ref, *refs, n_pages, lam_init):
    del pt_ref
    k_refs, v_refs = refs[:n_pages], refs[n_pages:2 * n_pages]
    kn_ref, vn_ref, bias_ref, lam_ref, sw_ref, o_ref = refs[2 * n_pages:]
    page = k_refs[0].shape[0] // DIFF_H
    steps = o_ref.shape[1]
    heads = range(DIFF_H)
    lam = _lam_of(lam_ref, lam_init)
    sw = sw_ref[...]
    zpad = jnp.zeros((page - kn_ref.shape[2], DIFF_E), BF)

    def head_rows(ref, h):
        return ref[pl.ds(h, page, stride=DIFF_H), :].astype(BF)

    def new_rows(ref, h):
        return jnp.concatenate([ref[0, h].astype(BF), zpad], axis=0)

    scores = [[_dot_nt(q_ref[0, h], head_rows(k_refs[i], h)) for i in range(n_pages)]
              + [_dot_nt(q_ref[0, h], new_rows(kn_ref, h))] for h in heads]
    probs, denom = [], []
    for h in heads:
        sc = jnp.concatenate(scores[h], axis=1) + bias_ref[h]
        pr = jnp.exp(sc - jnp.max(sc, axis=1, keepdims=True))
        denom.append(jnp.sum(pr, axis=1, keepdims=True))
        probs.append(pr.astype(BF))
    acc = []
    for h in heads:
        parts = [jnp.dot(probs[h][:, i * page:(i + 1) * page], head_rows(v_refs[i], h), preferred_element_type=F32)
                 for i in range(n_pages)]
        parts.append(jnp.dot(probs[h][:, n_pages * page:], new_rows(vn_ref, h), preferred_element_type=F32))
        acc.append(functools.reduce(jnp.add, parts))
    for h in heads:
        z = acc[h] / denom[h]
        o = z[:steps] - lam * z[steps:]
        o_ref[0, :, h * DIFF_E:(h + 1) * DIFF_E] = _rms(o, sw, SUBLN_EPS) * (1.0 - lam_init)


def _attn_sample(q, k_new, v_new, cache_k, cache_v, layer, page_table, rel_bias, lam, subln_w, lam_init):
    nb, n_pages = page_table.shape
    page = cache_k.shape[2]
    page_ids = page_table + layer * cache_k.shape[1]
    steps = q.shape[0] // nb
    past = n_pages * page
    width = DIFF_H * DIFF_E
    nrow = 2 * steps
    assert nrow % 8 == 0 and steps <= page
    q4 = jnp.transpose((q * (DIFF_DH ** -0.5)).reshape(nb, steps, DIFF_H, DIFF_E), (0, 2, 1, 3))
    first = jnp.arange(DIFF_E) < DIFF_DH
    q8 = jnp.concatenate([jnp.where(first, q4, 0.0), jnp.where(first, 0.0, q4)], axis=2).astype(BF)
    new_rows = lambda t: jnp.pad(jnp.transpose(t.reshape(nb, steps, DIFF_H, DIFF_E), (0, 2, 1, 3)),
                                 ((0, 0), (0, 0), (0, nrow - steps), (0, 0)))
    q_pos = past + jnp.arange(steps)
    key_pos = jnp.arange(past + steps)
    bias = _t5_bias(q_pos[None, :] - key_pos[:, None], rel_bias)
    bias = jnp.where((key_pos[:, None] <= q_pos[None, :])[..., None], bias, NEG)
    bias = jnp.pad(jnp.transpose(bias, (2, 1, 0)), ((0, 0), (0, 0), (0, page - steps)), constant_values=NEG)
    bias = jnp.concatenate([bias, bias], axis=1)
    rows = page * DIFF_H
    page_specs = [pl.BlockSpec((rows, DIFF_E), lambda b, pt, i=i: (pt[b, i], 0)) for i in range(n_pages)]
    head_spec = pl.BlockSpec((1, DIFF_H, nrow, DIFF_E), lambda b, pt: (b, 0, 0, 0))
    grid_spec = pltpu.PrefetchScalarGridSpec(
        num_scalar_prefetch=1, grid=(nb,),
        in_specs=[head_spec] + page_specs + page_specs + [
            head_spec, head_spec,
            pl.BlockSpec((DIFF_H, nrow, past + page), lambda b, pt: (0, 0, 0)),
            pl.BlockSpec((4, DIFF_DH), lambda b, pt: (0, 0)),
            pl.BlockSpec((1, DIFF_E), lambda b, pt: (0, 0))],
        out_specs=pl.BlockSpec((1, steps, width), lambda b, pt: (b, 0, 0)))
    ck = cache_k.reshape(-1, DIFF_E)
    cv = cache_v.reshape(-1, DIFF_E)
    o = pl.pallas_call(
        functools.partial(_attn_sample_body, n_pages=n_pages, lam_init=lam_init), grid_spec=grid_spec,
        out_shape=jax.ShapeDtypeStruct((nb, steps, width), F32), name="attn_sample",
        compiler_params=pltpu.CompilerParams(dimension_semantics=("parallel",), vmem_limit_bytes=VMEM_LIMIT),
    )(page_ids, q8, *([ck] * n_pages), *([cv] * n_pages), new_rows(k_new), new_rows(v_new), bias, lam,
      subln_w[None])
    return o.reshape(nb * steps, width)


def _attn_layer(x, u, dims, cache_k, cache_v, layer, page_table, p, lam_init, rel_bias):
    nb_p, seq_p, nb_s, seq_s = dims
    (q_p, q_s), (k_p, k_s), (v_p, v_s) = _linear(u, p['w_qkv'], 3, "attn_qkv")
    o_p = _attn_prompt(q_p, k_p, v_p, nb_p, seq_p, rel_bias, p['lam'], p['subln_w'], lam_init)
    o_s = _attn_sample(q_s, k_s, v_s, cache_k, cache_v, layer, page_table, rel_bias, p['lam'], p['subln_w'],
                       lam_init)
    x = _linear_res(x, (o_p, o_s), p['w_o'], "attn_out")
    shp_p = (nb_p, seq_p, DIFF_H, DIFF_E)
    shp_s = (nb_s, seq_s, DIFF_H, DIFF_E)
    return x, (k_p.reshape(shp_p), v_p.reshape(shp_p), k_s.reshape(shp_s), v_s.reshape(shp_s))


def _gdn_proj_body(is_prompt, tile, u_ref, wqkv_ref, wz_ref, wb_ref, wa_ref, al_ref, dt_ref,
                   qkv_o, z_o, beta_o, g_o):
    del is_prompt, tile
    u = u_ref[...].astype(BF)
    qkv_o[...] = jnp.dot(u, wqkv_ref[...], preferred_element_type=F32)
    z_o[...] = jnp.dot(u, wz_ref[...], preferred_element_type=F32)
    beta = _sigmoid(jnp.dot(u, wb_ref[...], preferred_element_type=F32))
    g = -jnp.exp(al_ref[...]) * _softplus(jnp.dot(u, wa_ref[...], preferred_element_type=F32) + dt_ref[...])
    for h in range(GDN_H):
        cols = slice(h * GDN_DV, (h + 1) * GDN_DV)
        beta_o[:, cols] = jnp.broadcast_to(beta[:, h:h + 1], (beta.shape[0], GDN_DV))
        g_o[:, cols] = jnp.broadcast_to(g[:, h:h + 1], (g.shape[0], GDN_DV))


def _gdn_post(conv, q_o, k_o, v_o):
    c = conv * _sigmoid(conv)
    hk = GDN_H * GDN_DK
    for h in range(GDN_H):
        qs = c[:, h * GDN_DK:(h + 1) * GDN_DK]
        ks = c[:, hk + h * GDN_DK:hk + (h + 1) * GDN_DK]
        q_o[:, h * GDN_DK:(h + 1) * GDN_DK] = qs * lax.rsqrt(jnp.sum(qs * qs, axis=1, keepdims=True) + 1e-6) * (GDN_DK ** -0.5)
        k_o[:, h * GDN_DK:(h + 1) * GDN_DK] = ks * lax.rsqrt(jnp.sum(ks * ks, axis=1, keepdims=True) + 1e-6)
    v_o[...] = c[:, 2 * hk:]


def _gdn_conv_prompt_body(x_ref, halo_ref, w_ref, q_o, k_o, v_o, *, tiles_per_seq):
    i = pl.program_id(0)
    x = x_ref[...]
    tm = x.shape[0]
    halo = jnp.where(i % tiles_per_seq == 0, 0.0, halo_ref[...])
    row8 = lax.broadcasted_iota(jnp.int32, (8, x.shape[1]), 0)
    w = w_ref[...]
    conv = w[GDN_CONV - 1:GDN_CONV] * x
    for j in range(GDN_CONV - 1):
        sh = GDN_CONV - 1 - j
        rolled = pltpu.roll(x, sh, 0)
        head = jnp.where(row8 < sh, pltpu.roll(halo, sh, 0), rolled[:8])
        tap = jnp.concatenate([head, rolled[8:]], axis=0) if tm > 8 else head
        conv = conv + w[j:j + 1] * tap
    _gdn_post(conv, q_o, k_o, v_o)


def _gdn_conv_taps_body(t0_ref, t1_ref, t2_ref, t3_ref, w_ref, q_o, k_o, v_o):
    w = w_ref[...]
    conv = w[0:1] * t0_ref[...] + w[1:2] * t1_ref[...] + w[2:3] * t2_ref[...] + w[3:4] * t3_ref[...]
    _gdn_post(conv, q_o, k_o, v_o)


def _gdn_advance(L, refs, rows, cols, st):
    q_ref, k_ref, v_ref, beta_ref, g_ref = refs
    P = LANES
    ii, jj = _iota2((L, L))
    strict = ii > jj
    incl = ii >= jj
    us = range(len(st))
    at = lambda ref, u: ref[rows[u], cols[u]]
    gc = [_cumsum_rows(at(g_ref, u)) for u in us]
    q = [at(q_ref, u) for u in us]
    k = [at(k_ref, u) for u in us]
    kk = [_dot_nt(k[u], k[u]) for u in us]
    qk = [_dot_nt(q[u], k[u]) for u in us]
    n_mats, dec_i, e_gc, rhs = [], [], [], []
    for u in us:
        diff = gc[u][:, :L] - gc[u].T[:L, :]
        beta = at(beta_ref, u)
        n_mats.append(-(beta[:, :L] * kk[u] * jnp.exp(jnp.where(strict, diff, NEG))))
        dec_i.append(jnp.exp(jnp.where(incl, diff, NEG)))
        e_gc.append(jnp.exp(gc[u]))
        rhs.append(jnp.concatenate([beta * at(v_ref, u), beta * e_gc[u] * k[u]], axis=1))
    side = GDN_INV_SIDE
    assert len(st) % side == 0
    t_rows = _tri_inv([jnp.concatenate(n_mats[c:c + side], axis=1) for c in range(0, len(st), side)], L)
    t_inv = [t_rows[u // side][:, (u % side) * L:(u % side + 1) * L] for u in us]
    w = [_dot(t_inv[u], rhs[u]) for u in us]
    uu = [w[u][:, :P] - _dot(w[u][:, P:], st[u]) for u in us]
    qs = [_dot(q[u] * e_gc[u], st[u]) for u in us]
    intra = [_dot(qk[u] * dec_i[u], uu[u]) for u in us]
    upd = [_dot_tn(k[u] * jnp.exp(gc[u][L - 1:L, :] - gc[u]), uu[u]) for u in us]
    return ([qs[u] + intra[u] for u in us], [jnp.exp(gc[u][L - 1:L, :]) * st[u] + upd[u] for u in us])


def _gdn_chunk_body(q_ref, k_ref, v_ref, beta_ref, g_ref, o_ref, s_ref, st_sc, *, tb, groups):
    L, P = CHUNK, LANES
    t = pl.program_id(2)

    @pl.when(t == 0)
    def _():
        st_sc[...] = jnp.zeros(st_sc.shape, F32)

    refs = (q_ref, k_ref, v_ref, beta_ref, g_ref)
    cols = [slice(g * P, (g + 1) * P) for g in range(groups)]

    def chunk(c, carry):
        rows = pl.ds(pl.multiple_of(c * L, L), L)
        outs, new = _gdn_advance(L, refs, [rows] * groups, cols, [st_sc[g] for g in range(groups)])
        for g in range(groups):
            o_ref[rows, cols[g]] = outs[g]
            st_sc[g] = new[g]
        return carry

    lax.fori_loop(0, tb // L, chunk, 0, unroll=2)

    @pl.when(t == pl.num_programs(2) - 1)
    def _():
        for g in range(groups):
            s_ref[0, g] = st_sc[g]


def _gdn_short_body(q_ref, k_ref, v_ref, beta_ref, g_ref, s0_ref, o_ref, s_ref, *, bb, length):
    P = LANES
    units = [(b_i, h) for b_i in range(bb) for h in range(GDN_H)]
    rows = [slice(b_i * length, (b_i + 1) * length) for b_i, _ in units]
    cols = [slice(h * P, (h + 1) * P) for _, h in units]
    outs, new = _gdn_advance(length, (q_ref, k_ref, v_ref, beta_ref, g_ref), rows, cols,
                             [s0_ref[b_i, h] for b_i, h in units])
    for u, (b_i, h) in enumerate(units):
        o_ref[rows[u], cols[u]] = outs[u]
        s_ref[b_i, h] = new[u]


def _gdn_out_body(is_prompt, tile, x_ref, o_ref, z_ref, nw_ref, wo_ref, y_ref):
    del is_prompt, tile
    z = z_ref[...]
    nw = nw_ref[...]
    o = o_ref[...]
    parts = []
    for h in range(GDN_H):
        sl = slice(h * GDN_DV, (h + 1) * GDN_DV)
        parts.append(_rms(o[:, sl], nw, NORM_EPS))
    on = jnp.concatenate(parts, axis=1) * (z * _sigmoid(z))
    y_ref[...] = x_ref[...] + _dot(on, wo_ref[...])


def _gdn_layer(x, u, dims, conv_s, states, layer, p):
    nb_p, seq_p, nb_s, seq_s = dims
    n_p, n_s = nb_p * seq_p, nb_s * seq_s
    hv = GDN_H * GDN_DV
    w_in = p['w_in']
    pad = lambda w: jnp.pad(w, ((0, 0), (0, LANES - w.shape[-1])))
    (qkv_p, qkv_s), z, (beta_p, beta_s), (g_p, g_s) = _rows_call(
        _gdn_proj_body, [u],
        [w_in[:, :GDN_QKV].astype(BF), w_in[:, GDN_QKV:GDN_QKV + hv].astype(BF),
         pad(w_in[:, GDN_QKV + hv:GDN_QKV + hv + GDN_H]).astype(BF), pad(w_in[:, GDN_QKV + hv + GDN_H:]).astype(BF),
         pad(p['a_log'][None]), pad(p['dt_bias'][None])],
        [(GDN_QKV, F32), (hv, F32), (hv, F32), (hv, F32)], 256, "gdn_proj")
    tm = _row_tile(seq_p, 256)
    cw = p['conv_w']
    outs3 = [jax.ShapeDtypeStruct((n_p, hv), F32)] * 3
    q_p, k_p, v_p = pl.pallas_call(
        functools.partial(_gdn_conv_prompt_body, tiles_per_seq=seq_p // tm), grid=(n_p // tm,),
        in_specs=[pl.BlockSpec((tm, GDN_QKV), lambda i: (i, 0)),
                  pl.BlockSpec((8, GDN_QKV), lambda i: (jnp.maximum(i * (tm // 8) - 1, 0), 0)),
                  pl.BlockSpec((GDN_CONV, GDN_QKV), lambda i: (0, 0))],
        out_specs=[pl.BlockSpec((tm, hv), lambda i: (i, 0))] * 3, out_shape=outs3, name="gdn_conv_prompt",
        compiler_params=pltpu.CompilerParams(dimension_semantics=("parallel",), vmem_limit_bytes=VMEM_LIMIT),
    )(qkv_p, qkv_p, cw)
    xp_s = jnp.concatenate([conv_s, qkv_s.reshape(nb_s, seq_s, GDN_QKV)], axis=1)
    taps = [xp_s[:, j:j + seq_s].reshape(n_s, GDN_QKV) for j in range(GDN_CONV)]
    ts = _row_tile(n_s, 256)
    q_s, k_s, v_s = pl.pallas_call(
        _gdn_conv_taps_body, grid=(n_s // ts,),
        in_specs=[pl.BlockSpec((ts, GDN_QKV), lambda i: (i, 0))] * GDN_CONV
        + [pl.BlockSpec((GDN_CONV, GDN_QKV), lambda i: (0, 0))],
        out_specs=[pl.BlockSpec((ts, hv), lambda i: (i, 0))] * 3,
        out_shape=[jax.ShapeDtypeStruct((n_s, hv), F32)] * 3, name="gdn_conv_sample",
        compiler_params=pltpu.CompilerParams(dimension_semantics=("parallel",), vmem_limit_bytes=VMEM_LIMIT),
    )(*taps, cw)
    groups = GDN_GROUPS
    tb = _row_tile(seq_p, SEQ_BLOCK)
    nt = seq_p // tb
    spec = pl.BlockSpec((tb, groups * LANES), lambda b, h, t: (b * nt + t, h))
    o_p, s_p = pl.pallas_call(
        functools.partial(_gdn_chunk_body, tb=tb, groups=groups), grid=(nb_p, GDN_H // groups, nt),
        in_specs=[spec] * 5,
        out_specs=[spec, pl.BlockSpec((1, groups, GDN_DK, GDN_DV), lambda b, h, t: (b, h, 0, 0))],
        out_shape=[jax.ShapeDtypeStruct((n_p, hv), F32), jax.ShapeDtypeStruct((nb_p, GDN_H, GDN_DK, GDN_DV), F32)],
        scratch_shapes=[pltpu.VMEM((groups, GDN_DK, GDN_DV), F32)],
        name="gdn_chunk",
        compiler_params=pltpu.CompilerParams(dimension_semantics=("parallel", "parallel", "arbitrary"),
                                             vmem_limit_bytes=VMEM_LIMIT),
    )(q_p, k_p, v_p, beta_p, g_p)
    length = SHORT_LEN * pl.cdiv(seq_s, SHORT_LEN)
    pad_rows = lambda t: jnp.pad(t.reshape(nb_s, seq_s, hv), ((0, 0), (0, length - seq_s), (0, 0))).reshape(nb_s * length, hv)
    bb = _row_tile(nb_s, SHORT_ROWS)
    sspec = pl.BlockSpec((bb * length, hv), lambda b: (b, 0))
    stspec = pl.BlockSpec((bb, GDN_H, GDN_DK, GDN_DV), lambda b: (b, 0, 0, 0))
    first = layer * (nb_s // bb)
    o_s, s_s = pl.pallas_call(
        functools.partial(_gdn_short_body, bb=bb, length=length), grid=(nb_s // bb,),
        in_specs=[sspec] * 5 + [pl.BlockSpec((bb, GDN_H, GDN_DK, GDN_DV), lambda b: (first + b, 0, 0, 0))],
        out_specs=[sspec, stspec],
        out_shape=[jax.ShapeDtypeStruct((nb_s * length, hv), F32),
                   jax.ShapeDtypeStruct((nb_s, GDN_H, GDN_DK, GDN_DV), F32)],
        name="gdn_short",
        compiler_params=pltpu.CompilerParams(dimension_semantics=("parallel",), vmem_limit_bytes=VMEM_LIMIT),
    )(pad_rows(q_s), pad_rows(k_s), pad_rows(v_s), pad_rows(beta_s), pad_rows(g_s),
      states.reshape(-1, GDN_H, GDN_DK, GDN_DV))
    o_s = o_s.reshape(nb_s, length, hv)[:, :seq_s].reshape(n_s, hv)
    x = _rows_call(_gdn_out_body, [x, (o_p, o_s), z], [p['norm_w'][None], p['w_o'].astype(BF)],
                   [(D_MODEL, F32)], 512, "gdn_out")[0]
    keep = GDN_CONV - 1
    tail_p = qkv_p.reshape(nb_p, seq_p, GDN_QKV)[:, -keep:]
    conv_p = jnp.pad(tail_p, ((0, 0), (max(keep - seq_p, 0), 0), (0, 0)))
    return x, (s_p, conv_p, s_s, xp_s[:, -keep:])


def kernel(x_prompt, x_sample, state_rwkv_wkv, state_rwkv_shift, cache_attn_k, cache_attn_v, state_gdn, state_gdn_conv, page_table, norm_w, final_norm_w, ffn_w_gate, ffn_w_up, ffn_w_down, rwkv_mu, rwkv_w_rkv, rwkv_w_o, rwkv_w0, rwkv_w1, rwkv_w2, rwkv_a0, rwkv_a1, rwkv_a2, rwkv_g1, rwkv_g2, rwkv_k_k, rwkv_k_a, rwkv_r_k, rwkv_lnx_w, rwkv_lnx_b, rwkv_v0, rwkv_v1, rwkv_v2, attn_w_qkv, attn_w_o, attn_lambda, attn_subln_w, rel_bias, gdn_w_in, gdn_conv_w, gdn_a_log, gdn_dt_bias, gdn_norm_w, gdn_w_o):
    nb_p, seq_p, _ = x_prompt.shape
    nb_s, seq_s, _ = x_sample.shape
    dims = (nb_p, seq_p, nb_s, seq_s)
    depth = norm_w.shape[0]
    x = (x_prompt.reshape(nb_p * seq_p, D_MODEL), x_sample.reshape(nb_s * seq_s, D_MODEL))
    v_first = None
    rw, at, gd = [], [], []
    for i in range(depth):
        kind, j = i % 3, i // 3
        x, u = _ffn(x, norm_w[i, 0], ffn_w_gate[i, 0], ffn_w_up[i, 0], ffn_w_down[i, 0], norm_w[i, 1])
        if kind == 0:
            p = dict(mu=rwkv_mu[j], w_rkv=rwkv_w_rkv[j], w_o=rwkv_w_o[j], w0=rwkv_w0[j], w1=rwkv_w1[j],
                     w2=rwkv_w2[j], a0=rwkv_a0[j], a1=rwkv_a1[j], a2=rwkv_a2[j], g1=rwkv_g1[j], g2=rwkv_g2[j],
                     k_k=rwkv_k_k[j], k_a=rwkv_k_a[j], r_k=rwkv_r_k[j], lnx_w=rwkv_lnx_w[j], lnx_b=rwkv_lnx_b[j])
            vres = None if j == 0 else (rwkv_v0[j - 1], rwkv_v1[j - 1], rwkv_v2[j - 1])
            x, v_first, st = _rwkv_layer(x, u, dims, state_rwkv_shift[j], state_rwkv_wkv, j, p, v_first, vres)
            rw.append(st)
        elif kind == 1:
            p = dict(w_qkv=attn_w_qkv[j], w_o=attn_w_o[j], lam=attn_lambda[j], subln_w=attn_subln_w[j])
            lam_init = 0.8 - 0.6 * math.exp(-0.3 * i)
            x, st = _attn_layer(x, u, dims, cache_attn_k, cache_attn_v, j, page_table, p, lam_init, rel_bias)
            at.append(st)
        else:
            p = dict(w_in=gdn_w_in[j], conv_w=gdn_conv_w[j], a_log=gdn_a_log[j], dt_bias=gdn_dt_bias[j],
                     norm_w=gdn_norm_w[j], w_o=gdn_w_o[j])
            x, st = _gdn_layer(x, u, dims, state_gdn_conv[j], state_gdn, j, p)
            gd.append(st)
        if i == depth - 1:
            x, y = _ffn(x, norm_w[i, 2], ffn_w_gate[i, 1], ffn_w_up[i, 1], ffn_w_down[i, 1], final_norm_w)
        else:
            x = _ffn(x, norm_w[i, 2], ffn_w_gate[i, 1], ffn_w_up[i, 1], ffn_w_down[i, 1])
    stack = lambda lst, k: lst[0][k][None] if len(lst) == 1 else jnp.stack([t[k] for t in lst])
    return (y[0].reshape(x_prompt.shape), y[1].reshape(x_sample.shape),
            stack(rw, 0), stack(rw, 1), stack(at, 0), stack(at, 1), stack(gd, 0), stack(gd, 1),
            stack(rw, 2), stack(rw, 3), stack(at, 2), stack(at, 3), stack(gd, 2), stack(gd, 3))
```

```python
import functools
import math

import jax
import jax.numpy as jnp
from jax import lax
from jax.experimental import pallas as pl
from jax.experimental.pallas import tpu as pltpu

F32 = jnp.float32
BF = jnp.bfloat16

D_MODEL = 1024
NORM_EPS = 1e-6
RWKV_N = 64
RWKV_H = D_MODEL // RWKV_N
RWKV_LNX_EPS = 64e-5
DIFF_H = 8
DIFF_DH = 64
DIFF_E = 128
SUBLN_EPS = 1e-5
REL_BUCKETS = 32
REL_MAX_EXACT = 16
REL_MAX_DIST = 128
GDN_H = 8
GDN_DK = 128
GDN_DV = 128
GDN_CONV = 4
GDN_QKV = GDN_H * (2 * GDN_DK + GDN_DV)
FFN_TILE = 1024
FFN_CHUNK_BIG_TILE = 256
FFN_CHUNK_SMALL_TILE = 1408
CHUNK = 64
SEQ_BLOCK = 512
RWKV_GROUPS = 8
GDN_GROUPS = 8
GDN_INV_SIDE = 1
RWKV_INV_UNITS = 1
LANES = 128
NEG = -1e30
LOG2E = math.log2(math.e)
VMEM_LIMIT = 56 * 1024 * 1024


def _dot(a, b):
    return jnp.dot(a.astype(BF), b.astype(BF), preferred_element_type=F32)


def _dot_nt(a, b):
    return lax.dot_general(a.astype(BF), b.astype(BF), (((1,), (1,)), ((), ())), preferred_element_type=F32)


def _dot_tn(a, b):
    return lax.dot_general(a.astype(BF), b.astype(BF), (((0,), (0,)), ((), ())), preferred_element_type=F32)


def _rms(x, g, eps):
    return x * lax.rsqrt(jnp.mean(x * x, axis=-1, keepdims=True) + eps) * g


def _sigmoid(x):
    return 1.0 / (1.0 + jnp.exp(-x))


def _softplus(x):
    return jnp.maximum(x, 0.0) + jnp.log(1.0 + jnp.exp(-jnp.abs(x)))


def _iota2(shape):
    return lax.broadcasted_iota(jnp.int32, shape, 0), lax.broadcasted_iota(jnp.int32, shape, 1)


def _split(x, n):
    out = []
    for _ in range(n):
        h = x.astype(BF)
        out.append(h)
        x = x - h.astype(F32)
    return out


def _dot_sel(x, sel, n=2):
    return sum(jnp.dot(t, sel, preferred_element_type=F32) for t in _split(x, n))


def _segment_allsum(x, e, et):
    return _dot_sel(_dot_sel(x, e), et)


def _cumsum_rows(x):
    n = x.shape[0]
    row = lax.broadcasted_iota(jnp.int32, x.shape, 0)
    shift = 1
    while shift < n:
        x = x + jnp.where(row >= shift, pltpu.roll(x, shift, 0), 0.0)
        shift *= 2
    return x


def _block_diag(m_row, size):
    mb = m_row.astype(BF)
    blk = lax.broadcasted_iota(jnp.int32, mb.shape, 1) >> (size.bit_length() - 1)
    return jnp.concatenate([jnp.where(blk == h, mb, jnp.zeros_like(mb)) for h in range(mb.shape[1] // size)], axis=0)


def _tri_inv(n_rows, size):
    ii, jj = _iota2(n_rows[0].shape)
    jj = jj & (size - 1)
    eye = (ii == jj).astype(F32)
    xs = [eye + jnp.where((ii >> 1) == (jj >> 1), n, 0.0) for n in n_rows]
    lvl = 1
    while (2 << lvl) <= size:
        sel = ((ii >> (lvl + 1)) == (jj >> (lvl + 1))) & (((ii >> lvl) & 1) == 1) & (((jj >> lvl) & 1) == 0)
        ys = [_dot(x, _block_diag(jnp.where(sel, n, 0.0), size)) for x, n in zip(xs, n_rows)]
        xs = [x + _dot(y, _block_diag(x, size)) for x, y in zip(xs, ys)]
        lvl += 1
    return xs


def _row_tile(n, want):
    tm = min(want, n)
    while n % tm:
        tm //= 2
    return tm


HALO = 8
SAMPLE_TILE = 128


def _rows_call(body, rows, consts, outs, tm, name, halo=()):
    n_p, n_s = rows[0][0].shape[0], rows[0][1].shape[0]
    tm_p, tm_s = _row_tile(n_p, tm), _row_tile(n_s, min(tm, SAMPLE_TILE))
    tp, ts = n_p // tm_p, n_s // tm_s
    p_idx = lambda i: jnp.minimum(i, tp - 1)
    s_idx = lambda i: jnp.maximum(i - tp, 0)
    in_specs, args = [], []
    for k, (a_p, a_s) in enumerate(rows):
        if k in halo:
            in_specs.append(pl.BlockSpec((HALO, a_p.shape[1]),
                                         lambda i: (jnp.maximum(p_idx(i) * (tm_p // HALO) - 1, 0), 0)))
        else:
            in_specs.append(pl.BlockSpec((tm_p, a_p.shape[1]), lambda i: (p_idx(i), 0)))
        in_specs.append(pl.BlockSpec((tm_s, a_s.shape[1]), lambda i: (s_idx(i), 0)))
        args += [a_p, a_s]
    for a in consts:
        in_specs.append(pl.BlockSpec(a.shape, lambda i, nd=a.ndim: (0,) * nd, pipeline_mode=pl.Buffered(1)))
    out_specs, out_shape = [], []
    for c, dt in outs:
        out_specs += [pl.BlockSpec((tm_p, c), lambda i: (p_idx(i), 0)), pl.BlockSpec((tm_s, c), lambda i: (s_idx(i), 0))]
        out_shape += [jax.ShapeDtypeStruct((n_p, c), dt), jax.ShapeDtypeStruct((n_s, c), dt)]
    nr, nc = len(rows), len(consts)

    def kern(*refs):
        i = pl.program_id(0)
        cs = refs[2 * nr:2 * nr + nc]
        o = refs[2 * nr + nc:]

        @pl.when(i < tp)
        def _():
            body(True, i, *refs[0:2 * nr:2], *cs, *o[0::2])

        @pl.when(i >= tp)
        def _():
            body(False, i - tp, *refs[1:2 * nr:2], *cs, *o[1::2])

    res = pl.pallas_call(
        kern, grid=(tp + ts,), in_specs=in_specs, out_specs=out_specs, out_shape=out_shape, name=name,
        compiler_params=pltpu.CompilerParams(dimension_semantics=("arbitrary",), vmem_limit_bytes=VMEM_LIMIT),
    )(*args, *consts)
    return [(res[2 * k], res[2 * k + 1]) for k in range(len(outs))]


def _ffn_body(is_prompt, tile, x_ref, g_ref, wg_ref, wu_ref, wd_ref, *rest, normed):
    del is_prompt, tile
    (g2_ref, o_ref, u_ref) = rest if normed else (None, rest[0], None)
    x = x_ref[...]
    h = _rms(x, g_ref[...], NORM_EPS).astype(BF)
    hidden = wg_ref.shape[1]
    want = FFN_CHUNK_BIG_TILE if x.shape[0] >= FFN_TILE else FFN_CHUNK_SMALL_TILE
    nchunk = max(n for n in range(1, hidden // LANES + 1) if hidden % (n * LANES) == 0 and hidden // n >= want)
    fc = hidden // nchunk
    acc = jnp.zeros_like(x)
    for c in range(nchunk):
        sl = slice(c * fc, (c + 1) * fc)
        gate = jnp.dot(h, wg_ref[:, sl], preferred_element_type=F32)
        up = jnp.dot(h, wu_ref[:, sl], preferred_element_type=F32)
        act = (gate * _sigmoid(gate) * up).astype(BF)
        acc = acc + jnp.dot(act, wd_ref[sl, :], preferred_element_type=F32)
    y = x + 0.5 * acc
    o_ref[...] = y
    if normed:
        u_ref[...] = _rms(y, g2_ref[...], NORM_EPS)


def _ffn(x, g, wg, wu, wd, g_next=None):
    normed = g_next is not None
    consts = [g[None], wg.astype(BF), wu.astype(BF), wd.astype(BF)] + ([g_next[None]] if normed else [])
    outs = _rows_call(functools.partial(_ffn_body, normed=normed), [x], consts,
                      [(D_MODEL, F32)] * (2 if normed else 1), FFN_TILE, "ffn_half")
    return outs if normed else outs[0]


def _linear_body(is_prompt, tile, *refs, n_out, residual):
    del is_prompt, tile
    if residual:
        x_ref, u_ref, w_ref = refs[:3]
        outs = refs[3:]
    else:
        u_ref, w_ref = refs[:2]
        outs = refs[2:]
    y = jnp.dot(u_ref[...].astype(BF), w_ref[...], preferred_element_type=F32)
    if residual:
        outs[0][...] = x_ref[...] + y
    else:
        c = y.shape[1] // n_out
        for k, o in enumerate(outs):
            o[...] = y[:, k * c:(k + 1) * c]


def _linear(u, w, n_out, name):
    body = functools.partial(_linear_body, n_out=n_out, residual=False)
    c = w.shape[1] // n_out
    return _rows_call(body, [u], [w.astype(BF)], [(c, F32)] * n_out, 512, name)


def _linear_res(x, u, w, name):
    body = functools.partial(_linear_body, n_out=1, residual=True)
    return _rows_call(body, [x, u], [w.astype(BF)], [(w.shape[1], F32)], 512, name)[0]


def _rwkv_proj_body(is_prompt, tile, *refs, has_vres, tiles_per_seq):
    (u_ref, prev_ref) = refs[:2]
    k0 = 2
    if has_vres:
        vf_ref = refs[2]
        k0 = 3
    (mu_ref, wr_ref, wk_ref, wv_ref, w0_ref, w1_ref, w2_ref, a0_ref, a1_ref, a2_ref, g1_ref, g2_ref,
     kk_ref, ka_ref, rk_ref, e_ref, et_ref) = refs[k0:k0 + 17]
    k1 = k0 + 17
    if has_vres:
        v0_ref, v1_ref, v2_ref = refs[k1:k1 + 3]
        k1 += 3
    r_o, lw_o, k_o, v_o, a_o, b_o, g_o, bon_o = refs[k1:]
    u = u_ref[...]
    if is_prompt:
        before = jnp.where(tile % tiles_per_seq == 0, 0.0, prev_ref[HALO - 1:HALO, :])
        row = lax.broadcasted_iota(jnp.int32, u.shape, 0)
        prev = jnp.where(row == 0, before, pltpu.roll(u, 1, 0))
    else:
        prev = prev_ref[...]
    xx = prev - u
    mu = mu_ref[...]
    xr, xw, xk = u + xx * mu[0:1], u + xx * mu[1:2], u + xx * mu[2:3]
    xv, xa, xg = u + xx * mu[3:4], u + xx * mu[4:5], u + xx * mu[5:6]
    r = _dot(xr, wr_ref[...])
    k = _dot(xk, wk_ref[...])
    v = _dot(xv, wv_ref[...])
    wlog = -_softplus(-(w0_ref[...] + _dot(jnp.tanh(_dot(xw, w1_ref[...])), w2_ref[...]))) - 0.5
    lw_o[...] = -jnp.exp(wlog)
    if has_vres:
        v = v + (vf_ref[...] - v) * _sigmoid(v0_ref[...] + _dot(_dot(xv, v1_ref[...]), v2_ref[...]))
    a = _sigmoid(a0_ref[...] + _dot(_dot(xa, a1_ref[...]), a2_ref[...]))
    g_o[...] = _dot(_sigmoid(_dot(xg, g1_ref[...])), g2_ref[...])
    kk = k * kk_ref[...]
    e, et = e_ref[...], et_ref[...]
    kk = kk * lax.rsqrt(_segment_allsum(kk * kk, e, et) + 1e-24)
    k_mod = k * (1.0 + (a - 1.0) * ka_ref[...])
    r_o[...] = r
    k_o[...] = k_mod
    v_o[...] = v
    a_o[...] = -kk
    b_o[...] = kk * a
    bon_o[...] = _segment_allsum(r * k_mod * rk_ref[...], e, et) * v


def _rwkv_advance(L, refs, rows, cols, st):
    r_ref, lw_ref, k_ref, v_ref, a_ref, b_ref = refs
    P, N = LANES, RWKV_N
    assert len(st) % RWKV_INV_UNITS == 0
    ii, jj = _iota2((2 * L, 4 * L))
    jj = jj & (L - 1)
    mask4 = (((ii < L) & (ii > jj)) | ((ii >= L) & (ii - L >= jj))).astype(F32)
    lane = lax.broadcasted_iota(jnp.int32, (L, P), 1)
    bi, bj = _iota2((P, P))
    bd = ((bi >= N) == (bj >= N)).astype(F32)
    us = range(len(st))
    head_lanes = [(lane >= N) == bool(h) for h in range(2)]

    def by_head(t):
        return jnp.concatenate([jnp.where(m, t, jnp.zeros_like(t)) for m in head_lanes], axis=0)

    x, wt, v, bk_end, w_end = [], [], [], [], []
    for u in us:
        at = lambda ref: ref[rows[u], cols[u]]
        lw = at(lw_ref)
        gc = _cumsum_rows(lw)
        gl = gc[L - 1:L, :]
        r, k, a, b = at(r_ref), at(k_ref), at(a_ref), at(b_ref)
        e_neg = jnp.exp(-gc)
        e_end = jnp.exp(gl - gc)
        x.append(jnp.concatenate([a * jnp.exp(gc - lw), r * jnp.exp(gc)], axis=0).astype(BF))
        wt.append(jnp.concatenate([by_head((b * e_neg).astype(BF)), by_head((k * e_neg).astype(BF))], axis=0))
        v.append(at(v_ref))
        bk_end.append(jnp.concatenate([b * e_end, k * e_end], axis=0))
        w_end.append(jnp.exp(gl))
    xs = [_dot_nt(x[u], st[u]) for u in us]
    abk = [_dot_nt(x[u], wt[u]) * mask4 for u in us]
    side = RWKV_INV_UNITS
    t_rows = _tri_inv([jnp.concatenate([abk[u + d][:L, :2 * L] for d in range(side)], axis=1) for u in us[::side]], L)
    t_inv = [t_rows[u // side][:, (u % side) * 2 * L:(u % side + 1) * 2 * L] for u in us]
    akv = [_dot(abk[u][:, 2 * L:], by_head(v[u])) for u in us]
    rhs = [xs[u][:L] + akv[u][:L] for u in us]
    uu = [_dot(t_inv[u], by_head(rhs[u])) for u in us]
    o_u = [_dot(abk[u][L:, :2 * L], by_head(uu[u])) for u in us]
    upd = [_dot_tn(jnp.concatenate([uu[u], v[u]], axis=0), bk_end[u]) for u in us]
    outs = [xs[u][L:] + akv[u][L:] + o_u[u] for u in us]
    return outs, [st[u] * w_end[u] + bd * upd[u] for u in us]


def _rwkv_chunk_body(r_ref, lw_ref, k_ref, v_ref, a_ref, b_ref, o_ref, s_ref, st_sc, *, tb, groups):
    L, P, N = CHUNK, LANES, RWKV_N
    t = pl.program_id(2)

    @pl.when(t == 0)
    def _():
        st_sc[...] = jnp.zeros(st_sc.shape, F32)

    refs = (r_ref, lw_ref, k_ref, v_ref, a_ref, b_ref)
    cols = [slice(g * P, (g + 1) * P) for g in range(groups)]

    def chunk(c, carry):
        rows = pl.ds(pl.multiple_of(c * L, L), L)
        outs, new = _rwkv_advance(L, refs, [rows] * groups, cols, [st_sc[g] for g in range(groups)])
        for g in range(groups):
            o_ref[rows, cols[g]] = outs[g]
            st_sc[g] = new[g]
        return carry

    lax.fori_loop(0, tb // L, chunk, 0, unroll=2)

    @pl.when(t == pl.num_programs(2) - 1)
    def _():
        for g in range(groups):
            s = st_sc[g]
            s_ref[0, 2 * g] = s[:N, :N]
            s_ref[0, 2 * g + 1] = s[N:, N:]


def _rwkv_chunk(seqs, nb, seq):
    groups = RWKV_GROUPS
    width = groups * LANES
    tb = _row_tile(seq, SEQ_BLOCK)
    nt = seq // tb
    spec = pl.BlockSpec((tb, width), lambda b, h, t: (b * nt + t, h))
    return pl.pallas_call(
        functools.partial(_rwkv_chunk_body, tb=tb, groups=groups), grid=(nb, D_MODEL // width, nt),
        in_specs=[spec] * 6,
        out_specs=[spec, pl.BlockSpec((1, 2 * groups, RWKV_N, RWKV_N), lambda b, h, t: (b, h, 0, 0))],
        out_shape=[jax.ShapeDtypeStruct((nb * seq, D_MODEL), F32),
                   jax.ShapeDtypeStruct((nb, RWKV_H, RWKV_N, RWKV_N), F32)],
        scratch_shapes=[pltpu.VMEM((groups, LANES, LANES), F32)],
        name="rwkv_chunk",
        compiler_params=pltpu.CompilerParams(dimension_semantics=("parallel", "parallel", "arbitrary"),
                                             vmem_limit_bytes=VMEM_LIMIT),
    )(*seqs)


def _rwkv_short_body(r_ref, lw_ref, k_ref, v_ref, a_ref, b_ref, s0_ref, o_ref, s_ref, *, bb, length):
    P, N = LANES, RWKV_N
    groups = D_MODEL // P
    units = [(b_i, g) for b_i in range(bb) for g in range(groups)]
    zero = jnp.zeros((N, N), F32)
    st = [jnp.concatenate([jnp.concatenate([s0_ref[0, b_i, 2 * g], zero], axis=1),
                           jnp.concatenate([zero, s0_ref[0, b_i, 2 * g + 1]], axis=1)], axis=0) for b_i, g in units]
    rows = [slice(b_i * length, (b_i + 1) * length) for b_i, _ in units]
    cols = [slice(g * P, (g + 1) * P) for _, g in units]
    outs, new = _rwkv_advance(length, (r_ref, lw_ref, k_ref, v_ref, a_ref, b_ref), rows, cols, st)
    for u, (b_i, g) in enumerate(units):
        o_ref[rows[u], cols[u]] = outs[u]
        s_ref[b_i, 2 * g] = new[u][:N, :N]
        s_ref[b_i, 2 * g + 1] = new[u][N:, N:]


SHORT_LEN = 8
SHORT_ROWS = 4


def _rwkv_short(seqs, states, layer, nb, steps):
    length = SHORT_LEN * pl.cdiv(steps, SHORT_LEN)
    pad = lambda t: jnp.pad(t.reshape(nb, steps, D_MODEL), ((0, 0), (0, length - steps), (0, 0))).reshape(nb * length, D_MODEL)
    bb = _row_tile(nb, SHORT_ROWS)
    spec = pl.BlockSpec((bb * length, D_MODEL), lambda b: (b, 0))
    sspec = pl.BlockSpec((bb, RWKV_H, RWKV_N, RWKV_N), lambda b: (b, 0, 0, 0))
    o, s = pl.pallas_call(
        functools.partial(_rwkv_short_body, bb=bb, length=length), grid=(nb // bb,),
        in_specs=[spec] * 6 + [pl.BlockSpec((1, bb, RWKV_H, RWKV_N, RWKV_N), lambda b: (layer, b, 0, 0, 0))],
        out_specs=[spec, sspec],
        out_shape=[jax.ShapeDtypeStruct((nb * length, D_MODEL), F32),
                   jax.ShapeDtypeStruct((nb, RWKV_H, RWKV_N, RWKV_N), F32)],
        name="rwkv_short",
        compiler_params=pltpu.CompilerParams(dimension_semantics=("parallel",), vmem_limit_bytes=VMEM_LIMIT),
    )(*[pad(t) for t in seqs], states)
    return o.reshape(nb, length, D_MODEL)[:, :steps].reshape(nb * steps, D_MODEL), s


def _rwkv_out_body(is_prompt, tile, x_ref, o_ref, g_ref, bon_ref, lw_ref, lb_ref, e_ref, et_ref, wo_ref, y_ref):
    del is_prompt, tile
    e, et = e_ref[...], et_ref[...]
    o = o_ref[...]
    inv_n = 1.0 / RWKV_N
    d = o - _segment_allsum(o, e, et) * inv_n
    var = _segment_allsum(d * d, e, et) * inv_n
    on = d * lax.rsqrt(var + RWKV_LNX_EPS) * lw_ref[...] + lb_ref[...]
    y_ref[...] = x_ref[...] + _dot((on + bon_ref[...]) * g_ref[...], wo_ref[...])


def _rwkv_layer(x, u, dims, shift_s, wkv_states, layer, p, v_first, vres):
    nb_p, seq_p, nb_s, seq_s = dims
    u_p, u_s = u
    u3s = u_s.reshape(nb_s, seq_s, D_MODEL)
    prev_s = jnp.concatenate([shift_s[:, None, :], u3s[:, :-1]], axis=1).reshape(nb_s * seq_s, D_MODEL)
    has_vres = vres is not None
    e = (jnp.arange(D_MODEL)[:, None] // RWKV_N == jnp.arange(RWKV_H)[None, :]).astype(BF)
    et = e.T
    tm = _row_tile(seq_p, 256)
    rows = [u, (u_p, prev_s)] + ([v_first] if has_vres else [])
    consts = [p['mu'], p['w_rkv'][0].astype(BF), p['w_rkv'][1].astype(BF), p['w_rkv'][2].astype(BF),
              p['w0'][None], p['w1'].astype(BF), p['w2'].astype(BF), p['a0'][None], p['a1'].astype(BF),
              p['a2'].astype(BF), p['g1'].astype(BF), p['g2'].astype(BF), p['k_k'][None], p['k_a'][None],
              p['r_k'].reshape(1, D_MODEL), e, et]
    if has_vres:
        consts += [vres[0][None], vres[1].astype(BF), vres[2].astype(BF)]
    body = functools.partial(_rwkv_proj_body, has_vres=has_vres, tiles_per_seq=seq_p // tm)
    r, lw, k, v, a, b, gate, bonus = _rows_call(body, rows, consts, [(D_MODEL, F32)] * 8, tm, "rwkv_proj", halo=(1,))
    seqs = (r, lw, k, v, a, b)
    o_p, s_p = _rwkv_chunk([t[0] for t in seqs], nb_p, seq_p)
    o_s, s_s = _rwkv_short([t[1] for t in seqs], wkv_states, layer, nb_s, seq_s)
    x = _rows_call(_rwkv_out_body, [x, (o_p, o_s), gate, bonus],
                   [p['lnx_w'][None], p['lnx_b'][None], e, et, p['w_o'].astype(BF)], [(D_MODEL, F32)], 512,
                   "rwkv_out")[0]
    u3p = u_p.reshape(nb_p, seq_p, D_MODEL)
    return x, (v if not has_vres else v_first), (s_p, u3p[:, -1], s_s, u3s[:, -1])


def _t5_bias(dist, rel_bias):
    n = jnp.maximum(dist, 0)
    nf = jnp.maximum(n, 1).astype(F32)
    large = REL_MAX_EXACT + (jnp.log(nf / REL_MAX_EXACT) / math.log(REL_MAX_DIST / REL_MAX_EXACT)
                             * (REL_BUCKETS - REL_MAX_EXACT)).astype(jnp.int32)
    large = jnp.minimum(large, REL_BUCKETS - 1)
    bucket = jnp.where(n < REL_MAX_EXACT, n, large)
    onehot = (bucket[..., None] == jnp.arange(REL_BUCKETS)).astype(F32)
    return jnp.einsum('...b,bh->...h', onehot, rel_bias.astype(F32), precision=lax.Precision.HIGHEST)


def _lam_of(lam_ref, lam_init):
    lv = lam_ref[...]
    return (jnp.exp(jnp.sum(lv[0:1] * lv[1:2], axis=1, keepdims=True))
            - jnp.exp(jnp.sum(lv[2:3] * lv[3:4], axis=1, keepdims=True)) + lam_init)


ATTN_HEADS = 2


def _attn_prompt_body(q_ref, k_ref, v_ref, bias_ref, far_ref, lam_ref, sw_ref, o_ref, kb_sc, vt_sc, *, tq, lam_init):
    qi = pl.program_id(2)
    E, dh = DIFF_E, DIFF_DH
    nkv = kb_sc.shape[0]
    chains = [(hh, c) for hh in range(ATTN_HEADS) for c in range(2)]

    @pl.when(qi == 0)
    def _():
        for j in range(nkv):
            kb_sc[j] = k_ref[j * tq:(j + 1) * tq, :].astype(BF)
            vt_sc[j] = v_ref[j * tq:(j + 1) * tq, :].T.astype(BF)

    q_t = (q_ref[...] * (dh ** -0.5 * LOG2E)).T
    row = lax.broadcasted_iota(jnp.int32, q_t.shape, 0)
    q_c = {(hh, c): jnp.where((row >= hh * E + c * dh) & (row < hh * E + (c + 1) * dh), q_t, 0.0).astype(BF)
           for hh, c in chains}
    kk, qq = _iota2((tq, tq))

    def step(carry, kj, bias=None, shift=None):
        kb = kb_sc[kj]
        s = {ch: jnp.dot(kb, q_c[ch], preferred_element_type=F32) for ch in chains}
        if bias is not None:
            s = {ch: s[ch] + bias[ch[0]] for ch in chains}
        top = {ch: jnp.max(s[ch], axis=0, keepdims=True) for ch in chains}
        if shift is not None:
            top = {ch: top[ch] + shift[ch[0]] for ch in chains}
        m_new = {ch: jnp.maximum(carry[ch][0], top[ch]) for ch in chains}
        alpha = {ch: jnp.exp2(carry[ch][0] - m_new[ch]) for ch in chains}
        offs = m_new if shift is None else {ch: m_new[ch] - shift[ch[0]] for ch in chains}
        pr = {ch: jnp.exp2(s[ch] - offs[ch]) for ch in chains}
        l_new = {ch: carry[ch][1] * alpha[ch] + jnp.sum(pr[ch], axis=0, keepdims=True) for ch in chains}
        vt = vt_sc[kj]
        pv = {(hh, c): jnp.dot(vt[hh * E:(hh + 1) * E, :], pr[(hh, c)].astype(BF), preferred_element_type=F32)
              for hh, c in chains}
        return {ch: (m_new[ch], l_new[ch], carry[ch][2] * alpha[ch] + pv[ch]) for ch in chains}

    init = {ch: (jnp.full((1, tq), NEG, F32), jnp.zeros((1, tq), F32), jnp.zeros((E, tq), F32)) for ch in chains}
    carry = step(init, qi, bias=[jnp.where(kk <= qq, bias_ref[hh, 0], NEG) for hh in range(ATTN_HEADS)])
    carry = step(carry, jnp.maximum(qi - 1, 0),
                 bias=[jnp.where(qi >= 1, bias_ref[hh, 1], NEG) for hh in range(ATTN_HEADS)])
    far = [far_ref[hh, 0:1, :] for hh in range(ATTN_HEADS)]
    carry = lax.fori_loop(0, jnp.maximum(qi - 1, 0), lambda kj, c: step(c, kj, shift=far), carry)
    lam = _lam_of(lam_ref, lam_init)
    for hh in range(ATTN_HEADS):
        (_, l1, acc1), (_, l2, acc2) = carry[(hh, 0)], carry[(hh, 1)]
        o_t = acc1 / l1 - lam * (acc2 / l2)
        o_t = o_t * lax.rsqrt(jnp.mean(o_t * o_t, axis=0, keepdims=True) + SUBLN_EPS) * sw_ref[...]
        o_ref[:, hh * E:(hh + 1) * E] = o_t.T * (1.0 - lam_init)


def _attn_prompt(q, k, v, nb, seq, rel_bias, lam, subln_w, lam_init):
    tq = _row_tile(seq, 256)
    assert tq >= REL_MAX_DIST or tq == seq
    nq = seq // tq
    width = ATTN_HEADS * DIFF_E
    d0 = jnp.arange(tq)[None, :] - jnp.arange(tq)[:, None]
    bias = jnp.stack([_t5_bias(d0, rel_bias), _t5_bias(d0 + tq, rel_bias)])
    bias = jnp.transpose(bias, (3, 0, 1, 2)) * LOG2E
    far = jnp.broadcast_to(rel_bias[REL_BUCKETS - 1][:, None, None] * LOG2E, (DIFF_H, 8, tq)).astype(F32)
    qspec = pl.BlockSpec((tq, width), lambda h, b, i: (b * nq + i, h))
    kspec = pl.BlockSpec((seq, width), lambda h, b, i: (b, h))
    return pl.pallas_call(
        functools.partial(_attn_prompt_body, tq=tq, lam_init=lam_init), grid=(DIFF_H // ATTN_HEADS, nb, nq),
        in_specs=[qspec, kspec, kspec,
                  pl.BlockSpec((ATTN_HEADS, 2, tq, tq), lambda h, b, i: (h, 0, 0, 0)),
                  pl.BlockSpec((ATTN_HEADS, 8, tq), lambda h, b, i: (h, 0, 0)),
                  pl.BlockSpec((4, DIFF_DH), lambda h, b, i: (0, 0)),
                  pl.BlockSpec((DIFF_E, 1), lambda h, b, i: (0, 0))],
        out_specs=qspec, out_shape=jax.ShapeDtypeStruct((nb * seq, D_MODEL), F32),
        scratch_shapes=[pltpu.VMEM((nq, tq, width), BF), pltpu.VMEM((nq, width, tq), BF)], name="attn_prompt",
        compiler_params=pltpu.CompilerParams(dimension_semantics=("parallel", "parallel", "arbitrary"),
                                             vmem_limit_bytes=VMEM_LIMIT),
    )(q, k, v, bias, far, lam, subln_w[:, None])


def _attn_sample_body(pt_ref, q_ref, *refs, n_pages, lam_init):
    del pt_ref
    k_refs, v_refs = refs[:n_pages], refs[n_pages:2 * n_pages]
    kn_ref, vn_ref, bias_ref, lam_ref, sw_ref, o_ref = refs[2 * n_pages:]
    page = k_refs[0].shape[0] // DIFF_H
    steps = o_ref.shape[1]
    heads = range(DIFF_H)
    lam = _lam_of(lam_ref, lam_init)
    sw = sw_ref[...]
    zpad = jnp.zeros((page - kn_ref.shape[2], DIFF_E), BF)

    def head_rows(ref, h):
        return ref[pl.ds(h, page, stride=DIFF_H), :].astype(BF)

    def new_rows(ref, h):
        return jnp.concatenate([ref[0, h].astype(BF), zpad], axis=0)

    scores = [[_dot_nt(q_ref[0, h], head_rows(k_refs[i], h)) for i in range(n_pages)]
              + [_dot_nt(q_ref[0, h], new_rows(kn_ref, h))] for h in heads]
    probs, denom = [], []
    for h in heads:
        sc = jnp.concatenate(scores[h], axis=1) + bias_ref[h]
        pr = jnp.exp(sc - jnp.max(sc, axis=1, keepdims=True))
        denom.append(jnp.sum(pr, axis=1, keepdims=True))
        probs.append(pr.astype(BF))
    acc = []
    for h in heads:
        parts = [jnp.dot(probs[h][:, i * page:(i + 1) * page], head_rows(v_refs[i], h), preferred_element_type=F32)
                 for i in range(n_pages)]
        parts.append(jnp.dot(probs[h][:, n_pages * page:], new_rows(vn_ref, h), preferred_element_type=F32))
        acc.append(functools.reduce(jnp.add, parts))
    for h in heads:
        z = acc[h] / denom[h]
        o = z[:steps] - lam * z[steps:]
        o_ref[0, :, h * DIFF_E:(h + 1) * DIFF_E] = _rms(o, sw, SUBLN_EPS) * (1.0 - lam_init)


def _attn_sample(q, k_new, v_new, cache_k, cache_v, layer, page_table, rel_bias, lam, subln_w, lam_init):
    nb, n_pages = page_table.shape
    page = cache_k.shape[2]
    page_ids = page_table + layer * cache_k.shape[1]
    steps = q.shape[0] // nb
    past = n_pages * page
    width = DIFF_H * DIFF_E
    nrow = 2 * steps
    assert nrow % 8 == 0 and steps <= page
    q4 = jnp.transpose((q * (DIFF_DH ** -0.5)).reshape(nb, steps, DIFF_H, DIFF_E), (0, 2, 1, 3))
    first = jnp.arange(DIFF_E) < DIFF_DH
    q8 = jnp.concatenate([jnp.where(first, q4, 0.0), jnp.where(first, 0.0, q4)], axis=2).astype(BF)
    new_rows = lambda t: jnp.pad(jnp.transpose(t.reshape(nb, steps, DIFF_H, DIFF_E), (0, 2, 1, 3)),
                                 ((0, 0), (0, 0), (0, nrow - steps), (0, 0)))
    q_pos = past + jnp.arange(steps)
    key_pos = jnp.arange(past + steps)
    bias = _t5_bias(q_pos[None, :] - key_pos[:, None], rel_bias)
    bias = jnp.where((key_pos[:, None] <= q_pos[None, :])[..., None], bias, NEG)
    bias = jnp.pad(jnp.transpose(bias, (2, 1, 0)), ((0, 0), (0, 0), (0, page - steps)), constant_values=NEG)
    bias = jnp.concatenate([bias, bias], axis=1)
    rows = page * DIFF_H
    page_specs = [pl.BlockSpec((rows, DIFF_E), lambda b, pt, i=i: (pt[b, i], 0)) for i in range(n_pages)]
    head_spec = pl.BlockSpec((1, DIFF_H, nrow, DIFF_E), lambda b, pt: (b, 0, 0, 0))
    grid_spec = pltpu.PrefetchScalarGridSpec(
        num_scalar_prefetch=1, grid=(nb,),
        in_specs=[head_spec] + page_specs + page_specs + [
            head_spec, head_spec,
            pl.BlockSpec((DIFF_H, nrow, past + page), lambda b, pt: (0, 0, 0)),
            pl.BlockSpec((4, DIFF_DH), lambda b, pt: (0, 0)),
            pl.BlockSpec((1, DIFF_E), lambda b, pt: (0, 0))],
        out_specs=pl.BlockSpec((1, steps, width), lambda b, pt: (b, 0, 0)))
    ck = cache_k.reshape(-1, DIFF_E)
    cv = cache_v.reshape(-1, DIFF_E)
    o = pl.pallas_call(
        functools.partial(_attn_sample_body, n_pages=n_pages, lam_init=lam_init), grid_spec=grid_spec,
        out_shape=jax.ShapeDtypeStruct((nb, steps, width), F32), name="attn_sample",
        compiler_params=pltpu.CompilerParams(dimension_semantics=("parallel",), vmem_limit_bytes=VMEM_LIMIT),
    )(page_ids, q8, *([ck] * n_pages), *([cv] * n_pages), new_rows(k_new), new_rows(v_new), bias, lam,
      subln_w[None])
    return o.reshape(nb * steps, width)


def _attn_layer(x, u, dims, cache_k, cache_v, layer, page_table, p, lam_init, rel_bias):
    nb_p, seq_p, nb_s, seq_s = dims
    (q_p, q_s), (k_p, k_s), (v_p, v_s) = _linear(u, p['w_qkv'], 3, "attn_qkv")
    o_p = _attn_prompt(q_p, k_p, v_p, nb_p, seq_p, rel_bias, p['lam'], p['subln_w'], lam_init)
    o_s = _attn_sample(q_s, k_s, v_s, cache_k, cache_v, layer, page_table, rel_bias, p['lam'], p['subln_w'],
                       lam_init)
    x = _linear_res(x, (o_p, o_s), p['w_o'], "attn_out")
    shp_p = (nb_p, seq_p, DIFF_H, DIFF_E)
    shp_s = (nb_s, seq_s, DIFF_H, DIFF_E)
    return x, (k_p.reshape(shp_p), v_p.reshape(shp_p), k_s.reshape(shp_s), v_s.reshape(shp_s))


def _gdn_proj_body(is_prompt, tile, u_ref, wqkv_ref, wz_ref, wb_ref, wa_ref, al_ref, dt_ref,
                   qkv_o, z_o, beta_o, g_o):
    del is_prompt, tile
    u = u_ref[...].astype(BF)
    qkv_o[...] = jnp.dot(u, wqkv_ref[...], preferred_element_type=F32)
    z_o[...] = jnp.dot(u, wz_ref[...], preferred_element_type=F32)
    beta = _sigmoid(jnp.dot(u, wb_ref[...], preferred_element_type=F32))
    g = -jnp.exp(al_ref[...]) * _softplus(jnp.dot(u, wa_ref[...], preferred_element_type=F32) + dt_ref[...])
    for h in range(GDN_H):
        cols = slice(h * GDN_DV, (h + 1) * GDN_DV)
        beta_o[:, cols] = jnp.broadcast_to(beta[:, h:h + 1], (beta.shape[0], GDN_DV))
        g_o[:, cols] = jnp.broadcast_to(g[:, h:h + 1], (g.shape[0], GDN_DV))


def _gdn_post(conv, q_o, k_o, v_o):
    c = conv * _sigmoid(conv)
    hk = GDN_H * GDN_DK
    for h in range(GDN_H):
        qs = c[:, h * GDN_DK:(h + 1) * GDN_DK]
        ks = c[:, hk + h * GDN_DK:hk + (h + 1) * GDN_DK]
        q_o[:, h * GDN_DK:(h + 1) * GDN_DK] = qs * lax.rsqrt(jnp.sum(qs * qs, axis=1, keepdims=True) + 1e-6) * (GDN_DK ** -0.5)
        k_o[:, h * GDN_DK:(h + 1) * GDN_DK] = ks * lax.rsqrt(jnp.sum(ks * ks, axis=1, keepdims=True) + 1e-6)
    v_o[...] = c[:, 2 * hk:]


def _gdn_conv_prompt_body(x_ref, halo_ref, w_ref, q_o, k_o, v_o, *, tiles_per_seq):
    i = pl.program_id(0)
    x = x_ref[...]
    tm = x.shape[0]
    halo = jnp.where(i % tiles_per_seq == 0, 0.0, halo_ref[...])
    row8 = lax.broadcasted_iota(jnp.int32, (8, x.shape[1]), 0)
    w = w_ref[...]
    conv = w[GDN_CONV - 1:GDN_CONV] * x
    for j in range(GDN_CONV - 1):
        sh = GDN_CONV - 1 - j
        rolled = pltpu.roll(x, sh, 0)
        head = jnp.where(row8 < sh, pltpu.roll(halo, sh, 0), rolled[:8])
        tap = jnp.concatenate([head, rolled[8:]], axis=0) if tm > 8 else head
        conv = conv + w[j:j + 1] * tap
    _gdn_post(conv, q_o, k_o, v_o)


def _gdn_conv_taps_body(t0_ref, t1_ref, t2_ref, t3_ref, w_ref, q_o, k_o, v_o):
    w = w_ref[...]
    conv = w[0:1] * t0_ref[...] + w[1:2] * t1_ref[...] + w[2:3] * t2_ref[...] + w[3:4] * t3_ref[...]
    _gdn_post(conv, q_o, k_o, v_o)


def _gdn_advance(L, refs, rows, cols, st):
    q_ref, k_ref, v_ref, beta_ref, g_ref = refs
    P = LANES
    ii, jj = _iota2((L, L))
    strict = ii > jj
    incl = ii >= jj
    us = range(len(st))
    at = lambda ref, u: ref[rows[u], cols[u]]
    gc = [_cumsum_rows(at(g_ref, u)) for u in us]
    q = [at(q_ref, u) for u in us]
    k = [at(k_ref, u) for u in us]
    kk = [_dot_nt(k[u], k[u]) for u in us]
    qk = [_dot_nt(q[u], k[u]) for u in us]
    n_mats, dec_i, e_gc, rhs = [], [], [], []
    for u in us:
        diff = gc[u][:, :L] - gc[u].T[:L, :]
        beta = at(beta_ref, u)
        n_mats.append(-(beta[:, :L] * kk[u] * jnp.exp(jnp.where(strict, diff, NEG))))
        dec_i.append(jnp.exp(jnp.where(incl, diff, NEG)))
        e_gc.append(jnp.exp(gc[u]))
        rhs.append(jnp.concatenate([beta * at(v_ref, u), beta * e_gc[u] * k[u]], axis=1))
    side = GDN_INV_SIDE
    assert len(st) % side == 0
    t_rows = _tri_inv([jnp.concatenate(n_mats[c:c + side], axis=1) for c in range(0, len(st), side)], L)
    t_inv = [t_rows[u // side][:, (u % side) * L:(u % side + 1) * L] for u in us]
    w = [_dot(t_inv[u], rhs[u]) for u in us]
    uu = [w[u][:, :P] - _dot(w[u][:, P:], st[u]) for u in us]
    qs = [_dot(q[u] * e_gc[u], st[u]) for u in us]
    intra = [_dot(qk[u] * dec_i[u], uu[u]) for u in us]
    upd = [_dot_tn(k[u] * jnp.exp(gc[u][L - 1:L, :] - gc[u]), uu[u]) for u in us]
    return ([qs[u] + intra[u] for u in us], [jnp.exp(gc[u][L - 1:L, :]) * st[u] + upd[u] for u in us])


def _gdn_chunk_body(q_ref, k_ref, v_ref, beta_ref, g_ref, o_ref, s_ref, st_sc, *, tb, groups):
    L, P = CHUNK, LANES
    t = pl.program_id(2)

    @pl.when(t == 0)
    def _():
        st_sc[...] = jnp.zeros(st_sc.shape, F32)

    refs = (q_ref, k_ref, v_ref, beta_ref, g_ref)
    cols = [slice(g * P, (g + 1) * P) for g in range(groups)]

    def chunk(c, carry):
        rows = pl.ds(pl.multiple_of(c * L, L), L)
        outs, new = _gdn_advance(L, refs, [rows] * groups, cols, [st_sc[g] for g in range(groups)])
        for g in range(groups):
            o_ref[rows, cols[g]] = outs[g]
            st_sc[g] = new[g]
        return carry

    lax.fori_loop(0, tb // L, chunk, 0, unroll=2)

    @pl.when(t == pl.num_programs(2) - 1)
    def _():
        for g in range(groups):
            s_ref[0, g] = st_sc[g]


def _gdn_short_body(q_ref, k_ref, v_ref, beta_ref, g_ref, s0_ref, o_ref, s_ref, *, bb, length):
    P = LANES
    units = [(b_i, h) for b_i in range(bb) for h in range(GDN_H)]
    rows = [slice(b_i * length, (b_i + 1) * length) for b_i, _ in units]
    cols = [slice(h * P, (h + 1) * P) for _, h in units]
    outs, new = _gdn_advance(length, (q_ref, k_ref, v_ref, beta_ref, g_ref), rows, cols,
                             [s0_ref[0, b_i, h] for b_i, h in units])
    for u, (b_i, h) in enumerate(units):
        o_ref[rows[u], cols[u]] = outs[u]
        s_ref[b_i, h] = new[u]


def _gdn_out_body(is_prompt, tile, x_ref, o_ref, z_ref, nw_ref, wo_ref, y_ref):
    del is_prompt, tile
    z = z_ref[...]
    nw = nw_ref[...]
    o = o_ref[...]
    parts = []
    for h in range(GDN_H):
        sl = slice(h * GDN_DV, (h + 1) * GDN_DV)
        parts.append(_rms(o[:, sl], nw, NORM_EPS))
    on = jnp.concatenate(parts, axis=1) * (z * _sigmoid(z))
    y_ref[...] = x_ref[...] + _dot(on, wo_ref[...])


def _gdn_layer(x, u, dims, conv_s, states, layer, p):
    nb_p, seq_p, nb_s, seq_s = dims
    n_p, n_s = nb_p * seq_p, nb_s * seq_s
    hv = GDN_H * GDN_DV
    w_in = p['w_in']
    pad = lambda w: jnp.pad(w, ((0, 0), (0, LANES - w.shape[-1])))
    (qkv_p, qkv_s), z, (beta_p, beta_s), (g_p, g_s) = _rows_call(
        _gdn_proj_body, [u],
        [w_in[:, :GDN_QKV].astype(BF), w_in[:, GDN_QKV:GDN_QKV + hv].astype(BF),
         pad(w_in[:, GDN_QKV + hv:GDN_QKV + hv + GDN_H]).astype(BF), pad(w_in[:, GDN_QKV + hv + GDN_H:]).astype(BF),
         pad(p['a_log'][None]), pad(p['dt_bias'][None])],
        [(GDN_QKV, F32), (hv, F32), (hv, F32), (hv, F32)], 256, "gdn_proj")
    tm = _row_tile(seq_p, 256)
    cw = p['conv_w']
    outs3 = [jax.ShapeDtypeStruct((n_p, hv), F32)] * 3
    q_p, k_p, v_p = pl.pallas_call(
        functools.partial(_gdn_conv_prompt_body, tiles_per_seq=seq_p // tm), grid=(n_p // tm,),
        in_specs=[pl.BlockSpec((tm, GDN_QKV), lambda i: (i, 0)),
                  pl.BlockSpec((8, GDN_QKV), lambda i: (jnp.maximum(i * (tm // 8) - 1, 0), 0)),
                  pl.BlockSpec((GDN_CONV, GDN_QKV), lambda i: (0, 0))],
        out_specs=[pl.BlockSpec((tm, hv), lambda i: (i, 0))] * 3, out_shape=outs3, name="gdn_conv_prompt",
        compiler_params=pltpu.CompilerParams(dimension_semantics=("parallel",), vmem_limit_bytes=VMEM_LIMIT),
    )(qkv_p, qkv_p, cw)
    xp_s = jnp.concatenate([conv_s, qkv_s.reshape(nb_s, seq_s, GDN_QKV)], axis=1)
    taps = [xp_s[:, j:j + seq_s].reshape(n_s, GDN_QKV) for j in range(GDN_CONV)]
    ts = _row_tile(n_s, 256)
    q_s, k_s, v_s = pl.pallas_call(
        _gdn_conv_taps_body, grid=(n_s // ts,),
        in_specs=[pl.BlockSpec((ts, GDN_QKV), lambda i: (i, 0))] * GDN_CONV
        + [pl.BlockSpec((GDN_CONV, GDN_QKV), lambda i: (0, 0))],
        out_specs=[pl.BlockSpec((ts, hv), lambda i: (i, 0))] * 3,
        out_shape=[jax.ShapeDtypeStruct((n_s, hv), F32)] * 3, name="gdn_conv_sample",
        compiler_params=pltpu.CompilerParams(dimension_semantics=("parallel",), vmem_limit_bytes=VMEM_LIMIT),
    )(*taps, cw)
    groups = GDN_GROUPS
    tb = _row_tile(seq_p, SEQ_BLOCK)
    nt = seq_p // tb
    spec = pl.BlockSpec((tb, groups * LANES), lambda b, h, t: (b * nt + t, h))
    o_p, s_p = pl.pallas_call(
        functools.partial(_gdn_chunk_body, tb=tb, groups=groups), grid=(nb_p, GDN_H // groups, nt),
        in_specs=[spec] * 5,
        out_specs=[spec, pl.BlockSpec((1, groups, GDN_DK, GDN_DV), lambda b, h, t: (b, h, 0, 0))],
        out_shape=[jax.ShapeDtypeStruct((n_p, hv), F32), jax.ShapeDtypeStruct((nb_p, GDN_H, GDN_DK, GDN_DV), F32)],
        scratch_shapes=[pltpu.VMEM((groups, GDN_DK, GDN_DV), F32)],
        name="gdn_chunk",
        compiler_params=pltpu.CompilerParams(dimension_semantics=("parallel", "parallel", "arbitrary"),
                                             vmem_limit_bytes=VMEM_LIMIT),
    )(q_p, k_p, v_p, beta_p, g_p)
    length = SHORT_LEN * pl.cdiv(seq_s, SHORT_LEN)
    pad_rows = lambda t: jnp.pad(t.reshape(nb_s, seq_s, hv), ((0, 0), (0, length - seq_s), (0, 0))).reshape(nb_s * length, hv)
    bb = _row_tile(nb_s, SHORT_ROWS)
    sspec = pl.BlockSpec((bb * length, hv), lambda b: (b, 0))
    stspec = pl.BlockSpec((bb, GDN_H, GDN_DK, GDN_DV), lambda b: (b, 0, 0, 0))
    o_s, s_s = pl.pallas_call(
        functools.partial(_gdn_short_body, bb=bb, length=length), grid=(nb_s // bb,),
        in_specs=[sspec] * 5 + [pl.BlockSpec((1, bb, GDN_H, GDN_DK, GDN_DV), lambda b: (layer, b, 0, 0, 0))],
        out_specs=[sspec, stspec],
        out_shape=[jax.ShapeDtypeStruct((nb_s * length, hv), F32),
                   jax.ShapeDtypeStruct((nb_s, GDN_H, GDN_DK, GDN_DV), F32)],
        name="gdn_short",
        compiler_params=pltpu.CompilerParams(dimension_semantics=("parallel",), vmem_limit_bytes=VMEM_LIMIT),
    )(pad_rows(q_s), pad_rows(k_s), pad_rows(v_s), pad_rows(beta_s), pad_rows(g_s), states)
    o_s = o_s.reshape(nb_s, length, hv)[:, :seq_s].reshape(n_s, hv)
    x = _rows_call(_gdn_out_body, [x, (o_p, o_s), z], [p['norm_w'][None], p['w_o'].astype(BF)],
                   [(D_MODEL, F32)], 512, "gdn_out")[0]
    keep = GDN_CONV - 1
    tail_p = qkv_p.reshape(nb_p, seq_p, GDN_QKV)[:, -keep:]
    conv_p = jnp.pad(tail_p, ((0, 0), (max(keep - seq_p, 0), 0), (0, 0)))
    return x, (s_p, conv_p, s_s, xp_s[:, -keep:])


def kernel(x_prompt, x_sample, state_rwkv_wkv, state_rwkv_shift, cache_attn_k, cache_attn_v, state_gdn, state_gdn_conv, page_table, norm_w, final_norm_w, ffn_w_gate, ffn_w_up, ffn_w_down, rwkv_mu, rwkv_w_rkv, rwkv_w_o, rwkv_w0, rwkv_w1, rwkv_w2, rwkv_a0, rwkv_a1, rwkv_a2, rwkv_g1, rwkv_g2, rwkv_k_k, rwkv_k_a, rwkv_r_k, rwkv_lnx_w, rwkv_lnx_b, rwkv_v0, rwkv_v1, rwkv_v2, attn_w_qkv, attn_w_o, attn_lambda, attn_subln_w, rel_bias, gdn_w_in, gdn_conv_w, gdn_a_log, gdn_dt_bias, gdn_norm_w, gdn_w_o):
    nb_p, seq_p, _ = x_prompt.shape
    nb_s, seq_s, _ = x_sample.shape
    dims = (nb_p, seq_p, nb_s, seq_s)
    depth = norm_w.shape[0]
    x = (x_prompt.reshape(nb_p * seq_p, D_MODEL), x_sample.reshape(nb_s * seq_s, D_MODEL))
    v_first = None
    rw, at, gd = [], [], []
    for i in range(depth):
        kind, j = i % 3, i // 3
        x, u = _ffn(x, norm_w[i, 0], ffn_w_gate[i, 0], ffn_w_up[i, 0], ffn_w_down[i, 0], norm_w[i, 1])
        if kind == 0:
            p = dict(mu=rwkv_mu[j], w_rkv=rwkv_w_rkv[j], w_o=rwkv_w_o[j], w0=rwkv_w0[j], w1=rwkv_w1[j],
                     w2=rwkv_w2[j], a0=rwkv_a0[j], a1=rwkv_a1[j], a2=rwkv_a2[j], g1=rwkv_g1[j], g2=rwkv_g2[j],
                     k_k=rwkv_k_k[j], k_a=rwkv_k_a[j], r_k=rwkv_r_k[j], lnx_w=rwkv_lnx_w[j], lnx_b=rwkv_lnx_b[j])
            vres = None if j == 0 else (rwkv_v0[j - 1], rwkv_v1[j - 1], rwkv_v2[j - 1])
            x, v_first, st = _rwkv_layer(x, u, dims, state_rwkv_shift[j], state_rwkv_wkv, j, p, v_first, vres)
            rw.append(st)
        elif kind == 1:
            p = dict(w_qkv=attn_w_qkv[j], w_o=attn_w_o[j], lam=attn_lambda[j], subln_w=attn_subln_w[j])
            lam_init = 0.8 - 0.6 * math.exp(-0.3 * i)
            x, st = _attn_layer(x, u, dims, cache_attn_k, cache_attn_v, j, page_table, p, lam_init, rel_bias)
            at.append(st)
        else:
            p = dict(w_in=gdn_w_in[j], conv_w=gdn_conv_w[j], a_log=gdn_a_log[j], dt_bias=gdn_dt_bias[j],
                     norm_w=gdn_norm_w[j], w_o=gdn_w_o[j])
            x, st = _gdn_layer(x, u, dims, state_gdn_conv[j], state_gdn, j, p)
            gd.append(st)
        if i == depth - 1:
            x, y = _ffn(x, norm_w[i, 2], ffn_w_gate[i, 1], ffn_w_up[i, 1], ffn_w_down[i, 1], final_norm_w)
        else:
            x = _ffn(x, norm_w[i, 2], ffn_w_gate[i, 1], ffn_w_up[i, 1], ffn_w_down[i, 1])
    stack = lambda lst, k: lst[0][k][None] if len(lst) == 1 else jnp.stack([t[k] for t in lst])
    return (y[0].reshape(x_prompt.shape), y[1].reshape(x_sample.shape),
            stack(rw, 0), stack(rw, 1), stack(at, 0), stack(at, 1), stack(gd, 0), stack(gd, 1),
            stack(rw, 2), stack(rw, 3), stack(at, 2), stack(at, 3), stack(gd, 2), stack(gd, 3))
```
